```python
import math
import jax, jax.numpy as jnp
from jax import lax
import numpy as np

D_MODEL = 2048
BATCH = 4
SEQ = 2048
DEPTH = 4
DEC_BATCH = 8
DEC_SEQ = 1
PAST_LEN = 16384
PAGE_SIZE = 128

N_MIXERS = 2
N_ATTN_LAYERS = (DEPTH + 1) // 2
N_SSM_LAYERS = DEPTH // 2
HEAD_DIM = 128
HEADS_PER_GROUP = D_MODEL // (2 * HEAD_DIM)
DIL_GROUPS = ((128, 1), (512, 4), (2048, 16))
N_DIL = len(DIL_GROUPS)
ROT_DIM = HEAD_DIM // 4
ROPE_THETA = 500000.0
Q_BLOCK = 128
NEG_INF = -1e30
D_SSM = D_MODEL
SSM_CH = 16
SSM_GROUPS = D_SSM // SSM_CH
SSM_STATE = 64
DT_MIN = 0.001
DT_MAX = 0.1
D_FF = ((8 * D_MODEL // 3 + 127) // 128) * 128
CONV_WIDTH = 3
RMS_EPS = 1e-6

kernel_name = "dilated_swa_s5_convffn_hybrid_step"


def rms_norm(x, g):
    xf = x.astype(jnp.float32)
    y = xf * lax.rsqrt(jnp.mean(xf * xf, axis=-1, keepdims=True) + RMS_EPS)
    return (y * g.astype(jnp.float32)).astype(x.dtype)


def rope_partial(x, pos):
    half = ROT_DIM // 2
    inv = ROPE_THETA ** (-(jnp.arange(half, dtype=jnp.float32) * (2.0 / ROT_DIM)))
    ang = pos.astype(jnp.float32)[:, None] * inv[None, :]
    cos = jnp.cos(ang)[None, :, None, None, :]
    sin = jnp.sin(ang)[None, :, None, None, :]
    xf = x.astype(jnp.float32)
    x1, x2 = xf[..., :half], xf[..., half:ROT_DIM]
    out = jnp.concatenate([x1 * cos - x2 * sin, x2 * cos + x1 * sin, xf[..., ROT_DIM:]], axis=-1)
    return out.astype(x.dtype)


def qkv_project(h, w_qkv, pos):
    B, T, _ = h.shape
    qkv = (h @ w_qkv).reshape(B, T, 3, N_DIL, HEADS_PER_GROUP, HEAD_DIM)
    q = rope_partial(qkv[:, :, 0], pos)
    k = rope_partial(qkv[:, :, 1], pos)
    v = qkv[:, :, 2]
    return q, k, v


def dilated_attend(q, k_all, v_all, q_idx, dilation, n_back):
    key_idx = q_idx[:, None] - dilation * jnp.arange(n_back + 1)[None, :]
    valid = key_idx >= 0
    key_idx = jnp.maximum(key_idx, 0)
    kg = jnp.take(k_all, key_idx, axis=1)
    vg = jnp.take(v_all, key_idx, axis=1)
    s = jnp.einsum('bqhd,bqkhd->bqhk', q, kg, preferred_element_type=jnp.float32) * (HEAD_DIM ** -0.5)
    s = jnp.where(valid[None, :, None, :], s, NEG_INF)
    m = jnp.max(s, axis=-1, keepdims=True)
    p = jnp.exp(s - m)
    l = jnp.sum(p, axis=-1, keepdims=True)
    o = jnp.einsum('bqhk,bqkhd->bqhd', p, vg.astype(jnp.float32)) / jnp.swapaxes(l, -1, -2)[..., 0][..., None]
    lse = (m + jnp.log(l))[..., 0]
    return o, lse


def merge_by_denominator(outs, lses):
    w = jax.nn.softmax(jnp.stack(lses, axis=0), axis=0)
    return jnp.sum(w[..., None] * jnp.stack(outs, axis=0), axis=0)


def attn_prompt(h, w_qkv, w_o):
    B, T, _ = h.shape
    q, k, v = qkv_project(h, w_qkv, jnp.arange(T))
    qs = [q[:, :, g] for g in range(N_DIL)]
    ks = [k[:, :, g] for g in range(N_DIL)]
    vs = [v[:, :, g] for g in range(N_DIL)]

    def block(start):
        idx = start + jnp.arange(Q_BLOCK)
        outs, lses = [], []
        for g, (win, dil) in enumerate(DIL_GROUPS):
            qb = lax.dynamic_slice_in_dim(qs[g], start, Q_BLOCK, axis=1)
            o, lse = dilated_attend(qb, ks[g], vs[g], idx, dil, win // dil)
            outs.append(o)
            lses.append(lse)
        return merge_by_denominator(outs, lses)

    o = lax.map(block, jnp.arange(0, T, Q_BLOCK))
    o = jnp.moveaxis(o, 0, 1).reshape(B, T, HEADS_PER_GROUP * HEAD_DIM).astype(h.dtype)
    new_kv = []
    for g, (win, _) in enumerate(DIL_GROUPS):
        keep = min(win, T)
        new_kv.append(jnp.stack([ks[g][:, T - keep:], vs[g][:, T - keep:]], axis=2))
    return o @ w_o, new_kv


def attn_sample(h, kv_bufs, w_qkv, w_o):
    B, S, _ = h.shape
    q, k, v = qkv_project(h, w_qkv, PAST_LEN + jnp.arange(S))
    outs, lses, new_rows = [], [], []
    for g, (win, dil) in enumerate(DIL_GROUPS):
        buf = kv_bufs[g]
        L = buf.shape[1]
        k_all = jnp.concatenate([buf[:, :, 0].astype(k.dtype), k[:, :, g]], axis=1)
        v_all = jnp.concatenate([buf[:, :, 1].astype(v.dtype), v[:, :, g]], axis=1)
        o, lse = dilated_attend(q[:, :, g], k_all, v_all, L + jnp.arange(S), dil, win // dil)
        outs.append(o)
        lses.append(lse)
        new_rows.append(jnp.stack([k[:, :, g], v[:, :, g]], axis=2))
    o = merge_by_denominator(outs, lses).reshape(B, S, HEADS_PER_GROUP * HEAD_DIM).astype(h.dtype)
    return o @ w_o, new_rows


def _linear_recurrence_combine(e1, e2):
    a1, b1 = e1
    a2, b2 = e2
    return a1 * a2, a2 * b1 + b2


def s5_mixer(h, h0, w_in, lam_re, lam_im, log_dt, b_re, b_im, c_re, c_im, d_skip, w_glu):
    B, T, _ = h.shape
    f32 = jnp.float32
    u = (h @ w_in).astype(f32)
    ug = u.reshape(B, T, SSM_GROUPS, SSM_CH)
    lam = lax.complex(jnp.minimum(lam_re.astype(f32), -1e-4), lam_im.astype(f32))
    dt = jnp.exp(log_dt.astype(f32))[:, None]
    a_bar = jnp.exp(lam * dt)
    b = lax.complex(b_re.astype(f32), b_im.astype(f32))
    b_bar = ((a_bar - 1.0) / lam)[..., None] * b
    c = lax.complex(c_re.astype(f32), c_im.astype(f32))
    bu = jnp.einsum('gpc,btgc->btgp', b_bar, ug.astype(jnp.complex64))
    h0c = lax.complex(h0[..., 0].astype(f32), h0[..., 1].astype(f32))
    bu = bu.at[:, 0].add(a_bar[None] * h0c)
    a = jnp.broadcast_to(a_bar, bu.shape)
    _, xs = lax.associative_scan(_linear_recurrence_combine, (a, bu), axis=1)
    y = jnp.einsum('gcp,btgp->btgc', c, xs).real.reshape(B, T, D_SSM) + d_skip.astype(f32) * u
    z = jax.nn.gelu(y).astype(h.dtype)
    ga = z @ w_glu
    out = ga[..., :D_MODEL] * jax.nn.sigmoid(ga[..., D_MODEL:])
    x_last = xs[:, -1]
    new_state = jnp.stack([x_last.real, x_last.imag], axis=-1).astype(h0.dtype)
    return out, new_state


def conv_ffn(h, conv_state, w_up, conv_w, conv_b, w_down):
    T = h.shape[1]
    u = h @ w_up
    up = jnp.concatenate([conv_state.astype(u.dtype), u], axis=1)
    c = conv_b
    for j in range(CONV_WIDTH):
        c = c + conv_w[j] * up[:, j:j + T]
    gate, val = c[..., :D_FF], c[..., D_FF:]
    out = (jax.nn.silu(gate) * val) @ w_down
    return out, up[:, -(CONV_WIDTH - 1):]


def setup_inputs(seed: int = 0) -> dict:
    key = jax.random.key(seed)
    ks = iter(jax.random.split(key, 32))

    def nrm(shape, scale=1.0):
        return scale * jax.random.normal(next(ks), shape, jnp.float32)

    attn_w = HEADS_PER_GROUP * HEAD_DIM
    qkv_cols = 3 * N_DIL * attn_w
    inp = {}
    inp["x_prompt"] = nrm((BATCH, SEQ, D_MODEL))
    inp["x_sample"] = nrm((DEC_BATCH, DEC_SEQ, D_MODEL))
    for g, (win, _) in enumerate(DIL_GROUPS):
        inp["cache_kv_g%d" % g] = nrm((N_ATTN_LAYERS, DEC_BATCH, min(win, PAST_LEN), 2, HEADS_PER_GROUP, HEAD_DIM))
    inp["state_ssm"] = nrm((N_SSM_LAYERS, DEC_BATCH, SSM_GROUPS, SSM_STATE, 2))
    inp["state_conv"] = nrm((DEPTH, DEC_BATCH, CONV_WIDTH - 1, 2 * D_FF))
    inp["norm_g"] = 1.0 + nrm((DEPTH, 4, D_MODEL), 0.05)
    inp["w_qkv"] = nrm((N_ATTN_LAYERS, D_MODEL, qkv_cols), D_MODEL ** -0.5)
    inp["w_attn_o"] = nrm((N_ATTN_LAYERS, attn_w, D_MODEL), attn_w ** -0.5)
    inp["w_ssm_in"] = nrm((N_SSM_LAYERS, D_MODEL, D_SSM), D_MODEL ** -0.5)
    inp["lambda_re"] = -0.5 + nrm((N_SSM_LAYERS, SSM_GROUPS, SSM_STATE), 0.01)
    inp["lambda_im"] = jnp.pi * jnp.arange(SSM_STATE, dtype=jnp.float32) + nrm((N_SSM_LAYERS, SSM_GROUPS, SSM_STATE), 0.01)
    inp["log_dt"] = jax.random.uniform(next(ks), (N_SSM_LAYERS, SSM_GROUPS), jnp.float32, math.log(DT_MIN), math.log(DT_MAX))
    inp["b_re"] = nrm((N_SSM_LAYERS, SSM_GROUPS, SSM_STATE, SSM_CH), (2 * SSM_CH) ** -0.5)
    inp["b_im"] = nrm((N_SSM_LAYERS, SSM_GROUPS, SSM_STATE, SSM_CH), (2 * SSM_CH) ** -0.5)
    inp["c_re"] = nrm((N_SSM_LAYERS, SSM_GROUPS, SSM_CH, SSM_STATE), (2 * SSM_STATE) ** -0.5)
    inp["c_im"] = nrm((N_SSM_LAYERS, SSM_GROUPS, SSM_CH, SSM_STATE), (2 * SSM_STATE) ** -0.5)
    inp["d_skip"] = nrm((N_SSM_LAYERS, D_SSM))
    inp["w_glu"] = nrm((N_SSM_LAYERS, D_SSM, 2 * D_MODEL), D_SSM ** -0.5)
    inp["w_up"] = nrm((DEPTH, D_MODEL, 2 * D_FF), D_MODEL ** -0.5)
    inp["conv_w"] = nrm((DEPTH, CONV_WIDTH, 2 * D_FF), CONV_WIDTH ** -0.5)
    inp["conv_b"] = nrm((DEPTH, 2 * D_FF), 0.01)
    inp["w_down"] = nrm((DEPTH, D_FF, D_MODEL), D_FF ** -0.5)
    return inp


def reference(x_prompt, x_sample, cache_kv_g0, cache_kv_g1, cache_kv_g2, state_ssm, state_conv,
              norm_g, w_qkv, w_attn_o, w_ssm_in, lambda_re, lambda_im, log_dt, b_re, b_im, c_re, c_im,
              d_skip, w_glu, w_up, conv_w, conv_b, w_down):
    xp, xs = x_prompt, x_sample
    Bp = xp.shape[0]
    kv_p = [[] for _ in range(N_DIL)]
    kv_s = [[] for _ in range(N_DIL)]
    ssm_p, ssm_s, conv_p, conv_s = [], [], [], []
    for i in range(DEPTH):
        li = i // N_MIXERS
        hp = rms_norm(xp, norm_g[i, 0])
        hs = rms_norm(xs, norm_g[i, 0])
        if i % N_MIXERS == 0:
            mp, new_p = attn_prompt(hp, w_qkv[li], w_attn_o[li])
            ms, new_s = attn_sample(hs, (cache_kv_g0[li], cache_kv_g1[li], cache_kv_g2[li]), w_qkv[li], w_attn_o[li])
            for g in range(N_DIL):
                kv_p[g].append(new_p[g])
                kv_s[g].append(new_s[g])
        else:
            prm = (w_ssm_in[li], lambda_re[li], lambda_im[li], log_dt[li], b_re[li], b_im[li],
                   c_re[li], c_im[li], d_skip[li], w_glu[li])
            h0_p = jnp.zeros((Bp, SSM_GROUPS, SSM_STATE, 2), state_ssm.dtype)
            mp, sp = s5_mixer(hp, h0_p, *prm)
            ms, ss = s5_mixer(hs, state_ssm[li], *prm)
            ssm_p.append(sp)
            ssm_s.append(ss)
        xp = xp + rms_norm(mp, norm_g[i, 1])
        xs = xs + rms_norm(ms, norm_g[i, 1])
        hp = rms_norm(xp, norm_g[i, 2])
        hs = rms_norm(xs, norm_g[i, 2])
        c0_p = jnp.zeros((Bp, CONV_WIDTH - 1, 2 * D_FF), state_conv.dtype)
        fp, cp = conv_ffn(hp, c0_p, w_up[i], conv_w[i], conv_b[i], w_down[i])
        fs, cs = conv_ffn(hs, state_conv[i], w_up[i], conv_w[i], conv_b[i], w_down[i])
        conv_p.append(cp)
        conv_s.append(cs)
        xp = xp + rms_norm(fp, norm_g[i, 3])
        xs = xs + rms_norm(fs, norm_g[i, 3])
    y_prompt, y_sample = xp, xs
    return (y_prompt, y_sample,
            jnp.stack(kv_p[0]), jnp.stack(kv_s[0]),
            jnp.stack(kv_p[1]), jnp.stack(kv_s[1]),
            jnp.stack(kv_p[2]), jnp.stack(kv_s[2]),
            jnp.stack(ssm_p), jnp.stack(ssm_s),
            jnp.stack(conv_p), jnp.stack(conv_s))
```

```python
import functools
import math

import jax
import jax.numpy as jnp
from jax import lax
from jax.experimental import pallas as pl
from jax.experimental.pallas import tpu as pltpu

F32 = jnp.float32
BF16 = jnp.bfloat16

D_MODEL = 2048
BATCH = 4
SEQ = 2048
DEPTH = 4
DEC_BATCH = 8
DEC_SEQ = 1
PAST_LEN = 16384
N_MIXERS = 2
HEAD_DIM = 128
HEADS = D_MODEL // (2 * HEAD_DIM)
DIL_GROUPS = ((128, 1), (512, 4), (2048, 16))
N_DIL = len(DIL_GROUPS)
N_BACK = 128
ROT_DIM = HEAD_DIM // 4
ROPE_THETA = 500000.0
NEG_INF = -1e30
SSM_CH = 16
SSM_GROUPS = D_MODEL // SSM_CH
SSM_STATE = 64
D_FF = ((8 * D_MODEL // 3 + 127) // 128) * 128
CONV_WIDTH = 3
RMS_EPS = 1e-6

ATTN_W = HEADS * HEAD_DIM
QKV_COLS = 3 * N_DIL * ATTN_W

LANES = 128
SUBLANES = 8
VMEM_LIMIT = 52 * 1024 * 1024

FF_TILE = 512
D_FF_PAD = ((D_FF + FF_TILE - 1) // FF_TILE) * FF_TILE
FF_KT = D_FF_PAD // FF_TILE

SSM_TILE_G = 16
SSM_TILES = SSM_GROUPS // SSM_TILE_G
SSM_TILE_CH = SSM_TILE_G * SSM_CH
SSM_TILE_ST = SSM_TILE_G * SSM_STATE
SSM_SLABS = 2 * SSM_TILE_ST // LANES
SSM_CHUNK = 256
SSM_PITCH = SSM_CHUNK + SUBLANES


def _cparams(sem):
    return pltpu.CompilerParams(dimension_semantics=sem, vmem_limit_bytes=VMEM_LIMIT)


def _rms(x, g):
    ms = jnp.mean(x * x, axis=-1, keepdims=True)
    return x * lax.rsqrt(ms + RMS_EPS) * g


def _rope_table_kernel(inv_ref, cos_ref, sa_ref, sb_ref, *, pos_base):
    rows = cos_ref.shape[0]
    pos = (lax.broadcasted_iota(jnp.int32, (rows, LANES), 0) + pos_base).astype(F32)
    lane = lax.broadcasted_iota(jnp.int32, (rows, LANES), 1)
    ang = pos * inv_ref[...]
    c = jnp.cos(ang)
    s = jnp.sin(ang)
    half = ROT_DIM // 2
    cos_ref[...] = jnp.where(lane < ROT_DIM, c, 1.0)
    sa_ref[...] = jnp.where(lane < half, -s, 0.0)
    sb_ref[...] = jnp.where((lane >= half) & (lane < ROT_DIM), s, 0.0)


def rope_tables(rows, pos_base):
    half = ROT_DIM // 2
    inv = ROPE_THETA ** (-(jnp.arange(half, dtype=F32) * (2.0 / ROT_DIM)))
    inv_lane = jnp.tile(inv, LANES // half).reshape(1, LANES)
    shp = jax.ShapeDtypeStruct((rows, LANES), F32)
    return pl.pallas_call(
        functools.partial(_rope_table_kernel, pos_base=pos_base),
        out_shape=(shp, shp, shp),
        name="rope_tables",
    )(inv_lane)


def _norm_mm_kernel(x_ref, g_ref, w_ref, o_ref, h_ref):
    @pl.when(pl.program_id(1) == 0)
    def _():
        h_ref[...] = _rms(x_ref[...], g_ref[...]).astype(BF16)

    o_ref[...] = jnp.dot(h_ref[...], w_ref[...], preferred_element_type=F32).astype(o_ref.dtype)


def _norm_mm_rope_kernel(x_ref, g_ref, w_ref, cos_ref, sa_ref, sb_ref, o_ref, h_ref, *, rope_tiles):
    j = pl.program_id(1)

    @pl.when(j == 0)
    def _():
        h_ref[...] = _rms(x_ref[...], g_ref[...]).astype(BF16)

    y = jnp.dot(h_ref[...], w_ref[...], preferred_element_type=F32)

    @pl.when(j < rope_tiles)
    def _():
        c = cos_ref[...]
        sa = sa_ref[...]
        sb = sb_ref[...]
        half = ROT_DIM // 2
        for h in range(y.shape[1] // HEAD_DIM):
            xh = y[:, h * HEAD_DIM:(h + 1) * HEAD_DIM]
            up = pltpu.roll(xh, HEAD_DIM - half, axis=1)
            dn = pltpu.roll(xh, half, axis=1)
            o_ref[:, h * HEAD_DIM:(h + 1) * HEAD_DIM] = xh * c + up * sa + dn * sb

    @pl.when(j >= rope_tiles)
    def _():
        o_ref[...] = y


def norm_matmul(x, g, w, *, tm, tn, out_dtype=F32, rope=None, name="norm_matmul"):
    m, d = x.shape
    n = w.shape[1]
    assert m % tm == 0 and n % tn == 0
    grid = (m // tm, n // tn)
    in_specs = [
        pl.BlockSpec((tm, d), lambda i, j: (i, 0)),
        pl.BlockSpec((1, d), lambda i, j: (0, 0)),
        pl.BlockSpec((d, tn), lambda i, j: (0, j)),
    ]
    args = [x, g.reshape(1, d), w]
    if rope is None:
        body = _norm_mm_kernel
    else:
        cos, sa, sb, rope_cols = rope
        assert rope_cols % tn == 0 and tn % HEAD_DIM == 0
        tab_tiles = cos.shape[0] // tm
        assert cos.shape[0] % tm == 0
        tab_spec = pl.BlockSpec((tm, LANES), lambda i, j: (i % tab_tiles, 0))
        in_specs += [tab_spec, tab_spec, tab_spec]
        args += [cos, sa, sb]
        body = functools.partial(_norm_mm_rope_kernel, rope_tiles=rope_cols // tn)
    return pl.pallas_call(
        body,
        out_shape=jax.ShapeDtypeStruct((m, n), out_dtype),
        grid=grid,
        in_specs=in_specs,
        out_specs=pl.BlockSpec((tm, tn), lambda i, j: (i, j)),
        scratch_shapes=[pltpu.VMEM((tm, d), BF16)],
        compiler_params=_cparams(("parallel", "arbitrary")),
        name=name,
    )(*args)


ATTN_QB = 256


def _band_mask(nq, nk, off):
    r = lax.broadcasted_iota(jnp.int32, (nq, nk), 0)
    c = lax.broadcasted_iota(jnp.int32, (nq, nk), 1)
    rel = r - c + off
    return (rel >= 0) & (rel <= N_BACK)


def _attend_sequence(qf, kf, vf, masks, store):
    n = qf.shape[0]
    qb = (qf * (HEAD_DIM ** -0.5)).astype(BF16)
    kb = kf.astype(BF16)
    vb = vf.astype(BF16)
    qbs = min(ATTN_QB, n)
    for i in range(n // qbs):
        s0 = i * qbs
        ks = max(0, s0 - N_BACK)
        s = lax.dot_general(qb[s0:s0 + qbs], kb[ks:s0 + qbs], (((1,), (1,)), ((), ())),
                            preferred_element_type=F32)
        s = jnp.where(masks[(qbs, s0 + qbs - ks)], s, NEG_INF)
        m = jnp.max(s, axis=-1, keepdims=True)
        p = jnp.exp(s - m)
        l = jnp.sum(p, axis=-1, keepdims=True)
        o = jnp.dot(p.astype(BF16), vb[ks:s0 + qbs], preferred_element_type=F32)
        store(s0, qbs, o, l, m + jnp.log(l))


def _attn_prompt_kernel(q0, q1, q2, k0, k1, k2, v0, v1, v2, o_ref, o_scr, l_scr, d_scr):
    qs, ks, vs = (q0, q1, q2), (k0, k1, k2), (v0, v1, v2)
    t = q0.shape[0]
    masks = {}
    for _, dil in DIL_GROUPS:
        n = t // dil
        qbs = min(ATTN_QB, n)
        masks[(qbs, qbs)] = None
        if n > qbs:
            masks[(qbs, qbs + N_BACK)] = None
    for (nq, nk) in masks:
        masks[(nq, nk)] = _band_mask(nq, nk, nk - nq)

    for g, (_, dil) in enumerate(DIL_GROUPS):
        n = t // dil

        def run(r, g=g, dil=dil, n=n):
            if dil == 1:
                rows = lambda s0, cnt: pl.ds(s0, cnt)
                qf, kf, vf = qs[g][...], ks[g][...], vs[g][...]
            else:
                rows = lambda s0, cnt: pl.ds(s0 * dil + r, cnt, stride=dil)
                qf = qs[g][pl.ds(r, n, stride=dil), :]
                kf = ks[g][pl.ds(r, n, stride=dil), :]
                vf = vs[g][pl.ds(r, n, stride=dil), :]

            def store(s0, cnt, o, l, lse):
                @pl.when(pl.program_id(1) == 0)
                def _():
                    d_scr[g, rows(s0, cnt), :] = jnp.broadcast_to(l, (cnt, LANES))

                o_scr[g, rows(s0, cnt), :] = o / d_scr[g, rows(s0, cnt), :]
                l_scr[g, rows(s0, cnt), :] = jnp.broadcast_to(lse, (cnt, LANES))

            _attend_sequence(qf, kf, vf, masks, store)

        if dil <= 4:
            for r in range(dil):
                run(r)
        else:
            def body(r, carry):
                run(r)
                return carry
            lax.fori_loop(0, dil, body, 0)

    l0, l1, l2 = l_scr[0], l_scr[1], l_scr[2]
    m = jnp.maximum(jnp.maximum(l0, l1), l2)
    e0, e1, e2 = jnp.exp(l0 - m), jnp.exp(l1 - m), jnp.exp(l2 - m)
    num = e0 * o_scr[0] + e1 * o_scr[1] + e2 * o_scr[2]
    o_ref[...] = (num / (e0 + e1 + e2)).astype(o_ref.dtype)


def attn_prompt(qkv):
    b, t, _ = qkv.shape

    def spec(part, g):
        return pl.BlockSpec((None, t, HEAD_DIM), lambda bi, h: (bi, 0, (part * N_DIL + g) * HEADS + h))

    in_specs = [spec(part, g) for part in range(3) for g in range(N_DIL)]
    return pl.pallas_call(
        _attn_prompt_kernel,
        out_shape=jax.ShapeDtypeStruct((b, t, ATTN_W), BF16),
        grid=(b, HEADS),
        in_specs=in_specs,
        out_specs=pl.BlockSpec((None, t, HEAD_DIM), lambda bi, h: (bi, 0, h)),
        scratch_shapes=[pltpu.VMEM((N_DIL, t, LANES), F32)] * 3,
        compiler_params=_cparams(("parallel", "arbitrary")),
        name="attn_prompt",
    )(*([qkv] * 9))


def _attn_sample_kernel(qkv_ref, c0, c1, c2, o_ref):
    caches = (c0, c1, c2)
    row = lax.broadcasted_iota(jnp.int32, (HEADS, ATTN_W), 0)
    lane = lax.broadcasted_iota(jnp.int32, (HEADS, ATTN_W), 1)
    own = (lane // HEAD_DIM) == row
    outs, lses = [], []
    for g in range(N_DIL):
        q = qkv_ref[:, g * ATTN_W:(g + 1) * ATTN_W] * (HEAD_DIM ** -0.5)
        k_new = qkv_ref[:, (N_DIL + g) * ATTN_W:(N_DIL + g + 1) * ATTN_W]
        v_new = qkv_ref[:, (2 * N_DIL + g) * ATTN_W:(2 * N_DIL + g + 1) * ATTN_W]
        qbd = jnp.where(own, jnp.broadcast_to(q, (HEADS, ATTN_W)), 0.0)
        kc = caches[g][:, :ATTN_W].astype(BF16)
        vc = caches[g][:, ATTN_W:].astype(BF16)
        s = lax.dot_general(qbd.astype(BF16), kc, (((1,), (1,)), ((), ())),
                            preferred_element_type=F32)
        s_new = jnp.sum(qbd * k_new, axis=-1, keepdims=True)
        m = jnp.maximum(jnp.max(s, axis=-1, keepdims=True), s_new)
        p = jnp.exp(s - m)
        p_new = jnp.exp(s_new - m)
        l = jnp.sum(p, axis=-1, keepdims=True) + p_new
        o = jnp.dot(p.astype(BF16), vc, preferred_element_type=F32) + p_new * v_new
        outs.append(o / l[0:1])
        lses.append(m + jnp.log(l))
    m = jnp.maximum(jnp.maximum(lses[0], lses[1]), lses[2])
    es = [jnp.exp(x - m) for x in lses]
    num = es[0] * outs[0] + es[1] * outs[1] + es[2] * outs[2]
    merged = jnp.where(own, num / (es[0] + es[1] + es[2]), 0.0)
    o_ref[...] = jnp.sum(merged, axis=0, keepdims=True).astype(o_ref.dtype)


def attn_sample(qkv, caches):
    b = qkv.shape[0]
    in_specs = [pl.BlockSpec((None, 1, QKV_COLS), lambda bi: (bi, 0, 0))]
    args = [qkv.reshape(b, 1, QKV_COLS)]
    for (win, dil), cache in zip(DIL_GROUPS, caches):
        length = cache.shape[1]
        assert length == win and length // dil == N_BACK
        args.append(cache.reshape(b, length // dil, dil * 2 * ATTN_W))
        in_specs.append(pl.BlockSpec((None, N_BACK, 2 * ATTN_W), lambda bi: (bi, 0, 0)))
    out = pl.pallas_call(
        _attn_sample_kernel,
        out_shape=jax.ShapeDtypeStruct((b, 1, ATTN_W), BF16),
        grid=(b,),
        in_specs=in_specs,
        out_specs=pl.BlockSpec((None, 1, ATTN_W), lambda bi: (bi, 0, 0)),
        compiler_params=_cparams(("parallel",)),
        name="attn_sample",
    )(*args)
    return out.reshape(b, ATTN_W)


def _mm_res_kernel(a_ref, w_ref, r_ref, g_ref, o_ref):
    y = jnp.dot(a_ref[...], w_ref[...], preferred_element_type=F32)
    o_ref[...] = r_ref[...] + _rms(y, g_ref[...])


def matmul_residual(a, w, resid, g, *, tm):
    m, k = a.shape
    d = w.shape[1]
    return pl.pallas_call(
        _mm_res_kernel,
        out_shape=jax.ShapeDtypeStruct((m, d), F32),
        grid=(m // tm,),
        in_specs=[
            pl.BlockSpec((tm, k), lambda i: (i, 0)),
            pl.BlockSpec((k, d), lambda i: (0, 0)),
            pl.BlockSpec((tm, d), lambda i: (i, 0)),
            pl.BlockSpec((1, d), lambda i: (0, 0)),
        ],
        out_specs=pl.BlockSpec((tm, d), lambda i: (i, 0)),
        compiler_params=_cparams(("parallel",)),
        name="matmul_residual",
    )(a, w, resid, g.reshape(1, d))


def _glu_res_kernel(a_ref, wv_ref, wg_ref, r_ref, g_ref, o_ref, y_scr, *, nj):
    j = pl.program_id(1)
    a = a_ref[...]
    val = jnp.dot(a, wv_ref[...], preferred_element_type=F32)
    gate = jnp.dot(a, wg_ref[...], preferred_element_type=F32)
    y_scr[j] = val * jax.nn.sigmoid(gate)

    @pl.when(j == nj - 1)
    def _():
        y = jnp.concatenate([y_scr[t] for t in range(nj)], axis=-1)
        o_ref[...] = r_ref[...] + _rms(y, g_ref[...])


def glu_residual(a, w, resid, g, *, tm, tn):
    m, k = a.shape
    d = w.shape[1] // 2
    nj = d // tn
    return pl.pallas_call(
        functools.partial(_glu_res_kernel, nj=nj),
        out_shape=jax.ShapeDtypeStruct((m, d), F32),
        grid=(m // tm, nj),
        in_specs=[
            pl.BlockSpec((tm, k), lambda i, j: (i, 0)),
            pl.BlockSpec((k, tn), lambda i, j: (0, j)),
            pl.BlockSpec((k, tn), lambda i, j: (0, nj + j)),
            pl.BlockSpec((tm, d), lambda i, j: (i, 0)),
            pl.BlockSpec((1, d), lambda i, j: (0, 0)),
        ],
        out_specs=pl.BlockSpec((tm, d), lambda i, j: (i, 0)),
        scratch_shapes=[pltpu.VMEM((nj, tm, tn), F32)],
        compiler_params=_cparams(("parallel", "arbitrary")),
        name="glu_residual",
    )(a, w, w, resid, g.reshape(1, d))


def _conv_act(u, um1, um2, cw, cb):
    return cb + cw[0:1] * um2 + cw[1:2] * um1 + cw[2:3] * u


def _gated(ug, ug1, ug2, uv, uv1, uv2, cwg, cwv, cbg, cbv):
    gate = _conv_act(ug, ug1, ug2, cwg, cbg)
    val = _conv_act(uv, uv1, uv2, cwv, cbv)
    return (gate * jax.nn.sigmoid(gate) * val).astype(BF16)


def _down_finish(act, w_ref, r_ref, g_ref, o_ref, acc, nk):
    k = pl.program_id(1)
    part = jnp.dot(act, w_ref[...], preferred_element_type=F32)

    @pl.when(k == 0)
    def _():
        acc[...] = part

    @pl.when(k > 0)
    def _():
        acc[...] += part

    @pl.when(k == nk - 1)
    def _():
        o_ref[...] = r_ref[...] + _rms(acc[...], g_ref[...])


def _shift_rows(u, halo, first):
    n = u.shape[0]
    row = lax.broadcasted_iota(jnp.int32, u.shape, 0)
    keep = jnp.where(first, 0.0, 1.0)
    h1 = halo[SUBLANES - 1:SUBLANES] * keep
    h2 = halo[SUBLANES - 2:SUBLANES - 1] * keep
    um1 = jnp.where(row == 0, h1, pltpu.roll(u, 1, axis=0))
    um2 = jnp.where(row == 0, h2, jnp.where(row == 1, h1, pltpu.roll(u, 2, axis=0)))
    return um1, um2


def _down_seq_kernel(ug_ref, uv_ref, hg_ref, hv_ref, cwg_ref, cwv_ref, cbg_ref, cbv_ref,
                     w_ref, r_ref, g_ref, o_ref, acc, *, nk, tiles_per_seq):
    first = (pl.program_id(0) % tiles_per_seq) == 0
    ug, uv = ug_ref[...], uv_ref[...]
    ug1, ug2 = _shift_rows(ug, hg_ref[...], first)
    uv1, uv2 = _shift_rows(uv, hv_ref[...], first)
    act = _gated(ug, ug1, ug2, uv, uv1, uv2, cwg_ref[...], cwv_ref[...], cbg_ref[...], cbv_ref[...])
    _down_finish(act, w_ref, r_ref, g_ref, o_ref, acc, nk)


def _down_step_kernel(ug_ref, uv_ref, g1_ref, v1_ref, g2_ref, v2_ref, cwg_ref, cwv_ref, cbg_ref, cbv_ref,
                      w_ref, r_ref, g_ref, o_ref, acc, *, nk):
    act = _gated(ug_ref[...], g1_ref[...], g2_ref[...], uv_ref[...], v1_ref[...], v2_ref[...],
                 cwg_ref[...], cwv_ref[...], cbg_ref[...], cbv_ref[...])
    _down_finish(act, w_ref, r_ref, g_ref, o_ref, acc, nk)


def down_residual(u, conv_w, conv_b, w_down, resid, g, *, tm, seq_len=None, prev=None):
    m = u.shape[0]
    d = w_down.shape[1]
    tk, nk = FF_TILE, FF_KT
    gate_spec = pl.BlockSpec((tm, tk), lambda i, k: (i, k))
    val_spec = pl.BlockSpec((tm, tk), lambda i, k: (i, nk + k))
    tail_specs = [
        pl.BlockSpec((CONV_WIDTH, tk), lambda i, k: (0, k)),
        pl.BlockSpec((CONV_WIDTH, tk), lambda i, k: (0, nk + k)),
        pl.BlockSpec((1, tk), lambda i, k: (0, k)),
        pl.BlockSpec((1, tk), lambda i, k: (0, nk + k)),
        pl.BlockSpec((tk, d), lambda i, k: (k, 0)),
        pl.BlockSpec((tm, d), lambda i, k: (i, 0)),
        pl.BlockSpec((1, d), lambda i, k: (0, 0)),
    ]
    tail_args = [conv_w, conv_w, conv_b, conv_b, w_down, resid, g.reshape(1, d)]
    if prev is None:
        assert seq_len % tm == 0 and tm % SUBLANES == 0
        hb = tm // SUBLANES
        halo_g = pl.BlockSpec((SUBLANES, tk), lambda i, k: (jnp.maximum(i * hb - 1, 0), k))
        halo_v = pl.BlockSpec((SUBLANES, tk), lambda i, k: (jnp.maximum(i * hb - 1, 0), nk + k))
        body = functools.partial(_down_seq_kernel, nk=nk, tiles_per_seq=seq_len // tm)
        in_specs = [gate_spec, val_spec, halo_g, halo_v] + tail_specs
        args = [u, u, u, u] + tail_args
    else:
        p1, p2 = prev
        body = functools.partial(_down_step_kernel, nk=nk)
        in_specs = [gate_spec, val_spec, gate_spec, val_spec, gate_spec, val_spec] + tail_specs
        args = [u, u, p1, p1, p2, p2] + tail_args
    return pl.pallas_call(
        body,
        out_shape=jax.ShapeDtypeStruct((m, d), F32),
        grid=(m // tm, nk),
        in_specs=in_specs,
        out_specs=pl.BlockSpec((tm, d), lambda i, k: (i, 0)),
        scratch_shapes=[pltpu.VMEM((tm, d), F32)],
        compiler_params=_cparams(("parallel", "arbitrary")),
        name="down_residual",
    )(*args)


def _ssm_params_kernel(lre_ref, lim_ref, ldt_ref, bre_ref, bim_ref, are_ref, aim_ref, bbre_ref, bbim_ref):
    lre = jnp.minimum(lre_ref[...], -1e-4)
    lim = lim_ref[...]
    dt = jnp.exp(ldt_ref[...])
    mag = jnp.exp(lre * dt)
    a_re = mag * jnp.cos(lim * dt)
    a_im = mag * jnp.sin(lim * dt)
    are_ref[...] = a_re
    aim_ref[...] = a_im
    x, y = a_re - 1.0, a_im
    den = lre * lre + lim * lim
    c_re = (x * lre + y * lim) / den
    c_im = (y * lre - x * lim) / den
    bre, bim = bre_ref[...], bim_ref[...]
    bbre_ref[...] = c_re * bre - c_im * bim
    bbim_ref[...] = c_re * bim + c_im * bre


def ssm_params(lam_re, lam_im, log_dt, b_re, b_im):
    g, p = lam_re.shape
    shp_a = jax.ShapeDtypeStruct((g, 1, p), F32)
    shp_b = jax.ShapeDtypeStruct((g, SSM_CH, p), F32)
    a_re, a_im, bb_re, bb_im = pl.pallas_call(
        _ssm_params_kernel,
        out_shape=(shp_a, shp_a, shp_b, shp_b),
        compiler_params=pltpu.CompilerParams(vmem_limit_bytes=VMEM_LIMIT),
        name="ssm_params",
    )(lam_re.reshape(g, 1, p), lam_im.reshape(g, 1, p), log_dt.reshape(g, 1, 1),
      jnp.swapaxes(b_re, 1, 2), jnp.swapaxes(b_im, 1, 2))
    return a_re.reshape(g, p), a_im.reshape(g, p), bb_re, bb_im


def _gelu_tanh(y):
    return 0.5 * y * (1.0 + jnp.tanh(math.sqrt(2.0 / math.pi) * (y + 0.044715 * (y * y * y))))


def _ssm_scan_kernel(u_ref, bm_ref, cm_ref, are_ref, aim_ref, d_ref, h0r_ref, h0i_ref,
                     z_ref, sr_ref, si_ref, bu_scr, xr_scr, xi_scr, *, n_chunks):
    c = pl.program_id(1)
    nb, tc = u_ref.shape[0], u_ref.shape[1]
    pitch = SSM_PITCH
    half = SSM_SLABS // 2

    @pl.when(c == 0)
    def _():
        xr_scr[...] = h0r_ref[...]
        xi_scr[...] = h0i_ref[...]

    bm = bm_ref[...]
    for b in range(nb):
        r = jnp.dot(u_ref[b].astype(BF16), bm, preferred_element_type=F32)
        for k in range(SSM_SLABS):
            bu_scr[b, k * pitch:k * pitch + tc, :] = r[:, k * LANES:(k + 1) * LANES]

    ar = are_ref[...]
    ai = aim_ref[...]

    def step(t, carry):
        new = []
        for b in range(nb):
            xr, xi = carry[b]
            re_rows = pl.ds(t, half, stride=pitch)
            im_rows = pl.ds(half * pitch + t, half, stride=pitch)
            nr = ar * xr - ai * xi + bu_scr[b, re_rows, :]
            ni = ar * xi + ai * xr + bu_scr[b, im_rows, :]
            bu_scr[b, re_rows, :] = nr
            bu_scr[b, im_rows, :] = ni
            new.append((nr, ni))
        return tuple(new)

    init = tuple((xr_scr[b], xi_scr[b]) for b in range(nb))
    fin = lax.fori_loop(0, tc, step, init, unroll=8)
    for b in range(nb):
        xr_scr[b] = fin[b][0]
        xi_scr[b] = fin[b][1]

    cm = cm_ref[...]
    for b in range(nb):
        xs = jnp.concatenate([bu_scr[b, k * pitch:k * pitch + tc, :] for k in range(SSM_SLABS)], axis=-1)
        y = jnp.dot(xs.astype(BF16), cm, preferred_element_type=F32) + d_ref[...] * u_ref[b]
        z_ref[b] = _gelu_tanh(y).astype(z_ref.dtype)

    @pl.when(c == n_chunks - 1)
    def _():
        sr_ref[...] = xr_scr[...]
        si_ref[...] = xi_scr[...]


def ssm_scan(u, bmat, cmat, a_re, a_im, d_skip, h0_re, h0_im):
    b, t, d = u.shape
    tc = SSM_CHUNK
    n_chunks = t // tc
    half = SSM_SLABS // 2
    st_spec = pl.BlockSpec((b, None, half, LANES), lambda j, c: (0, j, 0, 0))
    st_shape = jax.ShapeDtypeStruct((b, SSM_TILES, half, LANES), F32)
    return pl.pallas_call(
        functools.partial(_ssm_scan_kernel, n_chunks=n_chunks),
        out_shape=(jax.ShapeDtypeStruct((b, t, d), BF16), st_shape, st_shape),
        grid=(SSM_TILES, n_chunks),
        in_specs=[
            pl.BlockSpec((b, tc, SSM_TILE_CH), lambda j, c: (0, c, j)),
            pl.BlockSpec((None, SSM_TILE_CH, 2 * SSM_TILE_ST), lambda j, c: (j, 0, 0)),
            pl.BlockSpec((None, 2 * SSM_TILE_ST, SSM_TILE_CH), lambda j, c: (j, 0, 0)),
            pl.BlockSpec((None, half, LANES), lambda j, c: (j, 0, 0)),
            pl.BlockSpec((None, half, LANES), lambda j, c: (j, 0, 0)),
            pl.BlockSpec((1, SSM_TILE_CH), lambda j, c: (0, j)),
            st_spec, st_spec,
        ],
        out_specs=(pl.BlockSpec((b, tc, SSM_TILE_CH), lambda j, c: (0, c, j)), st_spec, st_spec),
        scratch_shapes=[
            pltpu.VMEM((b, SSM_SLABS * SSM_PITCH, LANES), F32),
            pltpu.VMEM((b, half, LANES), F32),
            pltpu.VMEM((b, half, LANES), F32),
        ],
        compiler_params=_cparams(("parallel", "arbitrary")),
        name="ssm_scan",
    )(u, bmat, cmat, a_re.reshape(SSM_TILES, half, LANES), a_im.reshape(SSM_TILES, half, LANES),
      d_skip.reshape(1, d), h0_re, h0_im)


def _ssm_step_kernel(u_ref, bm_ref, cm_ref, are_ref, aim_ref, d_ref, h0r_ref, h0i_ref, z_ref, sr_ref, si_ref):
    u = u_ref[...]
    bu = jnp.dot(u.astype(BF16), bm_ref[...], preferred_element_type=F32)
    ar, ai = are_ref[...], aim_ref[...]
    hr, hi = h0r_ref[...], h0i_ref[...]
    xr = ar * hr - ai * hi + bu[:, :SSM_TILE_ST]
    xi = ar * hi + ai * hr + bu[:, SSM_TILE_ST:]
    sr_ref[...] = xr
    si_ref[...] = xi
    xs = jnp.concatenate([xr, xi], axis=-1).astype(BF16)
    y = jnp.dot(xs, cm_ref[...], preferred_element_type=F32) + d_ref[...] * u
    z_ref[...] = _gelu_tanh(y).astype(z_ref.dtype)


def ssm_step(u, bmat, cmat, a_re, a_im, d_skip, h0_re, h0_im):
    b, d = u.shape
    st_spec = pl.BlockSpec((None, b, SSM_TILE_ST), lambda j: (j, 0, 0))
    a_spec = pl.BlockSpec((None, 1, SSM_TILE_ST), lambda j: (j, 0, 0))
    st_shape = jax.ShapeDtypeStruct((SSM_TILES, b, SSM_TILE_ST), F32)
    return pl.pallas_call(
        _ssm_step_kernel,
        out_shape=(jax.ShapeDtypeStruct((b, d), BF16), st_shape, st_shape),
        grid=(SSM_TILES,),
        in_specs=[
            pl.BlockSpec((b, SSM_TILE_CH), lambda j: (0, j)),
            pl.BlockSpec((None, SSM_TILE_CH, 2 * SSM_TILE_ST), lambda j: (j, 0, 0)),
            pl.BlockSpec((None, 2 * SSM_TILE_ST, SSM_TILE_CH), lambda j: (j, 0, 0)),
            a_spec, a_spec,
            pl.BlockSpec((1, SSM_TILE_CH), lambda j: (0, j)),
            st_spec, st_spec,
        ],
        out_specs=(pl.BlockSpec((b, SSM_TILE_CH), lambda j: (0, j)), st_spec, st_spec),
        compiler_params=_cparams(("parallel",)),
        name="ssm_step",
    )(u, bmat, cmat, a_re.reshape(SSM_TILES, 1, SSM_TILE_ST), a_im.reshape(SSM_TILES, 1, SSM_TILE_ST),
      d_skip.reshape(1, d), h0_re, h0_im)


def _block_diag_mats(bb_re, bb_im, c_re, c_im):
    eye = jnp.eye(SSM_TILE_G, dtype=F32)

    def bmat(bb):
        t = bb.reshape(SSM_TILES, SSM_TILE_G, SSM_CH, SSM_STATE)
        return jnp.einsum('jgcp,gh->jgchp', t, eye).reshape(SSM_TILES, SSM_TILE_CH, SSM_TILE_ST)

    def cmat(cc):
        t = cc.reshape(SSM_TILES, SSM_TILE_G, SSM_CH, SSM_STATE)
        return jnp.einsum('jgcp,gh->jhpgc', t, eye).reshape(SSM_TILES, SSM_TILE_ST, SSM_TILE_CH)

    b_all = jnp.concatenate([bmat(bb_re), bmat(bb_im)], axis=2).astype(BF16)
    c_all = jnp.concatenate([cmat(c_re), cmat(-c_im)], axis=1).astype(BF16)
    return b_all, c_all


def _pad_ff_cols(a):
    pad = [(0, 0)] * (a.ndim - 1) + [(0, D_FF_PAD - D_FF)]
    return jnp.concatenate([jnp.pad(a[..., :D_FF], pad), jnp.pad(a[..., D_FF:], pad)], axis=-1)


def _unpad_ff_cols(a):
    return jnp.concatenate([a[..., :D_FF], a[..., D_FF_PAD:D_FF_PAD + D_FF]], axis=-1)


def kernel(x_prompt, x_sample, cache_kv_g0, cache_kv_g1, cache_kv_g2, state_ssm, state_conv, norm_g, w_qkv,
           w_attn_o, w_ssm_in, lambda_re, lambda_im, log_dt, b_re, b_im, c_re, c_im, d_skip, w_glu, w_up,
           conv_w, conv_b, w_down):
    bp, t, d = x_prompt.shape
    bs = x_sample.shape[0]
    mp = bp * t
    xp = x_prompt.reshape(mp, d)
    xs = x_sample.reshape(bs, d)
    caches = (cache_kv_g0, cache_kv_g1, cache_kv_g2)
    tm = 512

    rope_p = rope_tables(t, 0)
    rope_s = rope_tables(SUBLANES, PAST_LEN)
    rope_s = tuple(jnp.broadcast_to(r[0:1], (bs, LANES)) for r in rope_s)

    kv_p = [[] for _ in range(N_DIL)]
    kv_s = [[] for _ in range(N_DIL)]
    ssm_p, ssm_s, conv_p, conv_s = [], [], [], []

    for i in range(DEPTH):
        li = i // N_MIXERS
        if i % N_MIXERS == 0:
            wq = w_qkv[li].astype(BF16)
            wo = w_attn_o[li].astype(BF16)
            rope_cols = 2 * N_DIL * ATTN_W
            qkv_p = norm_matmul(xp, norm_g[i, 0], wq, tm=tm, tn=512, rope=(*rope_p, rope_cols), name="qkv_prompt")
            qkv_s = norm_matmul(xs, norm_g[i, 0], wq, tm=bs, tn=512, rope=(*rope_s, rope_cols), name="qkv_sample")
            qkv_p3 = qkv_p.reshape(bp, t, QKV_COLS)
            op = attn_prompt(qkv_p3).reshape(mp, ATTN_W)
            os_ = attn_sample(qkv_s, [c[li] for c in caches])
            xp = matmul_residual(op, wo, xp, norm_g[i, 1], tm=tm)
            xs = matmul_residual(os_, wo, xs, norm_g[i, 1], tm=bs)
            for g, (win, _) in enumerate(DIL_GROUPS):
                keep = min(win, t)
                kcol = (N_DIL + g) * ATTN_W
                vcol = (2 * N_DIL + g) * ATTN_W
                kp = qkv_p3[:, t - keep:, kcol:kcol + ATTN_W].reshape(bp, keep, HEADS, HEAD_DIM)
                vp = qkv_p3[:, t - keep:, vcol:vcol + ATTN_W].reshape(bp, keep, HEADS, HEAD_DIM)
                kv_p[g].append(jnp.stack([kp, vp], axis=2))
                ksn = qkv_s[:, kcol:kcol + ATTN_W].reshape(bs, 1, HEADS, HEAD_DIM)
                vsn = qkv_s[:, vcol:vcol + ATTN_W].reshape(bs, 1, HEADS, HEAD_DIM)
                kv_s[g].append(jnp.stack([ksn, vsn], axis=2))
        else:
            w_in = w_ssm_in[li].astype(BF16)
            wg = w_glu[li].astype(BF16)
            a_re, a_im, bb_re, bb_im = ssm_params(lambda_re[li], lambda_im[li], log_dt[li], b_re[li], b_im[li])
            bmat, cmat = _block_diag_mats(bb_re, bb_im, c_re[li], c_im[li])
            half = SSM_SLABS // 2
            up_ = norm_matmul(xp, norm_g[i, 0], w_in, tm=tm, tn=512, name="ssm_in_prompt")
            us_ = norm_matmul(xs, norm_g[i, 0], w_in, tm=bs, tn=512, name="ssm_in_sample")
            zeros = jnp.zeros((bp, SSM_TILES, half, LANES), F32)
            zp, sp_re, sp_im = ssm_scan(up_.reshape(bp, t, d), bmat, cmat, a_re, a_im, d_skip[li], zeros, zeros)
            h0 = state_ssm[li].reshape(bs, SSM_TILES, SSM_TILE_ST, 2)
            h0_re = jnp.swapaxes(h0[..., 0], 0, 1)
            h0_im = jnp.swapaxes(h0[..., 1], 0, 1)
            zs, ss_re, ss_im = ssm_step(us_, bmat, cmat, a_re, a_im, d_skip[li], h0_re, h0_im)
            xp = glu_residual(zp.reshape(mp, d), wg, xp, norm_g[i, 1], tm=tm, tn=512)
            xs = glu_residual(zs, wg, xs, norm_g[i, 1], tm=bs, tn=512)
            ssm_p.append(jnp.stack([sp_re.reshape(bp, SSM_GROUPS, SSM_STATE),
                                    sp_im.reshape(bp, SSM_GROUPS, SSM_STATE)], axis=-1))
            ssm_s.append(jnp.stack([jnp.swapaxes(ss_re, 0, 1).reshape(bs, SSM_GROUPS, SSM_STATE),
                                    jnp.swapaxes(ss_im, 0, 1).reshape(bs, SSM_GROUPS, SSM_STATE)], axis=-1))

        wu = _pad_ff_cols(w_up[i]).astype(BF16)
        cw = _pad_ff_cols(conv_w[i])
        cb = _pad_ff_cols(conv_b[i].reshape(1, 2 * D_FF))
        wd = jnp.pad(w_down[i], ((0, D_FF_PAD - D_FF), (0, 0))).astype(BF16)
        uu_p = norm_matmul(xp, norm_g[i, 2], wu, tm=tm, tn=512, name="ffn_up_prompt")
        uu_s = norm_matmul(xs, norm_g[i, 2], wu, tm=bs, tn=512, name="ffn_up_sample")
        st = _pad_ff_cols(state_conv[i])
        xp = down_residual(uu_p, cw, cb, wd, xp, norm_g[i, 3], tm=tm, seq_len=t)
        xs = down_residual(uu_s, cw, cb, wd, xs, norm_g[i, 3], tm=bs, prev=(st[:, 1], st[:, 0]))
        conv_p.append(_unpad_ff_cols(uu_p.reshape(bp, t, 2 * D_FF_PAD)[:, t - (CONV_WIDTH - 1):]))
        conv_s.append(jnp.concatenate([state_conv[i][:, 1:], _unpad_ff_cols(uu_s)[:, None]], axis=1))

    return (xp.reshape(bp, t, d), xs.reshape(bs, DEC_SEQ, d),
            jnp.stack(kv_p[0]), jnp.stack(kv_s[0]),
            jnp.stack(kv_p[1]), jnp.stack(kv_s[1]),
            jnp.stack(kv_p[2]), jnp.stack(kv_s[2]),
            jnp.stack(ssm_p), jnp.stack(ssm_s),
            jnp.stack(conv_p), jnp.stack(conv_s))
```

```python
import functools
import math

import jax
import jax.numpy as jnp
from jax import lax
from jax.experimental import pallas as pl
from jax.experimental.pallas import tpu as pltpu

F32 = jnp.float32
BF16 = jnp.bfloat16

D_MODEL = 2048
DEPTH = 4
PAST_LEN = 16384
N_MIXERS = 2
HEAD_DIM = 128
HEADS = D_MODEL // (2 * HEAD_DIM)
DIL_GROUPS = ((128, 1), (512, 4), (2048, 16))
N_DIL = len(DIL_GROUPS)
N_BACK = 128
ROT_DIM = HEAD_DIM // 4
ROPE_THETA = 500000.0
NEG_INF = -1e30
SSM_CH = 16
SSM_GROUPS = D_MODEL // SSM_CH
SSM_STATE = 64
D_FF = ((8 * D_MODEL // 3 + 127) // 128) * 128
CONV_WIDTH = 3
RMS_EPS = 1e-6

ATTN_W = HEADS * HEAD_DIM
Q_COLS = N_DIL * ATTN_W
KV_COLS = 2 * ATTN_W
QKV_COLS = 3 * N_DIL * ATTN_W

LANES = 128
SUBLANES = 8
BF16_ROWS = 16
VMEM_LIMIT = 52 * 1024 * 1024

FF_ALIGN = 512
D_FF_PAD = ((D_FF + FF_ALIGN - 1) // FF_ALIGN) * FF_ALIGN
FF_DOWN_TK = D_FF_PAD // 4

SSM_TILE_G = 16
SSM_TILES = SSM_GROUPS // SSM_TILE_G
SSM_TILE_CH = SSM_TILE_G * SSM_CH
SSM_TILE_ST = SSM_TILE_G * SSM_STATE
SSM_SLABS = 2 * SSM_TILE_ST // LANES
SSM_CHUNK = 256
SSM_PITCH = SSM_CHUNK + SUBLANES


def _cparams(sem):
    return pltpu.CompilerParams(dimension_semantics=sem, vmem_limit_bytes=VMEM_LIMIT)


def _rms(x, g):
    ms = jnp.mean(x * x, axis=-1, keepdims=True)
    return x * lax.rsqrt(ms + RMS_EPS) * g


def _rope_table_kernel(inv_ref, cos_ref, sa_ref, sb_ref, *, pos_base):
    rows = cos_ref.shape[0]
    pos = (lax.broadcasted_iota(jnp.int32, (rows, LANES), 0) + pos_base).astype(F32)
    lane = lax.broadcasted_iota(jnp.int32, (rows, LANES), 1)
    ang = pos * inv_ref[...]
    c = jnp.cos(ang)
    s = jnp.sin(ang)
    half = ROT_DIM // 2
    cos_ref[...] = jnp.where(lane < ROT_DIM, c, 1.0)
    sa_ref[...] = jnp.where(lane < half, -s, 0.0)
    sb_ref[...] = jnp.where((lane >= half) & (lane < ROT_DIM), s, 0.0)


def rope_tables(rows, pos_base):
    half = ROT_DIM // 2
    inv = ROPE_THETA ** (-(jnp.arange(half, dtype=F32) * (2.0 / ROT_DIM)))
    inv_lane = jnp.tile(inv, LANES // half).reshape(1, LANES)
    shp = jax.ShapeDtypeStruct((rows, LANES), F32)
    return pl.pallas_call(
        functools.partial(_rope_table_kernel, pos_base=pos_base),
        out_shape=(shp, shp, shp),
        name="rope_tables",
    )(inv_lane)


QKV_TN = 512
Q_TILES = Q_COLS // QKV_TN
KV_TILES = KV_COLS // QKV_TN


def _qkv_kernel(x_ref, g_ref, w_ref, cos_ref, sa_ref, sb_ref, q_ref, kv0_ref, kv1_ref, kv2_ref, h_ref):
    j = pl.program_id(1)

    @pl.when(j == 0)
    def _():
        h_ref[...] = _rms(x_ref[...], g_ref[...]).astype(BF16)

    y = jnp.dot(h_ref[...], w_ref[...], preferred_element_type=F32)
    c, sa, sb = cos_ref[...], sa_ref[...], sb_ref[...]
    half = ROT_DIM // 2
    heads = []
    for h in range(QKV_TN // HEAD_DIM):
        xh = y[:, h * HEAD_DIM:(h + 1) * HEAD_DIM]
        up = pltpu.roll(xh, HEAD_DIM - half, axis=1)
        dn = pltpu.roll(xh, half, axis=1)
        heads.append(xh * c + up * sa + dn * sb)
    yr = jnp.concatenate(heads, axis=-1)

    @pl.when(j < Q_TILES)
    def _():
        q_ref[...] = yr

    for g, ref in enumerate((kv0_ref, kv1_ref, kv2_ref)):
        base = Q_TILES + g * KV_TILES

        @pl.when((j >= base) & (j < base + KV_TILES // 2))
        def _(ref=ref):
            ref[...] = yr

        @pl.when((j >= base + KV_TILES // 2) & (j < base + KV_TILES))
        def _(ref=ref):
            ref[...] = y


def qkv_project(x, g, w, rope, *, tm):
    m, d = x.shape
    cos, sa, sb = rope
    tab_tiles = cos.shape[0] // tm
    assert m % tm == 0 and cos.shape[0] % tm == 0 and w.shape[1] == QKV_COLS
    tab_spec = pl.BlockSpec((tm, LANES), lambda i, j: (i % tab_tiles, 0))

    def kv_spec(g):
        base = Q_TILES + g * KV_TILES
        return pl.BlockSpec((tm, QKV_TN), lambda i, j: (i, jnp.clip(j - base, 0, KV_TILES - 1)))

    return pl.pallas_call(
        _qkv_kernel,
        out_shape=(jax.ShapeDtypeStruct((m, Q_COLS), F32),) + (jax.ShapeDtypeStruct((m, KV_COLS), F32),) * N_DIL,
        grid=(m // tm, QKV_COLS // QKV_TN),
        in_specs=[
            pl.BlockSpec((tm, d), lambda i, j: (i, 0)),
            pl.BlockSpec((1, d), lambda i, j: (0, 0)),
            pl.BlockSpec((d, QKV_TN), lambda i, j: (0, j)),
            tab_spec, tab_spec, tab_spec,
        ],
        out_specs=(pl.BlockSpec((tm, QKV_TN), lambda i, j: (i, jnp.minimum(j, Q_TILES - 1))),
                   kv_spec(0), kv_spec(1), kv_spec(2)),
        scratch_shapes=[pltpu.VMEM((tm, d), BF16)],
        compiler_params=_cparams(("parallel", "arbitrary")),
        name="qkv_project",
    )(x, g.reshape(1, d), w, cos, sa, sb)


def _permute_qkv_cols(w):
    parts = [w[:, :Q_COLS]]
    for g in range(N_DIL):
        parts.append(w[:, Q_COLS + g * ATTN_W:Q_COLS + (g + 1) * ATTN_W])
        parts.append(w[:, 2 * Q_COLS + g * ATTN_W:2 * Q_COLS + (g + 1) * ATTN_W])
    return jnp.concatenate(parts, axis=1)


def _norm_mm_kernel(x_ref, g_ref, w_ref, o_ref, h_ref):
    @pl.when(pl.program_id(1) == 0)
    def _():
        h_ref[...] = _rms(x_ref[...], g_ref[...]).astype(BF16)

    o_ref[...] = jnp.dot(h_ref[...], w_ref[...], preferred_element_type=F32).astype(o_ref.dtype)


def norm_matmul(x, g, w, *, tm, tn, out_dtype=F32, name="norm_matmul"):
    m, d = x.shape
    n = w.shape[1]
    assert m % tm == 0 and n % tn == 0
    return pl.pallas_call(
        _norm_mm_kernel,
        out_shape=jax.ShapeDtypeStruct((m, n), out_dtype),
        grid=(m // tm, n // tn),
        in_specs=[
            pl.BlockSpec((tm, d), lambda i, j: (i, 0)),
            pl.BlockSpec((1, d), lambda i, j: (0, 0)),
            pl.BlockSpec((d, tn), lambda i, j: (0, j)),
        ],
        out_specs=pl.BlockSpec((tm, tn), lambda i, j: (i, j)),
        scratch_shapes=[pltpu.VMEM((tm, d), BF16)],
        compiler_params=_cparams(("parallel", "arbitrary")),
        name=name,
    )(x, g.reshape(1, d), w)


ATTN_QB = 256


def _band_mask(nq, nk, off):
    r = lax.broadcasted_iota(jnp.int32, (nq, nk), 0)
    c = lax.broadcasted_iota(jnp.int32, (nq, nk), 1)
    rel = r - c + off
    return (rel >= 0) & (rel <= N_BACK)


def _attend_sequence(qf, kf, vf, masks, store):
    n = qf.shape[0]
    qb = (qf * (HEAD_DIM ** -0.5)).astype(BF16)
    kb = kf.astype(BF16)
    vb = vf.astype(BF16)
    qbs = min(ATTN_QB, n)
    for i in range(n // qbs):
        s0 = i * qbs
        ks = max(0, s0 - N_BACK)
        s = lax.dot_general(qb[s0:s0 + qbs], kb[ks:s0 + qbs], (((1,), (1,)), ((), ())),
                            preferred_element_type=F32)
        s = jnp.where(masks[(qbs, s0 + qbs - ks)], s, NEG_INF)
        m = jnp.max(s, axis=-1, keepdims=True)
        p = jnp.exp(s - m)
        l = jnp.sum(p, axis=-1, keepdims=True)
        o = jnp.dot(p.astype(BF16), vb[ks:s0 + qbs], preferred_element_type=F32)
        store(s0, qbs, o, l, m + jnp.log(l))


def _attn_prompt_kernel(q0, q1, q2, k0, k1, k2, v0, v1, v2, o_ref, o_scr, l_scr, d_scr):
    qs, ks, vs = (q0, q1, q2), (k0, k1, k2), (v0, v1, v2)
    t = q0.shape[0]
    masks = {}
    for _, dil in DIL_GROUPS:
        n = t // dil
        qbs = min(ATTN_QB, n)
        masks[(qbs, qbs)] = None
        if n > qbs:
            masks[(qbs, qbs + N_BACK)] = None
    for (nq, nk) in masks:
        masks[(nq, nk)] = _band_mask(nq, nk, nk - nq)

    for g, (_, dil) in enumerate(DIL_GROUPS):
        n = t // dil

        def run(r, g=g, dil=dil, n=n):
            if dil == 1:
                rows = lambda s0, cnt: pl.ds(s0, cnt)
                qf, kf, vf = qs[g][...], ks[g][...], vs[g][...]
            else:
                rows = lambda s0, cnt: pl.ds(s0 * dil + r, cnt, stride=dil)
                qf = qs[g][pl.ds(r, n, stride=dil), :]
                kf = ks[g][pl.ds(r, n, stride=dil), :]
                vf = vs[g][pl.ds(r, n, stride=dil), :]

            def store(s0, cnt, o, l, lse):
                @pl.when(pl.program_id(1) == 0)
                def _():
                    d_scr[g, rows(s0, cnt), :] = jnp.broadcast_to(l, (cnt, LANES))

                o_scr[g, rows(s0, cnt), :] = o / d_scr[g, rows(s0, cnt), :]
                l_scr[g, rows(s0, cnt), :] = jnp.broadcast_to(lse, (cnt, LANES))

            _attend_sequence(qf, kf, vf, masks, store)

        if dil <= 4:
            for r in range(dil):
                run(r)
        else:
            def body(r, carry):
                run(r)
                return carry
            lax.fori_loop(0, dil, body, 0)

    l0, l1, l2 = l_scr[0], l_scr[1], l_scr[2]
    m = jnp.maximum(jnp.maximum(l0, l1), l2)
    e0, e1, e2 = jnp.exp(l0 - m), jnp.exp(l1 - m), jnp.exp(l2 - m)
    num = e0 * o_scr[0] + e1 * o_scr[1] + e2 * o_scr[2]
    o_ref[...] = (num / (e0 + e1 + e2)).astype(o_ref.dtype)


def attn_prompt(q, kvs):
    b, t, _ = q.shape

    def spec(col0):
        return pl.BlockSpec((None, t, HEAD_DIM), lambda bi, h: (bi, 0, col0 + h))

    in_specs = ([spec(g * HEADS) for g in range(N_DIL)]
                + [spec(0)] * N_DIL
                + [spec(HEADS)] * N_DIL)
    return pl.pallas_call(
        _attn_prompt_kernel,
        out_shape=jax.ShapeDtypeStruct((b, t, ATTN_W), BF16),
        grid=(b, HEADS),
        in_specs=in_specs,
        out_specs=pl.BlockSpec((None, t, HEAD_DIM), lambda bi, h: (bi, 0, h)),
        scratch_shapes=[pltpu.VMEM((N_DIL, t, LANES), F32)] * 3,
        compiler_params=_cparams(("parallel", "arbitrary")),
        name="attn_prompt",
    )(q, q, q, *kvs, *kvs)


def _attn_sample_kernel(q_ref, n0, n1, n2, c0, c1, c2, o_ref):
    news, caches = (n0, n1, n2), (c0, c1, c2)
    outs, lses = [], []
    for g in range(N_DIL):
        q = q_ref[g * HEADS:(g + 1) * HEADS, :] * (HEAD_DIM ** -0.5)
        k_new = news[g][0:HEADS, :]
        v_new = news[g][HEADS:2 * HEADS, :]
        kc = caches[g][:, 0:HEADS, :]
        vc = caches[g][:, HEADS:2 * HEADS, :]
        s = jnp.sum(kc * q[None], axis=-1, keepdims=True)
        s_new = jnp.sum(q * k_new, axis=-1, keepdims=True)
        m = jnp.maximum(jnp.max(s, axis=0), s_new)
        p = jnp.exp(s - m[None])
        p_new = jnp.exp(s_new - m)
        l = jnp.sum(p, axis=0) + p_new
        o = jnp.sum(p * vc, axis=0) + p_new * v_new
        outs.append(o / l[0:1])
        lses.append(m + jnp.log(l))
    m = jnp.maximum(jnp.maximum(lses[0], lses[1]), lses[2])
    es = [jnp.exp(x - m) for x in lses]
    num = es[0] * outs[0] + es[1] * outs[1] + es[2] * outs[2]
    o_ref[...] = num / (es[0] + es[1] + es[2])


def attn_sample(q, kv_new, caches, layer):
    b = q.shape[0]
    in_specs = [pl.BlockSpec((None, N_DIL * HEADS, HEAD_DIM), lambda bi: (bi, 0, 0))]
    args = [q.reshape(b, N_DIL * HEADS, HEAD_DIM)]
    for kv in kv_new:
        in_specs.append(pl.BlockSpec((None, 2 * HEADS, HEAD_DIM), lambda bi: (bi, 0, 0)))
        args.append(kv.reshape(b, 2 * HEADS, HEAD_DIM))
    for (win, dil), cache in zip(DIL_GROUPS, caches):
        n_layers, _, length = cache.shape[:3]
        assert length == win and length // dil == N_BACK
        args.append(cache.reshape(n_layers, b, N_BACK, dil * 2 * HEADS, HEAD_DIM))
        in_specs.append(pl.BlockSpec((None, None, N_BACK, 2 * HEADS, HEAD_DIM), lambda bi: (layer, bi, 0, 0, 0)))
    out = pl.pallas_call(
        _attn_sample_kernel,
        out_shape=jax.ShapeDtypeStruct((b, HEADS, HEAD_DIM), F32),
        grid=(b,),
        in_specs=in_specs,
        out_specs=pl.BlockSpec((None, HEADS, HEAD_DIM), lambda bi: (bi, 0, 0)),
        compiler_params=_cparams(("parallel",)),
        name="attn_sample",
    )(*args)
    return out.reshape(b, ATTN_W).astype(BF16)


def _mm_res_kernel(a_ref, w_ref, r_ref, g_ref, o_ref, acc, *, nk):
    k = pl.program_id(1)
    part = jnp.dot(a_ref[...], w_ref[...], preferred_element_type=F32)
    if nk == 1:
        o_ref[...] = r_ref[...] + _rms(part, g_ref[...])
        return

    @pl.when(k == 0)
    def _():
        acc[...] = part

    @pl.when(k > 0)
    def _():
        acc[...] += part

    @pl.when(k == nk - 1)
    def _():
        o_ref[...] = r_ref[...] + _rms(acc[...], g_ref[...])


def matmul_residual(a, w, resid, g, *, tm, tk, name="matmul_residual"):
    m, kk = a.shape
    d = w.shape[1]
    assert m % tm == 0 and kk % tk == 0
    nk = kk // tk
    acc_shape = (tm, d) if nk > 1 else (SUBLANES, LANES)
    return pl.pallas_call(
        functools.partial(_mm_res_kernel, nk=nk),
        out_shape=jax.ShapeDtypeStruct((m, d), F32),
        grid=(m // tm, nk),
        in_specs=[
            pl.BlockSpec((tm, tk), lambda i, k: (i, k)),
            pl.BlockSpec((tk, d), lambda i, k: (k, 0)),
            pl.BlockSpec((tm, d), lambda i, k: (i, 0)),
            pl.BlockSpec((1, d), lambda i, k: (0, 0)),
        ],
        out_specs=pl.BlockSpec((tm, d), lambda i, k: (i, 0)),
        scratch_shapes=[pltpu.VMEM(acc_shape, F32)],
        compiler_params=_cparams(("parallel", "arbitrary")),
        name=name,
    )(a, w, resid, g.reshape(1, d))


def _glu_res_kernel(a_ref, wv_ref, wg_ref, r_ref, g_ref, o_ref, y_scr, *, nj):
    j = pl.program_id(1)
    a = a_ref[...]
    val = jnp.dot(a, wv_ref[...], preferred_element_type=F32)
    gate = jnp.dot(a, wg_ref[...], preferred_element_type=F32)
    y_scr[j] = val * jax.nn.sigmoid(gate)

    @pl.when(j == nj - 1)
    def _():
        y = jnp.concatenate([y_scr[t] for t in range(nj)], axis=-1)
        o_ref[...] = r_ref[...] + _rms(y, g_ref[...])


def glu_residual(a, w, resid, g, *, tm, tn):
    m, k = a.shape
    d = w.shape[1] // 2
    nj = d // tn
    return pl.pallas_call(
        functools.partial(_glu_res_kernel, nj=nj),
        out_shape=jax.ShapeDtypeStruct((m, d), F32),
        grid=(m // tm, nj),
        in_specs=[
            pl.BlockSpec((tm, k), lambda i, j: (i, 0)),
            pl.BlockSpec((k, tn), lambda i, j: (0, j)),
            pl.BlockSpec((k, tn), lambda i, j: (0, nj + j)),
            pl.BlockSpec((tm, d), lambda i, j: (i, 0)),
            pl.BlockSpec((1, d), lambda i, j: (0, 0)),
        ],
        out_specs=pl.BlockSpec((tm, d), lambda i, j: (i, 0)),
        scratch_shapes=[pltpu.VMEM((nj, tm, tn), F32)],
        compiler_params=_cparams(("parallel", "arbitrary")),
        name="glu_residual",
    )(a, w, w, resid, g.reshape(1, d))


def _conv_gate(ug, ug1, ug2, uv, uv1, uv2, cwg, cwv, cbg, cbv):
    gate = cbg + cwg[0:1] * ug2 + cwg[1:2] * ug1 + cwg[2:3] * ug
    val = cbv + cwv[0:1] * uv2 + cwv[1:2] * uv1 + cwv[2:3] * uv
    return (gate * jax.nn.sigmoid(gate) * val).astype(BF16)


def _ffn_up_seq_kernel(x_ref, xh_ref, g_ref, wg_ref, wv_ref, cwg_ref, cwv_ref, cbg_ref, cbv_ref,
                       act_ref, sg_ref, sv_ref, h_ref, *, tiles_per_seq):
    halo = BF16_ROWS
    tm = x_ref.shape[0]

    @pl.when(pl.program_id(1) == 0)
    def _():
        g = g_ref[...]
        keep = jnp.where(pl.program_id(0) % tiles_per_seq == 0, 0.0, 1.0)
        h_ref[0:halo] = (_rms(xh_ref[...], g) * keep).astype(BF16)
        h_ref[halo:] = _rms(x_ref[...], g).astype(BF16)

    h = h_ref[...]
    ug = jnp.dot(h, wg_ref[...], preferred_element_type=F32)
    uv = jnp.dot(h, wv_ref[...], preferred_element_type=F32)
    sg_ref[...] = ug[halo + tm - SUBLANES:]
    sv_ref[...] = uv[halo + tm - SUBLANES:]
    act_ref[...] = _conv_gate(
        ug[halo:], pltpu.roll(ug, 1, axis=0)[halo:], pltpu.roll(ug, 2, axis=0)[halo:],
        uv[halo:], pltpu.roll(uv, 1, axis=0)[halo:], pltpu.roll(uv, 2, axis=0)[halo:],
        cwg_ref[...], cwv_ref[...], cbg_ref[...], cbv_ref[...])


def _ffn_up_step_kernel(x_ref, g_ref, wg_ref, wv_ref, g1_ref, v1_ref, g2_ref, v2_ref,
                        cwg_ref, cwv_ref, cbg_ref, cbv_ref, act_ref, sg_ref, sv_ref, h_ref):
    @pl.when(pl.program_id(1) == 0)
    def _():
        h_ref[...] = _rms(x_ref[...], g_ref[...]).astype(BF16)

    h = h_ref[...]
    ug = jnp.dot(h, wg_ref[...], preferred_element_type=F32)
    uv = jnp.dot(h, wv_ref[...], preferred_element_type=F32)
    sg_ref[...] = ug
    sv_ref[...] = uv
    act_ref[...] = _conv_gate(ug, g1_ref[...], g2_ref[...], uv, v1_ref[...], v2_ref[...],
                              cwg_ref[...], cwv_ref[...], cbg_ref[...], cbv_ref[...])


def ffn_up(x, g, w_gate, w_val, cw_gate, cw_val, cb_gate, cb_val, *, tm, tn, seq_len=None, prev=None):
    m, d = x.shape
    n = w_gate.shape[1]
    assert m % tm == 0 and n % tn == 0
    col = lambda i, j: (0, j)
    w_specs = [pl.BlockSpec((d, tn), col)] * 2
    c_specs = [pl.BlockSpec((CONV_WIDTH, tn), col)] * 2 + [pl.BlockSpec((1, tn), col)] * 2
    c_args = [cw_gate, cw_val, cb_gate, cb_val]
    x_spec = pl.BlockSpec((tm, d), lambda i, j: (i, 0))
    g_spec = pl.BlockSpec((1, d), lambda i, j: (0, 0))
    tile = pl.BlockSpec((tm, tn), lambda i, j: (i, j))
    if prev is None:
        assert seq_len % tm == 0 and tm % BF16_ROWS == 0
        hb = tm // BF16_ROWS
        halo_spec = pl.BlockSpec((BF16_ROWS, d), lambda i, j: (jnp.maximum(i * hb - 1, 0), 0))
        body = functools.partial(_ffn_up_seq_kernel, tiles_per_seq=seq_len // tm)
        in_specs = [x_spec, halo_spec, g_spec] + w_specs + c_specs
        args = [x, x, g.reshape(1, d), w_gate, w_val] + c_args
        st_rows, st_spec = (m // tm) * SUBLANES, pl.BlockSpec((SUBLANES, tn), lambda i, j: (i, j))
        h_rows = tm + BF16_ROWS
    else:
        (g1, v1), (g2, v2) = prev
        body = _ffn_up_step_kernel
        in_specs = [x_spec, g_spec] + w_specs + [tile] * 4 + c_specs
        args = [x, g.reshape(1, d), w_gate, w_val, g1, v1, g2, v2] + c_args
        st_rows, st_spec = m, tile
        h_rows = tm
    st_shape = jax.ShapeDtypeStruct((st_rows, n), F32)
    return pl.pallas_call(
        body,
        out_shape=(jax.ShapeDtypeStruct((m, n), BF16), st_shape, st_shape),
        grid=(m // tm, n // tn),
        in_specs=in_specs,
        out_specs=(tile, st_spec, st_spec),
        scratch_shapes=[pltpu.VMEM((h_rows, d), BF16)],
        compiler_params=_cparams(("parallel", "arbitrary")),
        name="ffn_up",
    )(*args)


def _ssm_params_kernel(lre_ref, lim_ref, ldt_ref, bre_ref, bim_ref, are_ref, aim_ref, bbre_ref, bbim_ref):
    lre = jnp.minimum(lre_ref[...], -1e-4)
    lim = lim_ref[...]
    dt = jnp.exp(ldt_ref[...])
    mag = jnp.exp(lre * dt)
    a_re = mag * jnp.cos(lim * dt)
    a_im = mag * jnp.sin(lim * dt)
    are_ref[...] = a_re
    aim_ref[...] = a_im
    x, y = a_re - 1.0, a_im
    den = lre * lre + lim * lim
    c_re = (x * lre + y * lim) / den
    c_im = (y * lre - x * lim) / den
    bre, bim = bre_ref[...], bim_ref[...]
    bbre_ref[...] = c_re * bre - c_im * bim
    bbim_ref[...] = c_re * bim + c_im * bre


def ssm_params(lam_re, lam_im, log_dt, b_re, b_im):
    g, p = lam_re.shape
    shp_a = jax.ShapeDtypeStruct((g, 1, p), F32)
    shp_b = jax.ShapeDtypeStruct((g, SSM_CH, p), F32)
    a_re, a_im, bb_re, bb_im = pl.pallas_call(
        _ssm_params_kernel,
        out_shape=(shp_a, shp_a, shp_b, shp_b),
        compiler_params=pltpu.CompilerParams(vmem_limit_bytes=VMEM_LIMIT),
        name="ssm_params",
    )(lam_re.reshape(g, 1, p), lam_im.reshape(g, 1, p), log_dt.reshape(g, 1, 1),
      jnp.swapaxes(b_re, 1, 2), jnp.swapaxes(b_im, 1, 2))
    return a_re.reshape(g, p), a_im.reshape(g, p), bb_re, bb_im


def _gelu_tanh(y):
    return 0.5 * y * (1.0 + jnp.tanh(math.sqrt(2.0 / math.pi) * (y + 0.044715 * (y * y * y))))


def _ssm_scan_kernel(u_ref, bm_ref, cm_ref, are_ref, aim_ref, d_ref, h0r_ref, h0i_ref,
                     z_ref, sr_ref, si_ref, bu_scr, xr_scr, xi_scr, *, n_chunks):
    c = pl.program_id(1)
    nb, tc = u_ref.shape[0], u_ref.shape[1]
    pitch = SSM_PITCH
    half = SSM_SLABS // 2

    @pl.when(c == 0)
    def _():
        xr_scr[...] = h0r_ref[...]
        xi_scr[...] = h0i_ref[...]

    bm = bm_ref[...]
    for b in range(nb):
        r = jnp.dot(u_ref[b].astype(BF16), bm, preferred_element_type=F32)
        for k in range(SSM_SLABS):
            bu_scr[b, k * pitch:k * pitch + tc, :] = r[:, k * LANES:(k + 1) * LANES]

    ar = are_ref[...]
    ai = aim_ref[...]

    def step(t, carry):
        new = []
        for b in range(nb):
            xr, xi = carry[b]
            re_rows = pl.ds(t, half, stride=pitch)
            im_rows = pl.ds(half * pitch + t, half, stride=pitch)
            nr = ar * xr - ai * xi + bu_scr[b, re_rows, :]
            ni = ar * xi + ai * xr + bu_scr[b, im_rows, :]
            bu_scr[b, re_rows, :] = nr
            bu_scr[b, im_rows, :] = ni
            new.append((nr, ni))
        return tuple(new)

    init = tuple((xr_scr[b], xi_scr[b]) for b in range(nb))
    fin = lax.fori_loop(0, tc, step, init, unroll=8)
    for b in range(nb):
        xr_scr[b] = fin[b][0]
        xi_scr[b] = fin[b][1]

    cm = cm_ref[...]
    for b in range(nb):
        xs = jnp.concatenate([bu_scr[b, k * pitch:k * pitch + tc, :] for k in range(SSM_SLABS)], axis=-1)
        y = jnp.dot(xs.astype(BF16), cm, preferred_element_type=F32) + d_ref[...] * u_ref[b]
        z_ref[b] = _gelu_tanh(y).astype(z_ref.dtype)

    @pl.when(c == n_chunks - 1)
    def _():
        sr_ref[...] = xr_scr[...]
        si_ref[...] = xi_scr[...]


def ssm_scan(u, bmat, cmat, a_re, a_im, d_skip, h0_re, h0_im):
    b, t, d = u.shape
    tc = SSM_CHUNK
    n_chunks = t // tc
    half = SSM_SLABS // 2
    st_spec = pl.BlockSpec((b, None, half, LANES), lambda j, c: (0, j, 0, 0))
    st_shape = jax.ShapeDtypeStruct((b, SSM_TILES, half, LANES), F32)
    return pl.pallas_call(
        functools.partial(_ssm_scan_kernel, n_chunks=n_chunks),
        out_shape=(jax.ShapeDtypeStruct((b, t, d), BF16), st_shape, st_shape),
        grid=(SSM_TILES, n_chunks),
        in_specs=[
            pl.BlockSpec((b, tc, SSM_TILE_CH), lambda j, c: (0, c, j)),
            pl.BlockSpec((None, SSM_TILE_CH, 2 * SSM_TILE_ST), lambda j, c: (j, 0, 0)),
            pl.BlockSpec((None, 2 * SSM_TILE_ST, SSM_TILE_CH), lambda j, c: (j, 0, 0)),
            pl.BlockSpec((None, half, LANES), lambda j, c: (j, 0, 0)),
            pl.BlockSpec((None, half, LANES), lambda j, c: (j, 0, 0)),
            pl.BlockSpec((1, SSM_TILE_CH), lambda j, c: (0, j)),
            st_spec, st_spec,
        ],
        out_specs=(pl.BlockSpec((b, tc, SSM_TILE_CH), lambda j, c: (0, c, j)), st_spec, st_spec),
        scratch_shapes=[
            pltpu.VMEM((b, SSM_SLABS * SSM_PITCH, LANES), F32),
            pltpu.VMEM((b, half, LANES), F32),
            pltpu.VMEM((b, half, LANES), F32),
        ],
        compiler_params=_cparams(("parallel", "arbitrary")),
        name="ssm_scan",
    )(u, bmat, cmat, a_re.reshape(SSM_TILES, half, LANES), a_im.reshape(SSM_TILES, half, LANES),
      d_skip.reshape(1, d), h0_re, h0_im)


def _ssm_step_kernel(u_ref, bm_ref, cm_ref, are_ref, aim_ref, d_ref, h0r_ref, h0i_ref, z_ref, sr_ref, si_ref):
    u = u_ref[...]
    bu = jnp.dot(u.astype(BF16), bm_ref[...], preferred_element_type=F32)
    ar, ai = are_ref[...], aim_ref[...]
    hr, hi = h0r_ref[...], h0i_ref[...]
    xr = ar * hr - ai * hi + bu[:, :SSM_TILE_ST]
    xi = ar * hi + ai * hr + bu[:, SSM_TILE_ST:]
    sr_ref[...] = xr
    si_ref[...] = xi
    xs = jnp.concatenate([xr, xi], axis=-1).astype(BF16)
    y = jnp.dot(xs, cm_ref[...], preferred_element_type=F32) + d_ref[...] * u
    z_ref[...] = _gelu_tanh(y).astype(z_ref.dtype)


def ssm_step(u, bmat, cmat, a_re, a_im, d_skip, h0_re, h0_im):
    b, d = u.shape
    st_spec = pl.BlockSpec((None, b, SSM_TILE_ST), lambda j: (j, 0, 0))
    a_spec = pl.BlockSpec((None, 1, SSM_TILE_ST), lambda j: (j, 0, 0))
    st_shape = jax.ShapeDtypeStruct((SSM_TILES, b, SSM_TILE_ST), F32)
    return pl.pallas_call(
        _ssm_step_kernel,
        out_shape=(jax.ShapeDtypeStruct((b, d), BF16), st_shape, st_shape),
        grid=(SSM_TILES,),
        in_specs=[
            pl.BlockSpec((b, SSM_TILE_CH), lambda j: (0, j)),
            pl.BlockSpec((None, SSM_TILE_CH, 2 * SSM_TILE_ST), lambda j: (j, 0, 0)),
            pl.BlockSpec((None, 2 * SSM_TILE_ST, SSM_TILE_CH), lambda j: (j, 0, 0)),
            a_spec, a_spec,
            pl.BlockSpec((1, SSM_TILE_CH), lambda j: (0, j)),
            st_spec, st_spec,
        ],
        out_specs=(pl.BlockSpec((b, SSM_TILE_CH), lambda j: (0, j)), st_spec, st_spec),
        compiler_params=_cparams(("parallel",)),
        name="ssm_step",
    )(u, bmat, cmat, a_re.reshape(SSM_TILES, 1, SSM_TILE_ST), a_im.reshape(SSM_TILES, 1, SSM_TILE_ST),
      d_skip.reshape(1, d), h0_re, h0_im)


def _block_diag_mats(bb_re, bb_im, c_re, c_im):
    eye = jnp.eye(SSM_TILE_G, dtype=F32)

    def bmat(bb):
        t = bb.reshape(SSM_TILES, SSM_TILE_G, SSM_CH, SSM_STATE)
        return jnp.einsum('jgcp,gh->jgchp', t, eye).reshape(SSM_TILES, SSM_TILE_CH, SSM_TILE_ST)

    def cmat(cc):
        t = cc.reshape(SSM_TILES, SSM_TILE_G, SSM_CH, SSM_STATE)
        return jnp.einsum('jgcp,gh->jhpgc', t, eye).reshape(SSM_TILES, SSM_TILE_ST, SSM_TILE_CH)

    b_all = jnp.concatenate([bmat(bb_re), bmat(bb_im)], axis=2).astype(BF16)
    c_all = jnp.concatenate([cmat(c_re), cmat(-c_im)], axis=1).astype(BF16)
    return b_all, c_all


def _ff_halves(a):
    pad = [(0, 0)] * (a.ndim - 1) + [(0, D_FF_PAD - D_FF)]
    return jnp.pad(a[..., :D_FF], pad), jnp.pad(a[..., D_FF:], pad)


def _ff_join(gate, val):
    return jnp.concatenate([gate[..., :D_FF], val[..., :D_FF]], axis=-1)


def kernel(x_prompt, x_sample, cache_kv_g0, cache_kv_g1, cache_kv_g2, state_ssm, state_conv, norm_g, w_qkv,
           w_attn_o, w_ssm_in, lambda_re, lambda_im, log_dt, b_re, b_im, c_re, c_im, d_skip, w_glu, w_up,
           conv_w, conv_b, w_down):
    bp, t, d = x_prompt.shape
    bs = x_sample.shape[0]
    mp = bp * t
    xp = x_prompt.reshape(mp, d)
    xs = x_sample.reshape(bs, d)
    caches = (cache_kv_g0, cache_kv_g1, cache_kv_g2)
    tm_big, tm = 1024, 512

    rope_p = rope_tables(t, 0)
    rope_s = rope_tables(SUBLANES, PAST_LEN)
    rope_s = tuple(jnp.broadcast_to(r[0:1], (bs, LANES)) for r in rope_s)

    kv_p = [[] for _ in range(N_DIL)]
    kv_s = [[] for _ in range(N_DIL)]
    ssm_p, ssm_s, conv_p, conv_s = [], [], [], []

    for i in range(DEPTH):
        li = i // N_MIXERS
        if i % N_MIXERS == 0:
            wq = _permute_qkv_cols(w_qkv[li]).astype(BF16)
            wo = w_attn_o[li].astype(BF16)
            q_p, *kvs_p = qkv_project(xp, norm_g[i, 0], wq, rope_p, tm=tm_big)
            q_s, *kvs_s = qkv_project(xs, norm_g[i, 0], wq, rope_s, tm=bs)
            op = attn_prompt(q_p.reshape(bp, t, Q_COLS), [kv.reshape(bp, t, KV_COLS) for kv in kvs_p])
            os_ = attn_sample(q_s, kvs_s, caches, li)
            xp = matmul_residual(op.reshape(mp, ATTN_W), wo, xp, norm_g[i, 1], tm=tm, tk=ATTN_W, name="attn_out")
            xs = matmul_residual(os_, wo, xs, norm_g[i, 1], tm=bs, tk=ATTN_W, name="attn_out")
            for g, (win, _) in enumerate(DIL_GROUPS):
                keep = min(win, t)
                kv_p[g].append(kvs_p[g].reshape(bp, t, 2, HEADS, HEAD_DIM)[:, t - keep:])
                kv_s[g].append(kvs_s[g].reshape(bs, 1, 2, HEADS, HEAD_DIM))
        else:
            w_in = w_ssm_in[li].astype(BF16)
            wg = w_glu[li].astype(BF16)
            a_re, a_im, bb_re, bb_im = ssm_params(lambda_re[li], lambda_im[li], log_dt[li], b_re[li], b_im[li])
            bmat, cmat = _block_diag_mats(bb_re, bb_im, c_re[li], c_im[li])
            half = SSM_SLABS // 2
            up_ = norm_matmul(xp, norm_g[i, 0], w_in, tm=tm_big, tn=512, name="ssm_in")
            us_ = norm_matmul(xs, norm_g[i, 0], w_in, tm=bs, tn=512, name="ssm_in")
            zeros = jnp.zeros((bp, SSM_TILES, half, LANES), F32)
            zp, sp_re, sp_im = ssm_scan(up_.reshape(bp, t, d), bmat, cmat, a_re, a_im, d_skip[li], zeros, zeros)
            h0 = state_ssm[li].reshape(bs, SSM_TILES, SSM_TILE_ST, 2)
            h0_re = jnp.swapaxes(h0[..., 0], 0, 1)
            h0_im = jnp.swapaxes(h0[..., 1], 0, 1)
            zs, ss_re, ss_im = ssm_step(us_, bmat, cmat, a_re, a_im, d_skip[li], h0_re, h0_im)
            xp = glu_residual(zp.reshape(mp, d), wg, xp, norm_g[i, 1], tm=tm, tn=512)
            xs = glu_residual(zs, wg, xs, norm_g[i, 1], tm=bs, tn=512)
            ssm_p.append(jnp.stack([sp_re.reshape(bp, SSM_GROUPS, SSM_STATE),
                                    sp_im.reshape(bp, SSM_GROUPS, SSM_STATE)], axis=-1))
            ssm_s.append(jnp.stack([jnp.swapaxes(ss_re, 0, 1).reshape(bs, SSM_GROUPS, SSM_STATE),
                                    jnp.swapaxes(ss_im, 0, 1).reshape(bs, SSM_GROUPS, SSM_STATE)], axis=-1))

        wu_g, wu_v = (w.astype(BF16) for w in _ff_halves(w_up[i]))
        cw_g, cw_v = _ff_halves(conv_w[i])
        cb_g, cb_v = _ff_halves(conv_b[i].reshape(1, 2 * D_FF))
        wd = jnp.pad(w_down[i], ((0, D_FF_PAD - D_FF), (0, 0))).astype(BF16)
        st_g, st_v = _ff_halves(state_conv[i])
        act_p, sg_p, sv_p = ffn_up(xp, norm_g[i, 2], wu_g, wu_v, cw_g, cw_v, cb_g, cb_v,
                                   tm=tm_big, tn=256, seq_len=t)
        act_s, sg_s, sv_s = ffn_up(xs, norm_g[i, 2], wu_g, wu_v, cw_g, cw_v, cb_g, cb_v, tm=bs, tn=512,
                                   prev=((st_g[:, 1], st_v[:, 1]), (st_g[:, 0], st_v[:, 0])))
        xp = matmul_residual(act_p, wd, xp, norm_g[i, 3], tm=tm, tk=FF_DOWN_TK, name="ffn_down")
        xs = matmul_residual(act_s, wd, xs, norm_g[i, 3], tm=bs, tk=FF_DOWN_TK, name="ffn_down")
        tiles = t // tm_big
        tail = _ff_join(sg_p, sv_p).reshape(bp, tiles, SUBLANES, 2 * D_FF)
        conv_p.append(tail[:, tiles - 1, SUBLANES - (CONV_WIDTH - 1):])
        conv_s.append(jnp.concatenate([state_conv[i][:, 1:], _ff_join(sg_s, sv_s)[:, None]], axis=1))

    return (xp.reshape(bp, t, d), xs.reshape(bs, 1, d),
            jnp.stack(kv_p[0]), jnp.stack(kv_s[0]),
            jnp.stack(kv_p[1]), jnp.stack(kv_s[1]),
            jnp.stack(kv_p[2]), jnp.stack(kv_s[2]),
            jnp.stack(ssm_p), jnp.stack(ssm_s),
            jnp.stack(conv_p), jnp.stack(conv_s))
```

```python
import functools
import math

import jax
import jax.numpy as jnp
from jax import lax
from jax.experimental import pallas as pl
from jax.experimental.pallas import tpu as pltpu

F32 = jnp.float32
BF16 = jnp.bfloat16

D_MODEL = 2048
DEPTH = 4
PAST_LEN = 16384
N_MIXERS = 2
HEAD_DIM = 128
HEADS = D_MODEL // (2 * HEAD_DIM)
DIL_GROUPS = ((128, 1), (512, 4), (2048, 16))
N_DIL = len(DIL_GROUPS)
N_BACK = 128
ROT_DIM = HEAD_DIM // 4
ROPE_THETA = 500000.0
NEG_INF = -1e30
SSM_CH = 16
SSM_GROUPS = D_MODEL // SSM_CH
SSM_STATE = 64
D_FF = ((8 * D_MODEL // 3 + 127) // 128) * 128
CONV_WIDTH = 3
RMS_EPS = 1e-6

ATTN_W = HEADS * HEAD_DIM
Q_COLS = N_DIL * ATTN_W
KV_COLS = 2 * ATTN_W
QKV_COLS = 3 * N_DIL * ATTN_W

LANES = 128
SUBLANES = 8
BF16_ROWS = 16
VMEM_LIMIT = 52 * 1024 * 1024
ROW_CHUNK = 256

FF_ALIGN = 512
D_FF_PAD = ((D_FF + FF_ALIGN - 1) // FF_ALIGN) * FF_ALIGN

SSM_TILE_G = 16
SSM_TILES = SSM_GROUPS // SSM_TILE_G
SSM_TILE_CH = SSM_TILE_G * SSM_CH
SSM_TILE_ST = SSM_TILE_G * SSM_STATE
SSM_SLABS = 2 * SSM_TILE_ST // LANES
SSM_CHUNK = 256
SSM_PITCH = SSM_CHUNK + SUBLANES


def _cparams(sem):
    return pltpu.CompilerParams(dimension_semantics=sem, vmem_limit_bytes=VMEM_LIMIT)


def _rms(x, g):
    ms = jnp.mean(x * x, axis=-1, keepdims=True)
    return x * lax.rsqrt(ms + RMS_EPS) * g


def _rope_table_kernel(inv_ref, cos_ref, sa_ref, sb_ref, *, pos_base):
    rows = cos_ref.shape[0]
    pos = (lax.broadcasted_iota(jnp.int32, (rows, LANES), 0) + pos_base).astype(F32)
    lane = lax.broadcasted_iota(jnp.int32, (rows, LANES), 1)
    ang = pos * inv_ref[...]
    c = jnp.cos(ang)
    s = jnp.sin(ang)
    half = ROT_DIM // 2
    cos_ref[...] = jnp.where(lane < ROT_DIM, c, 1.0)
    sa_ref[...] = jnp.where(lane < half, -s, 0.0)
    sb_ref[...] = jnp.where((lane >= half) & (lane < ROT_DIM), s, 0.0)


def rope_tables(rows, pos_base):
    half = ROT_DIM // 2
    inv = ROPE_THETA ** (-(jnp.arange(half, dtype=F32) * (2.0 / ROT_DIM)))
    inv_lane = jnp.tile(inv, LANES // half).reshape(1, LANES)
    shp = jax.ShapeDtypeStruct((rows, LANES), F32)
    return pl.pallas_call(
        functools.partial(_rope_table_kernel, pos_base=pos_base),
        out_shape=(shp, shp, shp),
        name="rope_tables",
    )(inv_lane)


QKV_TN = 512
Q_TILES = Q_COLS // QKV_TN
KV_TILES = KV_COLS // QKV_TN
assert KV_TILES & (KV_TILES - 1) == 0


def _qkv_kernel(x_ref, g_ref, w_ref, cos_ref, sa_ref, sb_ref, o_ref, h_ref):
    j = pl.program_id(1)

    @pl.when(j == 0)
    def _():
        h_ref[...] = _rms(x_ref[...], g_ref[...]).astype(BF16)

    is_rope = (j < Q_TILES) | (((j - Q_TILES) & (KV_TILES - 1)) < KV_TILES // 2)
    w = w_ref[...]
    half = ROT_DIM // 2
    tm = x_ref.shape[0]
    rc = min(ROW_CHUNK, tm)
    for c in range(tm // rc):
        rows = slice(c * rc, (c + 1) * rc)
        y = jnp.dot(h_ref[rows, :], w, preferred_element_type=F32)
        cs, sa, sb = cos_ref[rows, :], sa_ref[rows, :], sb_ref[rows, :]
        for h in range(QKV_TN // HEAD_DIM):
            xh = y[:, h * HEAD_DIM:(h + 1) * HEAD_DIM]
            up = pltpu.roll(xh, HEAD_DIM - half, axis=1)
            dn = pltpu.roll(xh, half, axis=1)
            o_ref[rows, h * HEAD_DIM:(h + 1) * HEAD_DIM] = jnp.where(is_rope, xh * cs + up * sa + dn * sb, xh)


def qkv_project(x, g, w_all, layer, rope, *, tm):
    m, d = x.shape
    cos, sa, sb = rope
    tab_tiles = cos.shape[0] // tm
    assert m % tm == 0 and cos.shape[0] % tm == 0 and w_all.shape[2] == QKV_COLS
    tab_spec = pl.BlockSpec((tm, LANES), lambda i, j: (i % tab_tiles, 0))
    return pl.pallas_call(
        _qkv_kernel,
        out_shape=jax.ShapeDtypeStruct((m, QKV_COLS), F32),
        grid=(m // tm, QKV_COLS // QKV_TN),
        in_specs=[
            pl.BlockSpec((tm, d), lambda i, j: (i, 0)),
            pl.BlockSpec((1, d), lambda i, j: (0, 0)),
            pl.BlockSpec((None, d, QKV_TN), lambda i, j: (layer, 0, j)),
            tab_spec, tab_spec, tab_spec,
        ],
        out_specs=pl.BlockSpec((tm, QKV_TN), lambda i, j: (i, j)),
        scratch_shapes=[pltpu.VMEM((tm, d), BF16)],
        compiler_params=_cparams(("parallel", "arbitrary")),
        name="qkv_project",
    )(x, g.reshape(1, d), w_all, cos, sa, sb)


def _permute_qkv_cols(w):
    parts = [w[..., :Q_COLS]]
    for g in range(N_DIL):
        parts.append(w[..., Q_COLS + g * ATTN_W:Q_COLS + (g + 1) * ATTN_W])
        parts.append(w[..., 2 * Q_COLS + g * ATTN_W:2 * Q_COLS + (g + 1) * ATTN_W])
    return jnp.concatenate(parts, axis=-1)


def _kv_col(g):
    return Q_COLS + g * KV_COLS


def _norm_mm_kernel(x_ref, g_ref, w_ref, o_ref, h_ref):
    @pl.when(pl.program_id(1) == 0)
    def _():
        h_ref[...] = _rms(x_ref[...], g_ref[...]).astype(BF16)

    o_ref[...] = jnp.dot(h_ref[...], w_ref[...], preferred_element_type=F32).astype(o_ref.dtype)


def norm_matmul(x, g, w_all, layer, *, tm, tn, out_dtype=F32, name="norm_matmul"):
    m, d = x.shape
    n = w_all.shape[2]
    assert m % tm == 0 and n % tn == 0
    return pl.pallas_call(
        _norm_mm_kernel,
        out_shape=jax.ShapeDtypeStruct((m, n), out_dtype),
        grid=(m // tm, n // tn),
        in_specs=[
            pl.BlockSpec((tm, d), lambda i, j: (i, 0)),
            pl.BlockSpec((1, d), lambda i, j: (0, 0)),
            pl.BlockSpec((None, d, tn), lambda i, j: (layer, 0, j)),
        ],
        out_specs=pl.BlockSpec((tm, tn), lambda i, j: (i, j)),
        scratch_shapes=[pltpu.VMEM((tm, d), BF16)],
        compiler_params=_cparams(("parallel", "arbitrary")),
        name=name,
    )(x, g.reshape(1, d), w_all)


ATTN_BLK = 128
assert ATTN_BLK == N_BACK


def _residue_major(ref, dil):
    t = ref.shape[0]
    if dil == 1:
        return ref[...]
    return jnp.concatenate([ref[pl.ds(r, t // dil, stride=dil), :] for r in range(dil)], axis=0)


def _store_token_order(ref, g, dil, val):
    nb = val.shape[0]
    if dil == 1:
        ref[g] = val.reshape(nb * ATTN_BLK, LANES)
        return
    per = nb // dil
    for r in range(dil):
        ref[g, pl.ds(r, per * ATTN_BLK, stride=dil), :] = val[r * per:(r + 1) * per].reshape(per * ATTN_BLK, LANES)


def _attn_prompt_kernel(q0, q1, q2, k0, k1, k2, v0, v1, v2, o_ref, o_scr, l_scr, d_scr):
    qs, ks, vs = (q0, q1, q2), (k0, k1, k2), (v0, v1, v2)
    t = q0.shape[0]
    nb = t // ATTN_BLK
    blk = ATTN_BLK

    for g, (_, dil) in enumerate(DIL_GROUPS):
        per = nb // dil
        to_blocks = lambda a: a.reshape(nb, blk, HEAD_DIM)
        qb = to_blocks((_residue_major(qs[g], dil) * (HEAD_DIM ** -0.5)).astype(BF16))
        kb = to_blocks(_residue_major(ks[g], dil).astype(BF16))
        vb = to_blocks(_residue_major(vs[g], dil).astype(BF16))
        if per > 1:
            kk = jnp.concatenate([jnp.concatenate([kb[:1], kb[:-1]], axis=0), kb], axis=1)
            vv = jnp.concatenate([jnp.concatenate([vb[:1], vb[:-1]], axis=0), vb], axis=1)
            nk = 2 * blk
            bi = lax.broadcasted_iota(jnp.int32, (nb, 1, nk), 0)
            ci = lax.broadcasted_iota(jnp.int32, (nb, 1, nk), 2)
            no_prev = jnp.where(((bi & (per - 1)) == 0) & (ci < blk), NEG_INF, 0.0)
        else:
            kk, vv, nk = kb, vb, blk
            no_prev = None
        rel = (lax.broadcasted_iota(jnp.int32, (blk, nk), 0) + (nk - blk)
               - lax.broadcasted_iota(jnp.int32, (blk, nk), 1))
        band = jnp.where((rel >= 0) & (rel <= N_BACK), 0.0, NEG_INF)

        s = jnp.einsum('bqd,bkd->bqk', qb, kk, preferred_element_type=F32) + band[None]
        if no_prev is not None:
            s = s + no_prev
        m = jnp.max(s, axis=-1, keepdims=True)
        p = jnp.exp(s - m)
        l = jnp.sum(p, axis=-1, keepdims=True)
        o = jnp.einsum('bqk,bkd->bqd', p.astype(BF16), vv, preferred_element_type=F32)

        @pl.when(pl.program_id(1) == 0)
        def _(g=g, l=l):
            d_scr[g] = jnp.broadcast_to(l, (nb, blk, LANES))

        _store_token_order(o_scr, g, dil, o / d_scr[g])
        _store_token_order(l_scr, g, dil, jnp.broadcast_to(m + jnp.log(l), (nb, blk, LANES)))

    l0, l1, l2 = l_scr[0], l_scr[1], l_scr[2]
    m = jnp.maximum(jnp.maximum(l0, l1), l2)
    e0, e1, e2 = jnp.exp(l0 - m), jnp.exp(l1 - m), jnp.exp(l2 - m)
    num = e0 * o_scr[0] + e1 * o_scr[1] + e2 * o_scr[2]
    o_ref[...] = (num / (e0 + e1 + e2)).astype(o_ref.dtype)


def attn_prompt(qkv):
    b, t, _ = qkv.shape
    assert t % (ATTN_BLK * DIL_GROUPS[-1][1]) == 0

    def spec(col0):
        return pl.BlockSpec((None, t, HEAD_DIM), lambda bi, h: (bi, 0, col0 // HEAD_DIM + h))

    in_specs = ([spec(g * ATTN_W) for g in range(N_DIL)]
                + [spec(_kv_col(g)) for g in range(N_DIL)]
                + [spec(_kv_col(g) + ATTN_W) for g in range(N_DIL)])
    nb = t // ATTN_BLK
    return pl.pallas_call(
        _attn_prompt_kernel,
        out_shape=jax.ShapeDtypeStruct((b, t, ATTN_W), BF16),
        grid=(b, HEADS),
        in_specs=in_specs,
        out_specs=pl.BlockSpec((None, t, HEAD_DIM), lambda bi, h: (bi, 0, h)),
        scratch_shapes=[pltpu.VMEM((N_DIL, t, LANES), F32), pltpu.VMEM((N_DIL, t, LANES), F32),
                        pltpu.VMEM((N_DIL, nb, ATTN_BLK, LANES), F32)],
        compiler_params=_cparams(("parallel", "arbitrary")),
        name="attn_prompt",
    )(*([qkv] * 9))


def _attn_sample_kernel(x_ref, c0, c1, c2, o_ref):
    caches = (c0, c1, c2)
    outs, lses = [], []
    for g in range(N_DIL):
        r0 = _kv_col(g) // HEAD_DIM
        q = x_ref[g * HEADS:(g + 1) * HEADS, :] * (HEAD_DIM ** -0.5)
        k_new = x_ref[r0:r0 + HEADS, :]
        v_new = x_ref[r0 + HEADS:r0 + 2 * HEADS, :]
        kc = caches[g][:, 0:HEADS, :]
        vc = caches[g][:, HEADS:2 * HEADS, :]
        s = jnp.sum(kc * q[None], axis=-1, keepdims=True)
        s_new = jnp.sum(q * k_new, axis=-1, keepdims=True)
        m = jnp.maximum(jnp.max(s, axis=0), s_new)
        p = jnp.exp(s - m[None])
        p_new = jnp.exp(s_new - m)
        l = jnp.sum(p, axis=0) + p_new
        o = jnp.sum(p * vc, axis=0) + p_new * v_new
        outs.append(o / l[0:1])
        lses.append(m + jnp.log(l))
    m = jnp.maximum(jnp.maximum(lses[0], lses[1]), lses[2])
    es = [jnp.exp(x - m) for x in lses]
    num = es[0] * outs[0] + es[1] * outs[1] + es[2] * outs[2]
    o_ref[...] = num / (es[0] + es[1] + es[2])


def attn_sample(qkv, caches, layer):
    b = qkv.shape[0]
    rows = QKV_COLS // HEAD_DIM
    in_specs = [pl.BlockSpec((None, rows, HEAD_DIM), lambda bi: (bi, 0, 0))]
    args = [qkv.reshape(b, rows, HEAD_DIM)]
    for (win, dil), cache in zip(DIL_GROUPS, caches):
        n_layers, _, length = cache.shape[:3]
        assert length == win and length // dil == N_BACK
        args.append(cache.reshape(n_layers, b, N_BACK, dil * 2 * HEADS, HEAD_DIM))
        in_specs.append(pl.BlockSpec((None, None, N_BACK, 2 * HEADS, HEAD_DIM), lambda bi: (layer, bi, 0, 0, 0)))
    out = pl.pallas_call(
        _attn_sample_kernel,
        out_shape=jax.ShapeDtypeStruct((b, HEADS, HEAD_DIM), F32),
        grid=(b,),
        in_specs=in_specs,
        out_specs=pl.BlockSpec((None, HEADS, HEAD_DIM), lambda bi: (bi, 0, 0)),
        compiler_params=_cparams(("parallel",)),
        name="attn_sample",
    )(*args)
    return out.reshape(b, ATTN_W).astype(BF16)


def _mm_res_kernel(a_ref, *refs, nj, gated):
    if gated:
        wv_ref, wg_ref, r_ref, g_ref, o_ref, y_scr = refs
    else:
        wv_ref, r_ref, g_ref, o_ref, y_scr = refs
    j = pl.program_id(1)
    a = a_ref[...]
    y = jnp.dot(a, wv_ref[...], preferred_element_type=F32)
    if gated:
        y = y * jax.nn.sigmoid(jnp.dot(a, wg_ref[...], preferred_element_type=F32))
    if nj == 1:
        o_ref[...] = r_ref[...] + _rms(y, g_ref[...])
        return
    y_scr[j] = y

    @pl.when(j == nj - 1)
    def _():
        full = jnp.concatenate([y_scr[t] for t in range(nj)], axis=-1)
        o_ref[...] = r_ref[...] + _rms(full, g_ref[...])


def matmul_residual(a, w_all, layer, resid, g, *, tm, tn, gated=False, name="matmul_residual"):
    m, k = a.shape
    d = resid.shape[1]
    assert m % tm == 0 and d % tn == 0 and w_all.shape[2] == (2 * d if gated else d)
    nj = d // tn
    w_specs = [pl.BlockSpec((None, k, tn), lambda i, j: (layer, 0, j))]
    if gated:
        w_specs.append(pl.BlockSpec((None, k, tn), lambda i, j: (layer, 0, nj + j)))
    scr_shape = (nj, tm, tn) if nj > 1 else (1, SUBLANES, LANES)
    return pl.pallas_call(
        functools.partial(_mm_res_kernel, nj=nj, gated=gated),
        out_shape=jax.ShapeDtypeStruct((m, d), F32),
        grid=(m // tm, nj),
        in_specs=[pl.BlockSpec((tm, k), lambda i, j: (i, 0))] + w_specs + [
            pl.BlockSpec((tm, d), lambda i, j: (i, 0)),
            pl.BlockSpec((1, d), lambda i, j: (0, 0)),
        ],
        out_specs=pl.BlockSpec((tm, d), lambda i, j: (i, 0)),
        scratch_shapes=[pltpu.VMEM(scr_shape, F32)],
        compiler_params=_cparams(("parallel", "arbitrary")),
        name=name,
    )(a, *([w_all] * len(w_specs)), resid, g.reshape(1, d))


def _conv_gate(ug, ug1, ug2, uv, uv1, uv2, cwg, cwv, cbg, cbv):
    gate = cbg + cwg[0:1] * ug2 + cwg[1:2] * ug1 + cwg[2:3] * ug
    val = cbv + cwv[0:1] * uv2 + cwv[1:2] * uv1 + cwv[2:3] * uv
    return (gate * jax.nn.sigmoid(gate) * val).astype(BF16)


def _ffn_up_seq_kernel(x_ref, xh_ref, g_ref, wg_ref, wv_ref, cwg_ref, cwv_ref, cbg_ref, cbv_ref,
                       act_ref, sg_ref, sv_ref, h_ref, *, tiles_per_seq):
    halo = BF16_ROWS
    tm = x_ref.shape[0]

    @pl.when(pl.program_id(1) == 0)
    def _():
        g = g_ref[...]
        keep = jnp.where(pl.program_id(0) % tiles_per_seq == 0, 0.0, 1.0)
        h_ref[0:halo] = (_rms(xh_ref[...], g) * keep).astype(BF16)
        h_ref[halo:] = _rms(x_ref[...], g).astype(BF16)

    wg, wv = wg_ref[...], wv_ref[...]
    cwg, cwv, cbg, cbv = cwg_ref[...], cwv_ref[...], cbg_ref[...], cbv_ref[...]
    rc = min(ROW_CHUNK, tm)
    n_chunks = tm // rc
    for c in range(n_chunks):
        hc = h_ref[c * rc:(c + 1) * rc + halo, :]
        ug = jnp.dot(hc, wg, preferred_element_type=F32)
        uv = jnp.dot(hc, wv, preferred_element_type=F32)
        act_ref[c * rc:(c + 1) * rc, :] = _conv_gate(
            ug[halo:], pltpu.roll(ug, 1, axis=0)[halo:], pltpu.roll(ug, 2, axis=0)[halo:],
            uv[halo:], pltpu.roll(uv, 1, axis=0)[halo:], pltpu.roll(uv, 2, axis=0)[halo:],
            cwg, cwv, cbg, cbv)
        if c == n_chunks - 1:
            sg_ref[...] = ug[halo + rc - SUBLANES:]
            sv_ref[...] = uv[halo + rc - SUBLANES:]


def _ffn_up_step_kernel(x_ref, g_ref, wg_ref, wv_ref, g1_ref, v1_ref, g2_ref, v2_ref,
                        cwg_ref, cwv_ref, cbg_ref, cbv_ref, act_ref, sg_ref, sv_ref, h_ref):
    @pl.when(pl.program_id(1) == 0)
    def _():
        h_ref[...] = _rms(x_ref[...], g_ref[...]).astype(BF16)

    h = h_ref[...]
    ug = jnp.dot(h, wg_ref[...], preferred_element_type=F32)
    uv = jnp.dot(h, wv_ref[...], preferred_element_type=F32)
    sg_ref[...] = ug
    sv_ref[...] = uv
    act_ref[...] = _conv_gate(ug, g1_ref[...], g2_ref[...], uv, v1_ref[...], v2_ref[...],
                              cwg_ref[...], cwv_ref[...], cbg_ref[...], cbv_ref[...])


def ffn_up(x, g, w_all, cw_all, cb_all, layer, *, tm, tn, seq_len=None, prev_all=None):
    m, d = x.shape
    n = w_all.shape[3]
    assert m % tm == 0 and n % tn == 0
    gv = lambda rows: [pl.BlockSpec((None, None, rows, tn), lambda i, j, s=s: (layer, s, 0, j)) for s in range(2)]
    w_specs = gv(d)
    c_specs = gv(CONV_WIDTH) + gv(1)
    c_args = [cw_all, cw_all, cb_all, cb_all]
    x_spec = pl.BlockSpec((tm, d), lambda i, j: (i, 0))
    g_spec = pl.BlockSpec((1, d), lambda i, j: (0, 0))
    tile = pl.BlockSpec((tm, tn), lambda i, j: (i, j))
    if prev_all is None:
        assert seq_len % tm == 0 and tm % BF16_ROWS == 0
        hb = tm // BF16_ROWS
        halo_spec = pl.BlockSpec((BF16_ROWS, d), lambda i, j: (jnp.maximum(i * hb - 1, 0), 0))
        body = functools.partial(_ffn_up_seq_kernel, tiles_per_seq=seq_len // tm)
        in_specs = [x_spec, halo_spec, g_spec] + w_specs + c_specs
        args = [x, x, g.reshape(1, d), w_all, w_all] + c_args
        st_rows, st_spec = (m // tm) * SUBLANES, pl.BlockSpec((SUBLANES, tn), lambda i, j: (i, j))
        h_rows = tm + BF16_ROWS
    else:
        body = _ffn_up_step_kernel
        prev_specs = [pl.BlockSpec((None, None, None, tm, tn), lambda i, j, r=r, s=s: (layer, r, s, i, j))
                      for r in (1, 0) for s in range(2)]
        in_specs = [x_spec, g_spec] + w_specs + prev_specs + c_specs
        args = [x, g.reshape(1, d), w_all, w_all] + [prev_all] * 4 + c_args
        st_rows, st_spec = m, tile
        h_rows = tm
    st_shape = jax.ShapeDtypeStruct((st_rows, n), F32)
    return pl.pallas_call(
        body,
        out_shape=(jax.ShapeDtypeStruct((m, n), BF16), st_shape, st_shape),
        grid=(m // tm, n // tn),
        in_specs=in_specs,
        out_specs=(tile, st_spec, st_spec),
        scratch_shapes=[pltpu.VMEM((h_rows, d), BF16)],
        compiler_params=_cparams(("parallel", "arbitrary")),
        name="ffn_up",
    )(*args)


def _ssm_params_kernel(lre_ref, lim_ref, ldt_ref, bre_ref, bim_ref, are_ref, aim_ref, bbre_ref, bbim_ref):
    lre = jnp.minimum(lre_ref[...], -1e-4)
    lim = lim_ref[...]
    dt = jnp.exp(ldt_ref[...])
    mag = jnp.exp(lre * dt)
    a_re = mag * jnp.cos(lim * dt)
    a_im = mag * jnp.sin(lim * dt)
    are_ref[...] = a_re
    aim_ref[...] = a_im
    x, y = a_re - 1.0, a_im
    den = lre * lre + lim * lim
    c_re = (x * lre + y * lim) / den
    c_im = (y * lre - x * lim) / den
    bre, bim = bre_ref[...], bim_ref[...]
    bbre_ref[...] = c_re * bre - c_im * bim
    bbim_ref[...] = c_re * bim + c_im * bre


def ssm_params(lam_re, lam_im, log_dt, b_re, b_im):
    g, p = lam_re.shape
    shp_a = jax.ShapeDtypeStruct((g, 1, p), F32)
    shp_b = jax.ShapeDtypeStruct((g, SSM_CH, p), F32)
    a_re, a_im, bb_re, bb_im = pl.pallas_call(
        _ssm_params_kernel,
        out_shape=(shp_a, shp_a, shp_b, shp_b),
        compiler_params=pltpu.CompilerParams(vmem_limit_bytes=VMEM_LIMIT),
        name="ssm_params",
    )(lam_re.reshape(g, 1, p), lam_im.reshape(g, 1, p), log_dt.reshape(g, 1, 1),
      jnp.swapaxes(b_re, 1, 2), jnp.swapaxes(b_im, 1, 2))
    return a_re.reshape(g, p), a_im.reshape(g, p), bb_re, bb_im


def _gelu_tanh(y):
    return 0.5 * y * (1.0 + jnp.tanh(math.sqrt(2.0 / math.pi) * (y + 0.044715 * (y * y * y))))


def _ssm_scan_kernel(u_ref, bm_ref, cm_ref, are_ref, aim_ref, d_ref, h0r_ref, h0i_ref,
                     z_ref, sr_ref, si_ref, bu_scr, xr_scr, xi_scr, *, n_chunks):
    c = pl.program_id(1)
    nb, tc = u_ref.shape[0], u_ref.shape[1]
    pitch = SSM_PITCH
    half = SSM_SLABS // 2

    @pl.when(c == 0)
    def _():
        xr_scr[...] = h0r_ref[...]
        xi_scr[...] = h0i_ref[...]

    bm = bm_ref[...]
    for b in range(nb):
        r = jnp.dot(u_ref[b].astype(BF16), bm, preferred_element_type=F32)
        for k in range(SSM_SLABS):
            bu_scr[b, k * pitch:k * pitch + tc, :] = r[:, k * LANES:(k + 1) * LANES]

    ar = are_ref[...]
    ai = aim_ref[...]

    def step(t, carry):
        new = []
        for b in range(nb):
            xr, xi = carry[b]
            re_rows = pl.ds(t, half, stride=pitch)
            im_rows = pl.ds(half * pitch + t, half, stride=pitch)
            nr = ar * xr - ai * xi + bu_scr[b, re_rows, :]
            ni = ar * xi + ai * xr + bu_scr[b, im_rows, :]
            bu_scr[b, re_rows, :] = nr
            bu_scr[b, im_rows, :] = ni
            new.append((nr, ni))
        return tuple(new)

    init = tuple((xr_scr[b], xi_scr[b]) for b in range(nb))
    fin = lax.fori_loop(0, tc, step, init, unroll=8)
    for b in range(nb):
        xr_scr[b] = fin[b][0]
        xi_scr[b] = fin[b][1]

    cm = cm_ref[...]
    for b in range(nb):
        xs = jnp.concatenate([bu_scr[b, k * pitch:k * pitch + tc, :] for k in range(SSM_SLABS)], axis=-1)
        y = jnp.dot(xs.astype(BF16), cm, preferred_element_type=F32) + d_ref[...] * u_ref[b]
        z_ref[b] = _gelu_tanh(y).astype(z_ref.dtype)

    @pl.when(c == n_chunks - 1)
    def _():
        sr_ref[...] = xr_scr[...]
        si_ref[...] = xi_scr[...]


def ssm_scan(u, bmat, cmat, a_re, a_im, d_skip, h0_re, h0_im):
    b, t, d = u.shape
    tc = SSM_CHUNK
    n_chunks = t // tc
    half = SSM_SLABS // 2
    st_spec = pl.BlockSpec((b, None, half, LANES), lambda j, c: (0, j, 0, 0))
    st_shape = jax.ShapeDtypeStruct((b, SSM_TILES, half, LANES), F32)
    return pl.pallas_call(
        functools.partial(_ssm_scan_kernel, n_chunks=n_chunks),
        out_shape=(jax.ShapeDtypeStruct((b, t, d), BF16), st_shape, st_shape),
        grid=(SSM_TILES, n_chunks),
        in_specs=[
            pl.BlockSpec((b, tc, SSM_TILE_CH), lambda j, c: (0, c, j)),
            pl.BlockSpec((None, SSM_TILE_CH, 2 * SSM_TILE_ST), lambda j, c: (j, 0, 0)),
            pl.BlockSpec((None, 2 * SSM_TILE_ST, SSM_TILE_CH), lambda j, c: (j, 0, 0)),
            pl.BlockSpec((None, half, LANES), lambda j, c: (j, 0, 0)),
            pl.BlockSpec((None, half, LANES), lambda j, c: (j, 0, 0)),
            pl.BlockSpec((1, SSM_TILE_CH), lambda j, c: (0, j)),
            st_spec, st_spec,
        ],
        out_specs=(pl.BlockSpec((b, tc, SSM_TILE_CH), lambda j, c: (0, c, j)), st_spec, st_spec),
        scratch_shapes=[
            pltpu.VMEM((b, SSM_SLABS * SSM_PITCH, LANES), F32),
            pltpu.VMEM((b, half, LANES), F32),
            pltpu.VMEM((b, half, LANES), F32),
        ],
        compiler_params=_cparams(("parallel", "arbitrary")),
        name="ssm_scan",
    )(u, bmat, cmat, a_re.reshape(SSM_TILES, half, LANES), a_im.reshape(SSM_TILES, half, LANES),
      d_skip.reshape(1, d), h0_re, h0_im)


def _ssm_step_kernel(u_ref, bm_ref, cm_ref, are_ref, aim_ref, d_ref, h0r_ref, h0i_ref, z_ref, sr_ref, si_ref):
    u = u_ref[...]
    bu = jnp.dot(u.astype(BF16), bm_ref[...], preferred_element_type=F32)
    ar, ai = are_ref[...], aim_ref[...]
    hr, hi = h0r_ref[...], h0i_ref[...]
    xr = ar * hr - ai * hi + bu[:, :SSM_TILE_ST]
    xi = ar * hi + ai * hr + bu[:, SSM_TILE_ST:]
    sr_ref[...] = xr
    si_ref[...] = xi
    xs = jnp.concatenate([xr, xi], axis=-1).astype(BF16)
    y = jnp.dot(xs, cm_ref[...], preferred_element_type=F32) + d_ref[...] * u
    z_ref[...] = _gelu_tanh(y).astype(z_ref.dtype)


def ssm_step(u, bmat, cmat, a_re, a_im, d_skip, h0_re, h0_im):
    b, d = u.shape
    st_spec = pl.BlockSpec((None, b, SSM_TILE_ST), lambda j: (j, 0, 0))
    a_spec = pl.BlockSpec((None, 1, SSM_TILE_ST), lambda j: (j, 0, 0))
    st_shape = jax.ShapeDtypeStruct((SSM_TILES, b, SSM_TILE_ST), F32)
    return pl.pallas_call(
        _ssm_step_kernel,
        out_shape=(jax.ShapeDtypeStruct((b, d), BF16), st_shape, st_shape),
        grid=(SSM_TILES,),
        in_specs=[
            pl.BlockSpec((b, SSM_TILE_CH), lambda j: (0, j)),
            pl.BlockSpec((None, SSM_TILE_CH, 2 * SSM_TILE_ST), lambda j: (j, 0, 0)),
            pl.BlockSpec((None, 2 * SSM_TILE_ST, SSM_TILE_CH), lambda j: (j, 0, 0)),
            a_spec, a_spec,
            pl.BlockSpec((1, SSM_TILE_CH), lambda j: (0, j)),
            st_spec, st_spec,
        ],
        out_specs=(pl.BlockSpec((b, SSM_TILE_CH), lambda j: (0, j)), st_spec, st_spec),
        compiler_params=_cparams(("parallel",)),
        name="ssm_step",
    )(u, bmat, cmat, a_re.reshape(SSM_TILES, 1, SSM_TILE_ST), a_im.reshape(SSM_TILES, 1, SSM_TILE_ST),
      d_skip.reshape(1, d), h0_re, h0_im)


def _block_diag_mats(bb_re, bb_im, c_re, c_im):
    eye = jnp.eye(SSM_TILE_G, dtype=F32)

    def bmat(bb):
        t = bb.reshape(SSM_TILES, SSM_TILE_G, SSM_CH, SSM_STATE)
        return jnp.einsum('jgcp,gh->jgchp', t, eye).reshape(SSM_TILES, SSM_TILE_CH, SSM_TILE_ST)

    def cmat(cc):
        t = cc.reshape(SSM_TILES, SSM_TILE_G, SSM_CH, SSM_STATE)
        return jnp.einsum('jgcp,gh->jhpgc', t, eye).reshape(SSM_TILES, SSM_TILE_ST, SSM_TILE_CH)

    b_all = jnp.concatenate([bmat(bb_re), bmat(bb_im)], axis=2).astype(BF16)
    c_all = jnp.concatenate([cmat(c_re), cmat(-c_im)], axis=1).astype(BF16)
    return b_all, c_all


def _ff_split(a, axis):
    a = a.reshape(a.shape[:-1] + (2, D_FF))
    a = jnp.pad(a, [(0, 0)] * (a.ndim - 1) + [(0, D_FF_PAD - D_FF)])
    return jnp.moveaxis(a, -2, axis)


def _ff_join(gate, val):
    return jnp.concatenate([gate[..., :D_FF], val[..., :D_FF]], axis=-1)


def kernel(x_prompt, x_sample, cache_kv_g0, cache_kv_g1, cache_kv_g2, state_ssm, state_conv, norm_g, w_qkv,
           w_attn_o, w_ssm_in, lambda_re, lambda_im, log_dt, b_re, b_im, c_re, c_im, d_skip, w_glu, w_up,
           conv_w, conv_b, w_down):
    bp, t, d = x_prompt.shape
    bs = x_sample.shape[0]
    mp = bp * t
    xp = x_prompt.reshape(mp, d)
    xs = x_sample.reshape(bs, d)
    caches = (cache_kv_g0, cache_kv_g1, cache_kv_g2)
    tm_big, tm = 1024, 512

    wq_all = _permute_qkv_cols(w_qkv).astype(BF16)
    wo_all = w_attn_o.astype(BF16)
    win_all = w_ssm_in.astype(BF16)
    wglu_all = w_glu.astype(BF16)
    wu_all = _ff_split(w_up, 1).astype(BF16)
    wd_all = jnp.pad(w_down, ((0, 0), (0, D_FF_PAD - D_FF), (0, 0))).astype(BF16)
    cw_all = _ff_split(conv_w, 1)
    cb_all = _ff_split(conv_b.reshape(DEPTH, 1, 2 * D_FF), 1)
    prev_all = jnp.transpose(_ff_split(state_conv, 3), (0, 2, 3, 1, 4))

    rope_p = rope_tables(t, 0)
    rope_s = rope_tables(SUBLANES, PAST_LEN)
    rope_s = tuple(jnp.broadcast_to(r[0:1], (bs, LANES)) for r in rope_s)

    kv_p = [[] for _ in range(N_DIL)]
    kv_s = [[] for _ in range(N_DIL)]
    ssm_p, ssm_s, conv_p, conv_s = [], [], [], []

    for i in range(DEPTH):
        li = i // N_MIXERS
        if i % N_MIXERS == 0:
            qkv_p = qkv_project(xp, norm_g[i, 0], wq_all, li, rope_p, tm=tm_big)
            qkv_s = qkv_project(xs, norm_g[i, 0], wq_all, li, rope_s, tm=bs)
            op = attn_prompt(qkv_p.reshape(bp, t, QKV_COLS))
            os_ = attn_sample(qkv_s, caches, li)
            xp = matmul_residual(op.reshape(mp, ATTN_W), wo_all, li, xp, norm_g[i, 1], tm=tm, tn=d, name="attn_out")
            xs = matmul_residual(os_, wo_all, li, xs, norm_g[i, 1], tm=bs, tn=d, name="attn_out")
            for g, (win, _) in enumerate(DIL_GROUPS):
                keep = min(win, t)
                c0 = _kv_col(g)
                kv_p[g].append(qkv_p.reshape(bp, t, QKV_COLS)[:, t - keep:, c0:c0 + KV_COLS]
                               .reshape(bp, keep, 2, HEADS, HEAD_DIM))
                kv_s[g].append(qkv_s[:, c0:c0 + KV_COLS].reshape(bs, 1, 2, HEADS, HEAD_DIM))
        else:
            a_re, a_im, bb_re, bb_im = ssm_params(lambda_re[li], lambda_im[li], log_dt[li], b_re[li], b_im[li])
            bmat, cmat = _block_diag_mats(bb_re, bb_im, c_re[li], c_im[li])
            half = SSM_SLABS // 2
            up_ = norm_matmul(xp, norm_g[i, 0], win_all, li, tm=tm_big, tn=1024, name="ssm_in")
            us_ = norm_matmul(xs, norm_g[i, 0], win_all, li, tm=bs, tn=1024, name="ssm_in")
            zeros = jnp.zeros((bp, SSM_TILES, half, LANES), F32)
            zp, sp_re, sp_im = ssm_scan(up_.reshape(bp, t, d), bmat, cmat, a_re, a_im, d_skip[li], zeros, zeros)
            h0 = state_ssm[li].reshape(bs, SSM_TILES, SSM_TILE_ST, 2)
            h0_re = jnp.swapaxes(h0[..., 0], 0, 1)
            h0_im = jnp.swapaxes(h0[..., 1], 0, 1)
            zs, ss_re, ss_im = ssm_step(us_, bmat, cmat, a_re, a_im, d_skip[li], h0_re, h0_im)
            xp = matmul_residual(zp.reshape(mp, d), wglu_all, li, xp, norm_g[i, 1], tm=tm, tn=512, gated=True,
                                 name="ssm_out")
            xs = matmul_residual(zs, wglu_all, li, xs, norm_g[i, 1], tm=bs, tn=512, gated=True, name="ssm_out")
            ssm_p.append(jnp.stack([sp_re.reshape(bp, SSM_GROUPS, SSM_STATE),
                                    sp_im.reshape(bp, SSM_GROUPS, SSM_STATE)], axis=-1))
            ssm_s.append(jnp.stack([jnp.swapaxes(ss_re, 0, 1).reshape(bs, SSM_GROUPS, SSM_STATE),
                                    jnp.swapaxes(ss_im, 0, 1).reshape(bs, SSM_GROUPS, SSM_STATE)], axis=-1))

        act_p, sg_p, sv_p = ffn_up(xp, norm_g[i, 2], wu_all, cw_all, cb_all, i, tm=tm_big, tn=256, seq_len=t)
        act_s, sg_s, sv_s = ffn_up(xs, norm_g[i, 2], wu_all, cw_all, cb_all, i, tm=bs, tn=512, prev_all=prev_all)
        xp = matmul_residual(act_p, wd_all, i, xp, norm_g[i, 3], tm=tm, tn=256, name="ffn_down")
        xs = matmul_residual(act_s, wd_all, i, xs, norm_g[i, 3], tm=bs, tn=512, name="ffn_down")
        tiles = t // tm_big
        tail = _ff_join(sg_p, sv_p).reshape(bp, tiles, SUBLANES, 2 * D_FF)
        conv_p.append(tail[:, tiles - 1, SUBLANES - (CONV_WIDTH - 1):])
        conv_s.append(jnp.concatenate([state_conv[i][:, 1:], _ff_join(sg_s, sv_s)[:, None]], axis=1))

    return (xp.reshape(bp, t, d), xs.reshape(bs, 1, d),
            jnp.stack(kv_p[0]), jnp.stack(kv_s[0]),
            jnp.stack(kv_p[1]), jnp.stack(kv_s[1]),
            jnp.stack(kv_p[2]), jnp.stack(kv_s[2]),
            jnp.stack(ssm_p), jnp.stack(ssm_s),
            jnp.stack(conv_p), jnp.stack(conv_s))
```

```python
import functools
import math

import jax
import jax.numpy as jnp
from jax import lax
from jax.experimental import pallas as pl
from jax.experimental.pallas import tpu as pltpu

F32 = jnp.float32
BF16 = jnp.bfloat16

D_MODEL = 2048
DEPTH = 4
PAST_LEN = 16384
N_MIXERS = 2
HEAD_DIM = 128
HEADS = D_MODEL // (2 * HEAD_DIM)
DIL_GROUPS = ((128, 1), (512, 4), (2048, 16))
N_DIL = len(DIL_GROUPS)
N_BACK = 128
ROT_DIM = HEAD_DIM // 4
ROPE_THETA = 500000.0
NEG_INF = -1e30
SSM_CH = 16
SSM_GROUPS = D_MODEL // SSM_CH
SSM_STATE = 64
D_FF = ((8 * D_MODEL // 3 + 127) // 128) * 128
CONV_WIDTH = 3
RMS_EPS = 1e-6

ATTN_W = HEADS * HEAD_DIM
Q_COLS = N_DIL * ATTN_W
KV_COLS = 2 * ATTN_W
QKV_COLS = 3 * N_DIL * ATTN_W

LANES = 128
SUBLANES = 8
BF16_ROWS = 16
VMEM_LIMIT = 52 * 1024 * 1024
ROW_CHUNK = 256
FFN_ROW_CHUNK = 512

FF_ALIGN = 512
D_FF_PAD = ((D_FF + FF_ALIGN - 1) // FF_ALIGN) * FF_ALIGN

SSM_TILE_G = 16
SSM_TILES = SSM_GROUPS // SSM_TILE_G
SSM_TILE_CH = SSM_TILE_G * SSM_CH
SSM_TILE_ST = SSM_TILE_G * SSM_STATE
SSM_SLABS = 2 * SSM_TILE_ST // LANES
SSM_CHUNK = 256
SSM_PITCH = SSM_CHUNK + SUBLANES


def _cparams(sem):
    return pltpu.CompilerParams(dimension_semantics=sem, vmem_limit_bytes=VMEM_LIMIT)


def _rms(x, g):
    ms = jnp.mean(x * x, axis=-1, keepdims=True)
    return x * lax.rsqrt(ms + RMS_EPS) * g


def _rope_table_kernel(inv_ref, cos_ref, sa_ref, sb_ref, *, pos_base):
    rows = cos_ref.shape[0]
    pos = (lax.broadcasted_iota(jnp.int32, (rows, LANES), 0) + pos_base).astype(F32)
    lane = lax.broadcasted_iota(jnp.int32, (rows, LANES), 1)
    ang = pos * inv_ref[...]
    c = jnp.cos(ang)
    s = jnp.sin(ang)
    half = ROT_DIM // 2
    cos_ref[...] = jnp.where(lane < ROT_DIM, c, 1.0)
    sa_ref[...] = jnp.where(lane < half, -s, 0.0)
    sb_ref[...] = jnp.where((lane >= half) & (lane < ROT_DIM), s, 0.0)


def rope_tables(rows, pos_base):
    half = ROT_DIM // 2
    inv = ROPE_THETA ** (-(jnp.arange(half, dtype=F32) * (2.0 / ROT_DIM)))
    inv_lane = jnp.tile(inv, LANES // half).reshape(1, LANES)
    shp = jax.ShapeDtypeStruct((rows, LANES), F32)
    return pl.pallas_call(
        functools.partial(_rope_table_kernel, pos_base=pos_base),
        out_shape=(shp, shp, shp),
        name="rope_tables",
    )(inv_lane)


QKV_TN = 512
Q_TILES = Q_COLS // QKV_TN
KV_TILES = KV_COLS // QKV_TN
assert KV_TILES & (KV_TILES - 1) == 0


def _qkv_kernel(x_ref, g_ref, w_ref, cos_ref, sa_ref, sb_ref, o_ref, h_ref):
    j = pl.program_id(1)

    @pl.when(j == 0)
    def _():
        h_ref[...] = _rms(x_ref[...], g_ref[...]).astype(BF16)

    is_rope = (j < Q_TILES) | (((j - Q_TILES) & (KV_TILES - 1)) < KV_TILES // 2)
    w = w_ref[...]
    half = ROT_DIM // 2
    tm = x_ref.shape[0]
    rc = min(ROW_CHUNK, tm)
    for c in range(tm // rc):
        rows = slice(c * rc, (c + 1) * rc)
        y = jnp.dot(h_ref[rows, :], w, preferred_element_type=F32)
        cs, sa, sb = cos_ref[rows, :], sa_ref[rows, :], sb_ref[rows, :]
        for h in range(QKV_TN // HEAD_DIM):
            xh = y[:, h * HEAD_DIM:(h + 1) * HEAD_DIM]
            up = pltpu.roll(xh, HEAD_DIM - half, axis=1)
            dn = pltpu.roll(xh, half, axis=1)
            o_ref[rows, h * HEAD_DIM:(h + 1) * HEAD_DIM] = jnp.where(is_rope, xh * cs + up * sa + dn * sb, xh)


def qkv_project(x, g, w_all, layer, rope, *, tm):
    m, d = x.shape
    cos, sa, sb = rope
    tab_tiles = cos.shape[0] // tm
    assert m % tm == 0 and cos.shape[0] % tm == 0 and w_all.shape[2] == QKV_COLS
    tab_spec = pl.BlockSpec((tm, LANES), lambda i, j: (i % tab_tiles, 0))
    return pl.pallas_call(
        _qkv_kernel,
        out_shape=jax.ShapeDtypeStruct((m, QKV_COLS), F32),
        grid=(m // tm, QKV_COLS // QKV_TN),
        in_specs=[
            pl.BlockSpec((tm, d), lambda i, j: (i, 0)),
            pl.BlockSpec((1, d), lambda i, j: (0, 0)),
            pl.BlockSpec((None, d, QKV_TN), lambda i, j: (layer, 0, j)),
            tab_spec, tab_spec, tab_spec,
        ],
        out_specs=pl.BlockSpec((tm, QKV_TN), lambda i, j: (i, j)),
        scratch_shapes=[pltpu.VMEM((tm, d), BF16)],
        compiler_params=_cparams(("parallel", "arbitrary")),
        name="qkv_project",
    )(x, g.reshape(1, d), w_all, cos, sa, sb)


def _permute_qkv_cols(w):
    parts = [w[..., :Q_COLS]]
    for g in range(N_DIL):
        parts.append(w[..., Q_COLS + g * ATTN_W:Q_COLS + (g + 1) * ATTN_W])
        parts.append(w[..., 2 * Q_COLS + g * ATTN_W:2 * Q_COLS + (g + 1) * ATTN_W])
    return jnp.concatenate(parts, axis=-1)


def _kv_col(g):
    return Q_COLS + g * KV_COLS


def _norm_mm_kernel(x_ref, g_ref, w_ref, o_ref, h_ref):
    @pl.when(pl.program_id(1) == 0)
    def _():
        h_ref[...] = _rms(x_ref[...], g_ref[...]).astype(BF16)

    o_ref[...] = jnp.dot(h_ref[...], w_ref[...], preferred_element_type=F32).astype(o_ref.dtype)


def norm_matmul(x, g, w_all, layer, *, tm, tn, out_dtype=F32, name="norm_matmul"):
    m, d = x.shape
    n = w_all.shape[2]
    assert m % tm == 0 and n % tn == 0
    return pl.pallas_call(
        _norm_mm_kernel,
        out_shape=jax.ShapeDtypeStruct((m, n), out_dtype),
        grid=(m // tm, n // tn),
        in_specs=[
            pl.BlockSpec((tm, d), lambda i, j: (i, 0)),
            pl.BlockSpec((1, d), lambda i, j: (0, 0)),
            pl.BlockSpec((None, d, tn), lambda i, j: (layer, 0, j)),
        ],
        out_specs=pl.BlockSpec((tm, tn), lambda i, j: (i, j)),
        scratch_shapes=[pltpu.VMEM((tm, d), BF16)],
        compiler_params=_cparams(("parallel", "arbitrary")),
        name=name,
    )(x, g.reshape(1, d), w_all)


ATTN_BLK = 128
assert ATTN_BLK == N_BACK


def _residue_major(ref, dil):
    t = ref.shape[0]
    if dil == 1:
        return ref[...]
    return jnp.concatenate([ref[pl.ds(r, t // dil, stride=dil), :] for r in range(dil)], axis=0)


def _store_token_order(ref, g, dil, val):
    nb = val.shape[0]
    if dil == 1:
        ref[g] = val.reshape(nb * ATTN_BLK, LANES)
        return
    per = nb // dil
    for r in range(dil):
        ref[g, pl.ds(r, per * ATTN_BLK, stride=dil), :] = val[r * per:(r + 1) * per].reshape(per * ATTN_BLK, LANES)


def _attn_prompt_kernel(q0, q1, q2, k0, k1, k2, v0, v1, v2, o_ref, o_scr, l_scr, d_scr):
    qs, ks, vs = (q0, q1, q2), (k0, k1, k2), (v0, v1, v2)
    t = q0.shape[0]
    nb = t // ATTN_BLK
    blk = ATTN_BLK

    for g, (_, dil) in enumerate(DIL_GROUPS):
        per = nb // dil
        to_blocks = lambda a: a.reshape(nb, blk, HEAD_DIM)
        qb = to_blocks((_residue_major(qs[g], dil) * (HEAD_DIM ** -0.5)).astype(BF16))
        kb = to_blocks(_residue_major(ks[g], dil).astype(BF16))
        vb = to_blocks(_residue_major(vs[g], dil).astype(BF16))
        if per > 1:
            kk = jnp.concatenate([jnp.concatenate([kb[:1], kb[:-1]], axis=0), kb], axis=1)
            vv = jnp.concatenate([jnp.concatenate([vb[:1], vb[:-1]], axis=0), vb], axis=1)
            nk = 2 * blk
            bi = lax.broadcasted_iota(jnp.int32, (nb, 1, nk), 0)
            ci = lax.broadcasted_iota(jnp.int32, (nb, 1, nk), 2)
            no_prev = jnp.where(((bi & (per - 1)) == 0) & (ci < blk), NEG_INF, 0.0)
        else:
            kk, vv, nk = kb, vb, blk
            no_prev = None
        rel = (lax.broadcasted_iota(jnp.int32, (blk, nk), 0) + (nk - blk)
               - lax.broadcasted_iota(jnp.int32, (blk, nk), 1))
        band = jnp.where((rel >= 0) & (rel <= N_BACK), 0.0, NEG_INF)

        s = jnp.einsum('bqd,bkd->bqk', qb, kk, preferred_element_type=F32) + band[None]
        if no_prev is not None:
            s = s + no_prev
        m = jnp.max(s, axis=-1, keepdims=True)
        p = jnp.exp(s - m)
        l = jnp.sum(p, axis=-1, keepdims=True)
        o = jnp.einsum('bqk,bkd->bqd', p.astype(BF16), vv, preferred_element_type=F32)

        @pl.when(pl.program_id(1) == 0)
        def _(g=g, l=l):
            d_scr[g] = jnp.broadcast_to(l, (nb, blk, LANES))

        _store_token_order(o_scr, g, dil, o / d_scr[g])
        _store_token_order(l_scr, g, dil, jnp.broadcast_to(m + jnp.log(l), (nb, blk, LANES)))

    l0, l1, l2 = l_scr[0], l_scr[1], l_scr[2]
    m = jnp.maximum(jnp.maximum(l0, l1), l2)
    e0, e1, e2 = jnp.exp(l0 - m), jnp.exp(l1 - m), jnp.exp(l2 - m)
    num = e0 * o_scr[0] + e1 * o_scr[1] + e2 * o_scr[2]
    o_ref[...] = (num / (e0 + e1 + e2)).astype(o_ref.dtype)


def attn_prompt(qkv):
    b, t, _ = qkv.shape
    assert t % (ATTN_BLK * DIL_GROUPS[-1][1]) == 0

    def spec(col0):
        return pl.BlockSpec((None, t, HEAD_DIM), lambda bi, h: (bi, 0, col0 // HEAD_DIM + h))

    in_specs = ([spec(g * ATTN_W) for g in range(N_DIL)]
                + [spec(_kv_col(g)) for g in range(N_DIL)]
                + [spec(_kv_col(g) + ATTN_W) for g in range(N_DIL)])
    nb = t // ATTN_BLK
    return pl.pallas_call(
        _attn_prompt_kernel,
        out_shape=jax.ShapeDtypeStruct((b, t, ATTN_W), BF16),
        grid=(b, HEADS),
        in_specs=in_specs,
        out_specs=pl.BlockSpec((None, t, HEAD_DIM), lambda bi, h: (bi, 0, h)),
        scratch_shapes=[pltpu.VMEM((N_DIL, t, LANES), F32), pltpu.VMEM((N_DIL, t, LANES), F32),
                        pltpu.VMEM((N_DIL, nb, ATTN_BLK, LANES), F32)],
        compiler_params=_cparams(("parallel", "arbitrary")),
        name="attn_prompt",
    )(*([qkv] * 9))


def _attn_sample_kernel(x_ref, c0, c1, c2, o_ref):
    caches = (c0, c1, c2)
    outs, lses = [], []
    for g in range(N_DIL):
        r0 = _kv_col(g) // HEAD_DIM
        q = x_ref[g * HEADS:(g + 1) * HEADS, :] * (HEAD_DIM ** -0.5)
        k_new = x_ref[r0:r0 + HEADS, :]
        v_new = x_ref[r0 + HEADS:r0 + 2 * HEADS, :]
        kc = caches[g][:, 0:HEADS, :]
        vc = caches[g][:, HEADS:2 * HEADS, :]
        s = jnp.sum(kc * q[None], axis=-1, keepdims=True)
        s_new = jnp.sum(q * k_new, axis=-1, keepdims=True)
        m = jnp.maximum(jnp.max(s, axis=0), s_new)
        p = jnp.exp(s - m[None])
        p_new = jnp.exp(s_new - m)
        l = jnp.sum(p, axis=0) + p_new
        o = jnp.sum(p * vc, axis=0) + p_new * v_new
        outs.append(o / l[0:1])
        lses.append(m + jnp.log(l))
    m = jnp.maximum(jnp.maximum(lses[0], lses[1]), lses[2])
    es = [jnp.exp(x - m) for x in lses]
    num = es[0] * outs[0] + es[1] * outs[1] + es[2] * outs[2]
    o_ref[...] = num / (es[0] + es[1] + es[2])


def attn_sample(qkv, caches, layer):
    b = qkv.shape[0]
    rows = QKV_COLS // HEAD_DIM
    in_specs = [pl.BlockSpec((None, rows, HEAD_DIM), lambda bi: (bi, 0, 0))]
    args = [qkv.reshape(b, rows, HEAD_DIM)]
    for (win, dil), cache in zip(DIL_GROUPS, caches):
        n_layers, _, length = cache.shape[:3]
        assert length == win and length // dil == N_BACK
        args.append(cache.reshape(n_layers, b, N_BACK, dil * 2 * HEADS, HEAD_DIM))
        in_specs.append(pl.BlockSpec((None, None, N_BACK, 2 * HEADS, HEAD_DIM), lambda bi: (layer, bi, 0, 0, 0)))
    out = pl.pallas_call(
        _attn_sample_kernel,
        out_shape=jax.ShapeDtypeStruct((b, HEADS, HEAD_DIM), F32),
        grid=(b,),
        in_specs=in_specs,
        out_specs=pl.BlockSpec((None, HEADS, HEAD_DIM), lambda bi: (bi, 0, 0)),
        compiler_params=_cparams(("parallel",)),
        name="attn_sample",
    )(*args)
    return out.reshape(b, ATTN_W).astype(BF16)


def _mm_res_kernel(a_ref, *refs, nj, gated):
    if gated:
        wv_ref, wg_ref, r_ref, g_ref, o_ref, y_scr = refs
    else:
        wv_ref, r_ref, g_ref, o_ref, y_scr = refs
    j = pl.program_id(1)
    a = a_ref[...]
    y = jnp.dot(a, wv_ref[...], preferred_element_type=F32)
    if gated:
        y = y * jax.nn.sigmoid(jnp.dot(a, wg_ref[...], preferred_element_type=F32))
    if nj == 1:
        o_ref[...] = r_ref[...] + _rms(y, g_ref[...])
        return
    y_scr[j] = y

    @pl.when(j == nj - 1)
    def _():
        full = jnp.concatenate([y_scr[t] for t in range(nj)], axis=-1)
        o_ref[...] = r_ref[...] + _rms(full, g_ref[...])


def matmul_residual(a, w_all, layer, resid, g, *, tm, tn, gated=False, name="matmul_residual"):
    m, k = a.shape
    d = resid.shape[1]
    assert m % tm == 0 and d % tn == 0 and w_all.shape[2] == (2 * d if gated else d)
    nj = d // tn
    w_specs = [pl.BlockSpec((None, k, tn), lambda i, j: (layer, 0, j))]
    if gated:
        w_specs.append(pl.BlockSpec((None, k, tn), lambda i, j: (layer, 0, nj + j)))
    scr_shape = (nj, tm, tn) if nj > 1 else (1, SUBLANES, LANES)
    return pl.pallas_call(
        functools.partial(_mm_res_kernel, nj=nj, gated=gated),
        out_shape=jax.ShapeDtypeStruct((m, d), F32),
        grid=(m // tm, nj),
        in_specs=[pl.BlockSpec((tm, k), lambda i, j: (i, 0))] + w_specs + [
            pl.BlockSpec((tm, d), lambda i, j: (i, 0)),
            pl.BlockSpec((1, d), lambda i, j: (0, 0)),
        ],
        out_specs=pl.BlockSpec((tm, d), lambda i, j: (i, 0)),
        scratch_shapes=[pltpu.VMEM(scr_shape, F32)],
        compiler_params=_cparams(("parallel", "arbitrary")),
        name=name,
    )(a, *([w_all] * len(w_specs)), resid, g.reshape(1, d))


def _conv_gate(ug, ug1, ug2, uv, uv1, uv2, cwg, cwv, cbg, cbv):
    gate = cbg + cwg[0:1] * ug2 + cwg[1:2] * ug1 + cwg[2:3] * ug
    val = cbv + cwv[0:1] * uv2 + cwv[1:2] * uv1 + cwv[2:3] * uv
    return (gate * jax.nn.sigmoid(gate) * val).astype(BF16)


def _ffn_up_seq_kernel(x_ref, xh_ref, g_ref, wg_ref, wv_ref, cwg_ref, cwv_ref, cbg_ref, cbv_ref,
                       act_ref, sg_ref, sv_ref, h_ref, u_scr, *, tiles_per_seq):
    halo = BF16_ROWS
    tm = x_ref.shape[0]

    @pl.when(pl.program_id(1) == 0)
    def _():
        g = g_ref[...]
        keep = jnp.where(pl.program_id(0) % tiles_per_seq == 0, 0.0, 1.0)
        h_ref[0:halo] = (_rms(xh_ref[...], g) * keep).astype(BF16)
        h_ref[halo:] = _rms(x_ref[...], g).astype(BF16)

    wg, wv = wg_ref[...], wv_ref[...]
    cwg, cwv, cbg, cbv = cwg_ref[...], cwv_ref[...], cbg_ref[...], cbv_ref[...]
    rc = min(FFN_ROW_CHUNK, tm)
    for c in range(tm // rc):
        lo = 0 if c == 0 else halo + c * rc
        hi = halo + (c + 1) * rc
        hc = h_ref[lo:hi, :]
        u_scr[0, lo:hi, :] = jnp.dot(hc, wg, preferred_element_type=F32)
        u_scr[1, lo:hi, :] = jnp.dot(hc, wv, preferred_element_type=F32)
        base = halo + c * rc
        taps = [[u_scr[s, pl.ds(base - k, rc), :] for k in range(CONV_WIDTH)] for s in range(2)]
        act_ref[c * rc:(c + 1) * rc, :] = _conv_gate(*taps[0], *taps[1], cwg, cwv, cbg, cbv)
    sg_ref[...] = u_scr[0, halo + tm - SUBLANES:halo + tm, :]
    sv_ref[...] = u_scr[1, halo + tm - SUBLANES:halo + tm, :]


def _ffn_up_step_kernel(x_ref, g_ref, wg_ref, wv_ref, g1_ref, v1_ref, g2_ref, v2_ref,
                        cwg_ref, cwv_ref, cbg_ref, cbv_ref, act_ref, sg_ref, sv_ref, h_ref):
    @pl.when(pl.program_id(1) == 0)
    def _():
        h_ref[...] = _rms(x_ref[...], g_ref[...]).astype(BF16)

    h = h_ref[...]
    ug = jnp.dot(h, wg_ref[...], preferred_element_type=F32)
    uv = jnp.dot(h, wv_ref[...], preferred_element_type=F32)
    sg_ref[...] = ug
    sv_ref[...] = uv
    act_ref[...] = _conv_gate(ug, g1_ref[...], g2_ref[...], uv, v1_ref[...], v2_ref[...],
                              cwg_ref[...], cwv_ref[...], cbg_ref[...], cbv_ref[...])


def ffn_up(x, g, w_all, cw_all, cb_all, layer, *, tm, tn, seq_len=None, prev_all=None):
    m, d = x.shape
    n = w_all.shape[3]
    assert m % tm == 0 and n % tn == 0
    gv = lambda rows: [pl.BlockSpec((None, None, rows, tn), lambda i, j, s=s: (layer, s, 0, j)) for s in range(2)]
    w_specs = gv(d)
    c_specs = gv(CONV_WIDTH) + gv(1)
    c_args = [cw_all, cw_all, cb_all, cb_all]
    x_spec = pl.BlockSpec((tm, d), lambda i, j: (i, 0))
    g_spec = pl.BlockSpec((1, d), lambda i, j: (0, 0))
    tile = pl.BlockSpec((tm, tn), lambda i, j: (i, j))
    if prev_all is None:
        assert seq_len % tm == 0 and tm % BF16_ROWS == 0
        hb = tm // BF16_ROWS
        halo_spec = pl.BlockSpec((BF16_ROWS, d), lambda i, j: (jnp.maximum(i * hb - 1, 0), 0))
        body = functools.partial(_ffn_up_seq_kernel, tiles_per_seq=seq_len // tm)
        in_specs = [x_spec, halo_spec, g_spec] + w_specs + c_specs
        args = [x, x, g.reshape(1, d), w_all, w_all] + c_args
        st_rows, st_spec = (m // tm) * SUBLANES, pl.BlockSpec((SUBLANES, tn), lambda i, j: (i, j))
        scratch = [pltpu.VMEM((tm + BF16_ROWS, d), BF16), pltpu.VMEM((2, tm + BF16_ROWS, tn), F32)]
    else:
        body = _ffn_up_step_kernel
        prev_specs = [pl.BlockSpec((None, None, None, tm, tn), lambda i, j, r=r, s=s: (layer, r, s, i, j))
                      for r in (1, 0) for s in range(2)]
        in_specs = [x_spec, g_spec] + w_specs + prev_specs + c_specs
        args = [x, g.reshape(1, d), w_all, w_all] + [prev_all] * 4 + c_args
        st_rows, st_spec = m, tile
        scratch = [pltpu.VMEM((tm, d), BF16)]
    st_shape = jax.ShapeDtypeStruct((st_rows, n), F32)
    return pl.pallas_call(
        body,
        out_shape=(jax.ShapeDtypeStruct((m, n), BF16), st_shape, st_shape),
        grid=(m // tm, n // tn),
        in_specs=in_specs,
        out_specs=(tile, st_spec, st_spec),
        scratch_shapes=scratch,
        compiler_params=_cparams(("parallel", "arbitrary")),
        name="ffn_up",
    )(*args)


def _ssm_params_kernel(lre_ref, lim_ref, ldt_ref, bre_ref, bim_ref, are_ref, aim_ref, bbre_ref, bbim_ref):
    lre = jnp.minimum(lre_ref[...], -1e-4)
    lim = lim_ref[...]
    dt = jnp.exp(ldt_ref[...])
    mag = jnp.exp(lre * dt)
    a_re = mag * jnp.cos(lim * dt)
    a_im = mag * jnp.sin(lim * dt)
    are_ref[...] = a_re
    aim_ref[...] = a_im
    x, y = a_re - 1.0, a_im
    den = lre * lre + lim * lim
    c_re = (x * lre + y * lim) / den
    c_im = (y * lre - x * lim) / den
    bre, bim = bre_ref[...], bim_ref[...]
    bbre_ref[...] = c_re * bre - c_im * bim
    bbim_ref[...] = c_re * bim + c_im * bre


def ssm_params(lam_re, lam_im, log_dt, b_re, b_im):
    g, p = lam_re.shape
    shp_a = jax.ShapeDtypeStruct((g, 1, p), F32)
    shp_b = jax.ShapeDtypeStruct((g, SSM_CH, p), F32)
    a_re, a_im, bb_re, bb_im = pl.pallas_call(
        _ssm_params_kernel,
        out_shape=(shp_a, shp_a, shp_b, shp_b),
        compiler_params=pltpu.CompilerParams(vmem_limit_bytes=VMEM_LIMIT),
        name="ssm_params",
    )(lam_re.reshape(g, 1, p), lam_im.reshape(g, 1, p), log_dt.reshape(g, 1, 1),
      jnp.swapaxes(b_re, 1, 2), jnp.swapaxes(b_im, 1, 2))
    return a_re.reshape(g, p), a_im.reshape(g, p), bb_re, bb_im


def _gelu_tanh(y):
    return 0.5 * y * (1.0 + jnp.tanh(math.sqrt(2.0 / math.pi) * (y + 0.044715 * (y * y * y))))


def _ssm_scan_kernel(u_ref, bm_ref, cm_ref, are_ref, aim_ref, d_ref, h0r_ref, h0i_ref,
                     z_ref, sr_ref, si_ref, bu_scr, xr_scr, xi_scr, *, n_chunks):
    c = pl.program_id(1)
    nb, tc = u_ref.shape[0], u_ref.shape[1]
    pitch = SSM_PITCH
    half = SSM_SLABS // 2

    @pl.when(c == 0)
    def _():
        xr_scr[...] = h0r_ref[...]
        xi_scr[...] = h0i_ref[...]

    bm = bm_ref[...]
    for b in range(nb):
        r = jnp.dot(u_ref[b].astype(BF16), bm, preferred_element_type=F32)
        for k in range(SSM_SLABS):
            bu_scr[b, k * pitch:k * pitch + tc, :] = r[:, k * LANES:(k + 1) * LANES]

    ar = are_ref[...]
    ai = aim_ref[...]

    def step(t, carry):
        new = []
        for b in range(nb):
            xr, xi = carry[b]
            re_rows = pl.ds(t, half, stride=pitch)
            im_rows = pl.ds(half * pitch + t, half, stride=pitch)
            nr = ar * xr - ai * xi + bu_scr[b, re_rows, :]
            ni = ar * xi + ai * xr + bu_scr[b, im_rows, :]
            bu_scr[b, re_rows, :] = nr
            bu_scr[b, im_rows, :] = ni
            new.append((nr, ni))
        return tuple(new)

    init = tuple((xr_scr[b], xi_scr[b]) for b in range(nb))
    fin = lax.fori_loop(0, tc, step, init, unroll=8)
    for b in range(nb):
        xr_scr[b] = fin[b][0]
        xi_scr[b] = fin[b][1]

    cm = cm_ref[...]
    for b in range(nb):
        xs = jnp.concatenate([bu_scr[b, k * pitch:k * pitch + tc, :] for k in range(SSM_SLABS)], axis=-1)
        y = jnp.dot(xs.astype(BF16), cm, preferred_element_type=F32) + d_ref[...] * u_ref[b]
        z_ref[b] = _gelu_tanh(y).astype(z_ref.dtype)

    @pl.when(c == n_chunks - 1)
    def _():
        sr_ref[...] = xr_scr[...]
        si_ref[...] = xi_scr[...]


def ssm_scan(u, bmat, cmat, a_re, a_im, d_skip, h0_re, h0_im):
    b, t, d = u.shape
    tc = SSM_CHUNK
    n_chunks = t // tc
    half = SSM_SLABS // 2
    st_spec = pl.BlockSpec((b, None, half, LANES), lambda j, c: (0, j, 0, 0))
    st_shape = jax.ShapeDtypeStruct((b, SSM_TILES, half, LANES), F32)
    return pl.pallas_call(
        functools.partial(_ssm_scan_kernel, n_chunks=n_chunks),
        out_shape=(jax.ShapeDtypeStruct((b, t, d), BF16), st_shape, st_shape),
        grid=(SSM_TILES, n_chunks),
        in_specs=[
            pl.BlockSpec((b, tc, SSM_TILE_CH), lambda j, c: (0, c, j)),
            pl.BlockSpec((None, SSM_TILE_CH, 2 * SSM_TILE_ST), lambda j, c: (j, 0, 0)),
            pl.BlockSpec((None, 2 * SSM_TILE_ST, SSM_TILE_CH), lambda j, c: (j, 0, 0)),
            pl.BlockSpec((None, half, LANES), lambda j, c: (j, 0, 0)),
            pl.BlockSpec((None, half, LANES), lambda j, c: (j, 0, 0)),
            pl.BlockSpec((1, SSM_TILE_CH), lambda j, c: (0, j)),
            st_spec, st_spec,
        ],
        out_specs=(pl.BlockSpec((b, tc, SSM_TILE_CH), lambda j, c: (0, c, j)), st_spec, st_spec),
        scratch_shapes=[
            pltpu.VMEM((b, SSM_SLABS * SSM_PITCH, LANES), F32),
            pltpu.VMEM((b, half, LANES), F32),
            pltpu.VMEM((b, half, LANES), F32),
        ],
        compiler_params=_cparams(("parallel", "arbitrary")),
        name="ssm_scan",
    )(u, bmat, cmat, a_re.reshape(SSM_TILES, half, LANES), a_im.reshape(SSM_TILES, half, LANES),
      d_skip.reshape(1, d), h0_re, h0_im)


def _ssm_step_kernel(u_ref, bm_ref, cm_ref, are_ref, aim_ref, d_ref, h0r_ref, h0i_ref, z_ref, sr_ref, si_ref):
    u = u_ref[...]
    bu = jnp.dot(u.astype(BF16), bm_ref[...], preferred_element_type=F32)
    ar, ai = are_ref[...], aim_ref[...]
    hr, hi = h0r_ref[...], h0i_ref[...]
    xr = ar * hr - ai * hi + bu[:, :SSM_TILE_ST]
    xi = ar * hi + ai * hr + bu[:, SSM_TILE_ST:]
    sr_ref[...] = xr
    si_ref[...] = xi
    xs = jnp.concatenate([xr, xi], axis=-1).astype(BF16)
    y = jnp.dot(xs, cm_ref[...], preferred_element_type=F32) + d_ref[...] * u
    z_ref[...] = _gelu_tanh(y).astype(z_ref.dtype)


def ssm_step(u, bmat, cmat, a_re, a_im, d_skip, h0_re, h0_im):
    b, d = u.shape
    st_spec = pl.BlockSpec((None, b, SSM_TILE_ST), lambda j: (j, 0, 0))
    a_spec = pl.BlockSpec((None, 1, SSM_TILE_ST), lambda j: (j, 0, 0))
    st_shape = jax.ShapeDtypeStruct((SSM_TILES, b, SSM_TILE_ST), F32)
    return pl.pallas_call(
        _ssm_step_kernel,
        out_shape=(jax.ShapeDtypeStruct((b, d), BF16), st_shape, st_shape),
        grid=(SSM_TILES,),
        in_specs=[
            pl.BlockSpec((b, SSM_TILE_CH), lambda j: (0, j)),
            pl.BlockSpec((None, SSM_TILE_CH, 2 * SSM_TILE_ST), lambda j: (j, 0, 0)),
            pl.BlockSpec((None, 2 * SSM_TILE_ST, SSM_TILE_CH), lambda j: (j, 0, 0)),
            a_spec, a_spec,
            pl.BlockSpec((1, SSM_TILE_CH), lambda j: (0, j)),
            st_spec, st_spec,
        ],
        out_specs=(pl.BlockSpec((b, SSM_TILE_CH), lambda j: (0, j)), st_spec, st_spec),
        compiler_params=_cparams(("parallel",)),
        name="ssm_step",
    )(u, bmat, cmat, a_re.reshape(SSM_TILES, 1, SSM_TILE_ST), a_im.reshape(SSM_TILES, 1, SSM_TILE_ST),
      d_skip.reshape(1, d), h0_re, h0_im)


def _block_diag_mats(bb_re, bb_im, c_re, c_im):
    eye = jnp.eye(SSM_TILE_G, dtype=F32)

    def bmat(bb):
        t = bb.reshape(SSM_TILES, SSM_TILE_G, SSM_CH, SSM_STATE)
        return jnp.einsum('jgcp,gh->jgchp', t, eye).reshape(SSM_TILES, SSM_TILE_CH, SSM_TILE_ST)

    def cmat(cc):
        t = cc.reshape(SSM_TILES, SSM_TILE_G, SSM_CH, SSM_STATE)
        return jnp.einsum('jgcp,gh->jhpgc', t, eye).reshape(SSM_TILES, SSM_TILE_ST, SSM_TILE_CH)

    b_all = jnp.concatenate([bmat(bb_re), bmat(bb_im)], axis=2).astype(BF16)
    c_all = jnp.concatenate([cmat(c_re), cmat(-c_im)], axis=1).astype(BF16)
    return b_all, c_all


def _cast_pad_kernel(x_ref, o_ref):
    r, c = x_ref.shape
    rr, cc = o_ref.shape
    o_ref[0:r, 0:c] = x_ref[...].astype(o_ref.dtype)
    if cc > c:
        o_ref[:, c:] = jnp.zeros((rr, cc - c), o_ref.dtype)
    if rr > r:
        o_ref[r:, 0:c] = jnp.zeros((rr - r, c), o_ref.dtype)


def _cast_call(x, out_shape, grid, in_spec, out_spec, name):
    return pl.pallas_call(
        _cast_pad_kernel,
        out_shape=jax.ShapeDtypeStruct(out_shape, BF16),
        grid=grid,
        in_specs=[in_spec],
        out_specs=out_spec,
        compiler_params=_cparams(("parallel",) * len(grid)),
        name=name,
    )(x)


def prep_w_qkv(w):
    n_layers, d, n = w.shape
    tr = 512

    def src_block(ob):
        kv = ob - N_DIL
        return jnp.where(ob < N_DIL, ob, N_DIL + (kv % 2) * N_DIL + kv // 2)

    return _cast_call(
        w, (n_layers, d, n), (n_layers, d // tr, n // ATTN_W),
        pl.BlockSpec((None, tr, ATTN_W), lambda l, r, ob: (l, r, src_block(ob))),
        pl.BlockSpec((None, tr, ATTN_W), lambda l, r, ob: (l, r, ob)), "prep_w_qkv")


def prep_w_up(w):
    n_layers, d, _ = w.shape
    tr = 256
    return _cast_call(
        w, (n_layers, 2, d, D_FF_PAD), (n_layers, 2, d // tr),
        pl.BlockSpec((None, tr, D_FF), lambda l, s, r: (l, r, s)),
        pl.BlockSpec((None, None, tr, D_FF_PAD), lambda l, s, r: (l, s, r, 0)), "prep_w_up")


def prep_w_down(w):
    n_layers, _, d = w.shape
    tc = 256
    return _cast_call(
        w, (n_layers, D_FF_PAD, d), (n_layers, d // tc),
        pl.BlockSpec((None, D_FF, tc), lambda l, c: (l, 0, c)),
        pl.BlockSpec((None, D_FF_PAD, tc), lambda l, c: (l, 0, c)), "prep_w_down")


def _ff_split(a, axis):
    a = a.reshape(a.shape[:-1] + (2, D_FF))
    a = jnp.pad(a, [(0, 0)] * (a.ndim - 1) + [(0, D_FF_PAD - D_FF)])
    return jnp.moveaxis(a, -2, axis)


def _ff_join(gate, val):
    return jnp.concatenate([gate[..., :D_FF], val[..., :D_FF]], axis=-1)


def kernel(x_prompt, x_sample, cache_kv_g0, cache_kv_g1, cache_kv_g2, state_ssm, state_conv, norm_g, w_qkv,
           w_attn_o, w_ssm_in, lambda_re, lambda_im, log_dt, b_re, b_im, c_re, c_im, d_skip, w_glu, w_up,
           conv_w, conv_b, w_down):
    bp, t, d = x_prompt.shape
    bs = x_sample.shape[0]
    mp = bp * t
    xp = x_prompt.reshape(mp, d)
    xs = x_sample.reshape(bs, d)
    caches = (cache_kv_g0, cache_kv_g1, cache_kv_g2)
    tm_big, tm = 1024, 512

    wq_all = prep_w_qkv(w_qkv)
    wo_all = w_attn_o.astype(BF16)
    win_all = w_ssm_in.astype(BF16)
    wglu_all = w_glu.astype(BF16)
    wu_all = prep_w_up(w_up)
    wd_all = prep_w_down(w_down)
    cw_all = _ff_split(conv_w, 1)
    cb_all = _ff_split(conv_b.reshape(DEPTH, 1, 2 * D_FF), 1)
    prev_all = jnp.transpose(_ff_split(state_conv, 3), (0, 2, 3, 1, 4))

    rope_p = rope_tables(t, 0)
    rope_s = rope_tables(SUBLANES, PAST_LEN)
    rope_s = tuple(jnp.broadcast_to(r[0:1], (bs, LANES)) for r in rope_s)

    kv_p = [[] for _ in range(N_DIL)]
    kv_s = [[] for _ in range(N_DIL)]
    ssm_p, ssm_s, conv_p, conv_s = [], [], [], []

    for i in range(DEPTH):
        li = i // N_MIXERS
        if i % N_MIXERS == 0:
            qkv_p = qkv_project(xp, norm_g[i, 0], wq_all, li, rope_p, tm=tm_big)
            qkv_s = qkv_project(xs, norm_g[i, 0], wq_all, li, rope_s, tm=bs)
            op = attn_prompt(qkv_p.reshape(bp, t, QKV_COLS))
            os_ = attn_sample(qkv_s, caches, li)
            xp = matmul_residual(op.reshape(mp, ATTN_W), wo_all, li, xp, norm_g[i, 1], tm=tm, tn=d, name="attn_out")
            xs = matmul_residual(os_, wo_all, li, xs, norm_g[i, 1], tm=bs, tn=d, name="attn_out")
            for g, (win, _) in enumerate(DIL_GROUPS):
                keep = min(win, t)
                c0 = _kv_col(g)
                kv_p[g].append(qkv_p.reshape(bp, t, QKV_COLS)[:, t - keep:, c0:c0 + KV_COLS]
                               .reshape(bp, keep, 2, HEADS, HEAD_DIM))
                kv_s[g].append(qkv_s[:, c0:c0 + KV_COLS].reshape(bs, 1, 2, HEADS, HEAD_DIM))
        else:
            a_re, a_im, bb_re, bb_im = ssm_params(lambda_re[li], lambda_im[li], log_dt[li], b_re[li], b_im[li])
            bmat, cmat = _block_diag_mats(bb_re, bb_im, c_re[li], c_im[li])
            half = SSM_SLABS // 2
            up_ = norm_matmul(xp, norm_g[i, 0], win_all, li, tm=tm_big, tn=1024, name="ssm_in")
            us_ = norm_matmul(xs, norm_g[i, 0], win_all, li, tm=bs, tn=1024, name="ssm_in")
            zeros = jnp.zeros((bp, SSM_TILES, half, LANES), F32)
            zp, sp_re, sp_im = ssm_scan(up_.reshape(bp, t, d), bmat, cmat, a_re, a_im, d_skip[li], zeros, zeros)
            h0 = state_ssm[li].reshape(bs, SSM_TILES, SSM_TILE_ST, 2)
            h0_re = jnp.swapaxes(h0[..., 0], 0, 1)
            h0_im = jnp.swapaxes(h0[..., 1], 0, 1)
            zs, ss_re, ss_im = ssm_step(us_, bmat, cmat, a_re, a_im, d_skip[li], h0_re, h0_im)
            xp = matmul_residual(zp.reshape(mp, d), wglu_all, li, xp, norm_g[i, 1], tm=tm, tn=512, gated=True,
                                 name="ssm_out")
            xs = matmul_residual(zs, wglu_all, li, xs, norm_g[i, 1], tm=bs, tn=512, gated=True, name="ssm_out")
            ssm_p.append(jnp.stack([sp_re.reshape(bp, SSM_GROUPS, SSM_STATE),
                                    sp_im.reshape(bp, SSM_GROUPS, SSM_STATE)], axis=-1))
            ssm_s.append(jnp.stack([jnp.swapaxes(ss_re, 0, 1).reshape(bs, SSM_GROUPS, SSM_STATE),
                                    jnp.swapaxes(ss_im, 0, 1).reshape(bs, SSM_GROUPS, SSM_STATE)], axis=-1))

        act_p, sg_p, sv_p = ffn_up(xp, norm_g[i, 2], wu_all, cw_all, cb_all, i, tm=tm_big, tn=256, seq_len=t)
        act_s, sg_s, sv_s = ffn_up(xs, norm_g[i, 2], wu_all, cw_all, cb_all, i, tm=bs, tn=512, prev_all=prev_all)
        xp = matmul_residual(act_p, wd_all, i, xp, norm_g[i, 3], tm=tm, tn=256, name="ffn_down")
        xs = matmul_residual(act_s, wd_all, i, xs, norm_g[i, 3], tm=bs, tn=512, name="ffn_down")
        tiles = t // tm_big
        tail = _ff_join(sg_p, sv_p).reshape(bp, tiles, SUBLANES, 2 * D_FF)
        conv_p.append(tail[:, tiles - 1, SUBLANES - (CONV_WIDTH - 1):])
        conv_s.append(jnp.concatenate([state_conv[i][:, 1:], _ff_join(sg_s, sv_s)[:, None]], axis=1))

    return (xp.reshape(bp, t, d), xs.reshape(bs, 1, d),
            jnp.stack(kv_p[0]), jnp.stack(kv_s[0]),
            jnp.stack(kv_p[1]), jnp.stack(kv_s[1]),
            jnp.stack(kv_p[2]), jnp.stack(kv_s[2]),
            jnp.stack(ssm_p), jnp.stack(ssm_s),
            jnp.stack(conv_p), jnp.stack(conv_s))
```

```python
import functools
import math

import jax
import jax.numpy as jnp
from jax import lax
from jax.experimental import pallas as pl
from jax.experimental.pallas import tpu as pltpu

F32 = jnp.float32
BF16 = jnp.bfloat16

D_MODEL = 2048
DEPTH = 4
PAST_LEN = 16384
N_MIXERS = 2
HEAD_DIM = 128
HEADS = D_MODEL // (2 * HEAD_DIM)
DIL_GROUPS = ((128, 1), (512, 4), (2048, 16))
N_DIL = len(DIL_GROUPS)
N_BACK = 128
ROT_DIM = HEAD_DIM // 4
ROPE_THETA = 500000.0
NEG_INF = -1e30
SSM_CH = 16
SSM_GROUPS = D_MODEL // SSM_CH
SSM_STATE = 64
D_FF = ((8 * D_MODEL // 3 + 127) // 128) * 128
CONV_WIDTH = 3
RMS_EPS = 1e-6

ATTN_W = HEADS * HEAD_DIM
Q_COLS = N_DIL * ATTN_W
KV_COLS = 2 * ATTN_W
QKV_COLS = 3 * N_DIL * ATTN_W

LANES = 128
SUBLANES = 8
BF16_ROWS = 16
MXU_COLS = 256
VMEM_LIMIT = 52 * 1024 * 1024
ROW_CHUNK = 256
FFN_ROW_CHUNK = 512

FF_ALIGN = 512
D_FF_PAD = ((D_FF + FF_ALIGN - 1) // FF_ALIGN) * FF_ALIGN

SSM_TILE_G = 16
SSM_TILES = SSM_GROUPS // SSM_TILE_G
SSM_TILE_CH = SSM_TILE_G * SSM_CH
SSM_TILE_ST = SSM_TILE_G * SSM_STATE
SSM_SLABS = 2 * SSM_TILE_ST // LANES
SSM_CHUNK = 256
SSM_PITCH = SSM_CHUNK + SUBLANES


def _cparams(sem):
    return pltpu.CompilerParams(dimension_semantics=sem, vmem_limit_bytes=VMEM_LIMIT)


def _rms(x, g):
    ms = jnp.mean(x * x, axis=-1, keepdims=True)
    return x * lax.rsqrt(ms + RMS_EPS) * g


def _rope_table_kernel(inv_ref, cos_ref, sa_ref, sb_ref, *, pos_base):
    rows = cos_ref.shape[0]
    pos = (lax.broadcasted_iota(jnp.int32, (rows, LANES), 0) + pos_base).astype(F32)
    lane = lax.broadcasted_iota(jnp.int32, (rows, LANES), 1)
    ang = pos * inv_ref[...]
    c = jnp.cos(ang)
    s = jnp.sin(ang)
    half = ROT_DIM // 2
    cos_ref[...] = jnp.where(lane < ROT_DIM, c, 1.0)
    sa_ref[...] = jnp.where(lane < half, -s, 0.0)
    sb_ref[...] = jnp.where((lane >= half) & (lane < ROT_DIM), s, 0.0)


def rope_tables(rows, pos_base):
    half = ROT_DIM // 2
    inv = ROPE_THETA ** (-(jnp.arange(half, dtype=F32) * (2.0 / ROT_DIM)))
    inv_lane = jnp.tile(inv, LANES // half).reshape(1, LANES)
    shp = jax.ShapeDtypeStruct((rows, LANES), F32)
    return pl.pallas_call(
        functools.partial(_rope_table_kernel, pos_base=pos_base),
        out_shape=(shp, shp, shp),
        name="rope_tables",
    )(inv_lane)


QKV_TN = 512
Q_TILES = Q_COLS // QKV_TN
KV_TILES = KV_COLS // QKV_TN
assert KV_TILES & (KV_TILES - 1) == 0


def _qkv_kernel(x_ref, g_ref, w_ref, cos_ref, sa_ref, sb_ref, *refs, stacked):
    if stacked:
        o_ref, s_ref, h_ref = refs[-3:]
    else:
        (o_ref, h_ref), s_ref = refs, None
    j = pl.program_id(1)

    @pl.when(j == 0)
    def _():
        h_ref[...] = _rms(x_ref[...], g_ref[...]).astype(BF16)

    is_rope = (j < Q_TILES) | (((j - Q_TILES) & (KV_TILES - 1)) < KV_TILES // 2)
    w = w_ref[...]
    half = ROT_DIM // 2
    tm = x_ref.shape[0]
    rc = min(ROW_CHUNK, tm)
    for c in range(tm // rc):
        rows = slice(c * rc, (c + 1) * rc)
        y = jnp.dot(h_ref[rows, :], w, preferred_element_type=F32)
        cs, sa, sb = cos_ref[rows, :], sa_ref[rows, :], sb_ref[rows, :]
        for h in range(QKV_TN // HEAD_DIM):
            xh = y[:, h * HEAD_DIM:(h + 1) * HEAD_DIM]
            up = pltpu.roll(xh, HEAD_DIM - half, axis=1)
            dn = pltpu.roll(xh, half, axis=1)
            out = jnp.where(is_rope, xh * cs + up * sa + dn * sb, xh)
            o_ref[rows, h * HEAD_DIM:(h + 1) * HEAD_DIM] = out
            if s_ref is not None:
                s_ref[rows, h * HEAD_DIM:(h + 1) * HEAD_DIM] = out


def qkv_project(x, g, w_all, layer, rope, *, tm, stack=None):
    m, d = x.shape
    cos, sa, sb = rope
    tab_tiles = cos.shape[0] // tm
    assert m % tm == 0 and cos.shape[0] % tm == 0 and w_all.shape[2] == QKV_COLS
    tab_spec = pl.BlockSpec((tm, LANES), lambda i, j: (i % tab_tiles, 0))
    in_specs = [
        pl.BlockSpec((tm, d), lambda i, j: (i, 0)),
        pl.BlockSpec((1, d), lambda i, j: (0, 0)),
        pl.BlockSpec((None, d, QKV_TN), lambda i, j: (layer, 0, j)),
        tab_spec, tab_spec, tab_spec,
    ]
    args = [x, g.reshape(1, d), w_all, cos, sa, sb]
    out_shape = jax.ShapeDtypeStruct((m, QKV_COLS), F32)
    out_specs = pl.BlockSpec((tm, QKV_TN), lambda i, j: (i, j))
    aliases = {}
    if stack is not None:
        n_layers, earlier = stack
        base = QKV_COLS // QKV_TN - KV_TILES
        out_shape = (out_shape, jax.ShapeDtypeStruct((n_layers, m, KV_COLS), F32))
        out_specs = (out_specs, pl.BlockSpec((None, tm, QKV_TN), lambda i, j: (layer, i, jnp.maximum(j - base, 0))))
        if earlier is not None:
            aliases = {len(args): 1}
            in_specs.append(pl.BlockSpec(memory_space=pl.ANY))
            args.append(earlier)
    return pl.pallas_call(
        functools.partial(_qkv_kernel, stacked=stack is not None),
        out_shape=out_shape,
        grid=(m // tm, QKV_COLS // QKV_TN),
        in_specs=in_specs,
        out_specs=out_specs,
        scratch_shapes=[pltpu.VMEM((tm, d), BF16)],
        input_output_aliases=aliases,
        compiler_params=_cparams(("parallel", "arbitrary")),
        name="qkv_project",
    )(*args)


def _permute_qkv_cols(w):
    parts = [w[..., :Q_COLS]]
    for g in range(N_DIL):
        parts.append(w[..., Q_COLS + g * ATTN_W:Q_COLS + (g + 1) * ATTN_W])
        parts.append(w[..., 2 * Q_COLS + g * ATTN_W:2 * Q_COLS + (g + 1) * ATTN_W])
    return jnp.concatenate(parts, axis=-1)


def _kv_col(g):
    return Q_COLS + g * KV_COLS


def _norm_mm_kernel(x_ref, g_ref, w_ref, o_ref, h_ref):
    @pl.when(pl.program_id(1) == 0)
    def _():
        h_ref[...] = _rms(x_ref[...], g_ref[...]).astype(BF16)

    o_ref[...] = jnp.dot(h_ref[...], w_ref[...], preferred_element_type=F32).astype(o_ref.dtype)


def norm_matmul(x, g, w_all, layer, *, tm, tn, out_dtype=F32, name="norm_matmul"):
    m, d = x.shape
    n = w_all.shape[2]
    assert m % tm == 0 and n % tn == 0
    return pl.pallas_call(
        _norm_mm_kernel,
        out_shape=jax.ShapeDtypeStruct((m, n), out_dtype),
        grid=(m // tm, n // tn),
        in_specs=[
            pl.BlockSpec((tm, d), lambda i, j: (i, 0)),
            pl.BlockSpec((1, d), lambda i, j: (0, 0)),
            pl.BlockSpec((None, d, tn), lambda i, j: (layer, 0, j)),
        ],
        out_specs=pl.BlockSpec((tm, tn), lambda i, j: (i, j)),
        scratch_shapes=[pltpu.VMEM((tm, d), BF16)],
        compiler_params=_cparams(("parallel", "arbitrary")),
        name=name,
    )(x, g.reshape(1, d), w_all)


ATTN_BLK = 128
assert ATTN_BLK == N_BACK


def _residue_major(ref, dil):
    t = ref.shape[0]
    if dil == 1:
        return ref[...]
    return jnp.concatenate([ref[pl.ds(r, t // dil, stride=dil), :] for r in range(dil)], axis=0)


def _store_token_order(ref, g, dil, val):
    nb = val.shape[0]
    if dil == 1:
        ref[g] = val.reshape(nb * ATTN_BLK, LANES)
        return
    per = nb // dil
    for r in range(dil):
        ref[g, pl.ds(r, per * ATTN_BLK, stride=dil), :] = val[r * per:(r + 1) * per].reshape(per * ATTN_BLK, LANES)


def _attn_prompt_kernel(q0, q1, q2, k0, k1, k2, v0, v1, v2, o_ref, o_scr, l_scr, d_scr):
    qs, ks, vs = (q0, q1, q2), (k0, k1, k2), (v0, v1, v2)
    t = q0.shape[0]
    nb = t // ATTN_BLK
    blk = ATTN_BLK

    for g, (_, dil) in enumerate(DIL_GROUPS):
        per = nb // dil
        to_blocks = lambda a: a.reshape(nb, blk, HEAD_DIM)
        qb = to_blocks((_residue_major(qs[g], dil) * (HEAD_DIM ** -0.5)).astype(BF16))
        kb = to_blocks(_residue_major(ks[g], dil).astype(BF16))
        vb = to_blocks(_residue_major(vs[g], dil).astype(BF16))
        if per > 1:
            kk = jnp.concatenate([jnp.concatenate([kb[:1], kb[:-1]], axis=0), kb], axis=1)
            vv = jnp.concatenate([jnp.concatenate([vb[:1], vb[:-1]], axis=0), vb], axis=1)
            nk = 2 * blk
            bi = lax.broadcasted_iota(jnp.int32, (nb, 1, nk), 0)
            ci = lax.broadcasted_iota(jnp.int32, (nb, 1, nk), 2)
            no_prev = jnp.where(((bi & (per - 1)) == 0) & (ci < blk), NEG_INF, 0.0)
        else:
            kk, vv, nk = kb, vb, blk
            no_prev = None
        rel = (lax.broadcasted_iota(jnp.int32, (blk, nk), 0) + (nk - blk)
               - lax.broadcasted_iota(jnp.int32, (blk, nk), 1))
        band = jnp.where((rel >= 0) & (rel <= N_BACK), 0.0, NEG_INF)

        s = jnp.einsum('bqd,bkd->bqk', qb, kk, preferred_element_type=F32) + band[None]
        if no_prev is not None:
            s = s + no_prev
        m = jnp.max(s, axis=-1, keepdims=True)
        p = jnp.exp(s - m)
        l = jnp.sum(p, axis=-1, keepdims=True)
        o = jnp.einsum('bqk,bkd->bqd', p.astype(BF16), vv, preferred_element_type=F32)

        @pl.when(pl.program_id(1) == 0)
        def _(g=g, l=l):
            d_scr[g] = jnp.broadcast_to(l, (nb, blk, LANES))

        _store_token_order(o_scr, g, dil, o / d_scr[g])
        _store_token_order(l_scr, g, dil, jnp.broadcast_to(m + jnp.log(l), (nb, blk, LANES)))

    l0, l1, l2 = l_scr[0], l_scr[1], l_scr[2]
    m = jnp.maximum(jnp.maximum(l0, l1), l2)
    e0, e1, e2 = jnp.exp(l0 - m), jnp.exp(l1 - m), jnp.exp(l2 - m)
    num = e0 * o_scr[0] + e1 * o_scr[1] + e2 * o_scr[2]
    o_ref[...] = (num / (e0 + e1 + e2)).astype(o_ref.dtype)


def attn_prompt(qkv):
    b, t, _ = qkv.shape
    assert t % (ATTN_BLK * DIL_GROUPS[-1][1]) == 0

    def spec(col0):
        return pl.BlockSpec((None, t, HEAD_DIM), lambda bi, h: (bi, 0, col0 // HEAD_DIM + h))

    in_specs = ([spec(g * ATTN_W) for g in range(N_DIL)]
                + [spec(_kv_col(g)) for g in range(N_DIL)]
                + [spec(_kv_col(g) + ATTN_W) for g in range(N_DIL)])
    nb = t // ATTN_BLK
    return pl.pallas_call(
        _attn_prompt_kernel,
        out_shape=jax.ShapeDtypeStruct((b, t, ATTN_W), BF16),
        grid=(b, HEADS),
        in_specs=in_specs,
        out_specs=pl.BlockSpec((None, t, HEAD_DIM), lambda bi, h: (bi, 0, h)),
        scratch_shapes=[pltpu.VMEM((N_DIL, t, LANES), F32), pltpu.VMEM((N_DIL, t, LANES), F32),
                        pltpu.VMEM((N_DIL, nb, ATTN_BLK, LANES), F32)],
        compiler_params=_cparams(("parallel", "arbitrary")),
        name="attn_prompt",
    )(*([qkv] * 9))


def _attn_sample_kernel(x_ref, c0, c1, c2, o_ref):
    caches = (c0, c1, c2)
    outs, lses = [], []
    for g in range(N_DIL):
        r0 = _kv_col(g) // HEAD_DIM
        q = x_ref[g * HEADS:(g + 1) * HEADS, :] * (HEAD_DIM ** -0.5)
        k_new = x_ref[r0:r0 + HEADS, :]
        v_new = x_ref[r0 + HEADS:r0 + 2 * HEADS, :]
        kc = caches[g][:, 0:HEADS, :]
        vc = caches[g][:, HEADS:2 * HEADS, :]
        s = jnp.sum(kc * q[None], axis=-1, keepdims=True)
        s_new = jnp.sum(q * k_new, axis=-1, keepdims=True)
        m = jnp.maximum(jnp.max(s, axis=0), s_new)
        p = jnp.exp(s - m[None])
        p_new = jnp.exp(s_new - m)
        l = jnp.sum(p, axis=0) + p_new
        o = jnp.sum(p * vc, axis=0) + p_new * v_new
        outs.append(o / l[0:1])
        lses.append(m + jnp.log(l))
    m = jnp.maximum(jnp.maximum(lses[0], lses[1]), lses[2])
    es = [jnp.exp(x - m) for x in lses]
    num = es[0] * outs[0] + es[1] * outs[1] + es[2] * outs[2]
    o_ref[...] = num / (es[0] + es[1] + es[2])


def attn_sample(qkv, caches, layer):
    b = qkv.shape[0]
    rows = QKV_COLS // HEAD_DIM
    in_specs = [pl.BlockSpec((None, rows, HEAD_DIM), lambda bi: (bi, 0, 0))]
    args = [qkv.reshape(b, rows, HEAD_DIM)]
    for (win, dil), cache in zip(DIL_GROUPS, caches):
        n_layers, _, length = cache.shape[:3]
        assert length == win and length // dil == N_BACK
        args.append(cache.reshape(n_layers, b, N_BACK, dil * 2 * HEADS, HEAD_DIM))
        in_specs.append(pl.BlockSpec((None, None, N_BACK, 2 * HEADS, HEAD_DIM), lambda bi: (layer, bi, 0, 0, 0)))
    out = pl.pallas_call(
        _attn_sample_kernel,
        out_shape=jax.ShapeDtypeStruct((b, HEADS, HEAD_DIM), F32),
        grid=(b,),
        in_specs=in_specs,
        out_specs=pl.BlockSpec((None, HEADS, HEAD_DIM), lambda bi: (bi, 0, 0)),
        compiler_params=_cparams(("parallel",)),
        name="attn_sample",
    )(*args)
    return out.reshape(b, ATTN_W).astype(BF16)


def _mm_res_kernel(a_ref, *refs, nj, gated):
    if gated:
        wv_ref, wg_ref, r_ref, g_ref, o_ref, y_scr = refs
    else:
        wv_ref, r_ref, g_ref, o_ref, y_scr = refs
    j = pl.program_id(1)
    a = a_ref[...]
    y = jnp.dot(a, wv_ref[...], preferred_element_type=F32)
    if gated:
        y = y * jax.nn.sigmoid(jnp.dot(a, wg_ref[...], preferred_element_type=F32))
    if nj == 1:
        o_ref[...] = r_ref[...] + _rms(y, g_ref[...])
        return
    y_scr[j] = y

    @pl.when(j == nj - 1)
    def _():
        full = jnp.concatenate([y_scr[t] for t in range(nj)], axis=-1)
        o_ref[...] = r_ref[...] + _rms(full, g_ref[...])


def matmul_residual(a, w_all, layer, resid, g, *, tm, tn, gated=False, name="matmul_residual"):
    m, k = a.shape
    d = resid.shape[1]
    assert m % tm == 0 and d % tn == 0 and w_all.shape[2] == (2 * d if gated else d)
    nj = d // tn
    w_specs = [pl.BlockSpec((None, k, tn), lambda i, j: (layer, 0, j))]
    if gated:
        w_specs.append(pl.BlockSpec((None, k, tn), lambda i, j: (layer, 0, nj + j)))
    scr_shape = (nj, tm, tn) if nj > 1 else (1, SUBLANES, LANES)
    return pl.pallas_call(
        functools.partial(_mm_res_kernel, nj=nj, gated=gated),
        out_shape=jax.ShapeDtypeStruct((m, d), F32),
        grid=(m // tm, nj),
        in_specs=[pl.BlockSpec((tm, k), lambda i, j: (i, 0))] + w_specs + [
            pl.BlockSpec((tm, d), lambda i, j: (i, 0)),
            pl.BlockSpec((1, d), lambda i, j: (0, 0)),
        ],
        out_specs=pl.BlockSpec((tm, d), lambda i, j: (i, 0)),
        scratch_shapes=[pltpu.VMEM(scr_shape, F32)],
        compiler_params=_cparams(("parallel", "arbitrary")),
        name=name,
    )(a, *([w_all] * len(w_specs)), resid, g.reshape(1, d))


def _conv_gate(ug, ug1, ug2, uv, uv1, uv2, cwg, cwv, cbg, cbv):
    gate = cbg + cwg[0:1] * ug2 + cwg[1:2] * ug1 + cwg[2:3] * ug
    val = cbv + cwv[0:1] * uv2 + cwv[1:2] * uv1 + cwv[2:3] * uv
    return (gate * jax.nn.sigmoid(gate) * val).astype(BF16)


def _ffn_seq_kernel(x_ref, xh_ref, g_ref, wg_ref, wv_ref, cwg_ref, cwv_ref, cbg_ref, cbv_ref, wd_ref, gn_ref,
                    o_ref, sg_ref, sv_ref, h_ref, u_scr, *, tiles_per_seq, nj):
    halo = BF16_ROWS
    tm = x_ref.shape[0]
    j = pl.program_id(1)

    @pl.when(j == 0)
    def _():
        g = g_ref[...]
        keep = jnp.where(pl.program_id(0) % tiles_per_seq == 0, 0.0, 1.0)
        h_ref[0:halo] = (_rms(xh_ref[...], g) * keep).astype(BF16)
        h_ref[halo:] = _rms(x_ref[...], g).astype(BF16)
        o_ref[...] = jnp.zeros(o_ref.shape, F32)

    wg, wv, wd = wg_ref[...], wv_ref[...], wd_ref[...]
    cwg, cwv, cbg, cbv = cwg_ref[...], cwv_ref[...], cbg_ref[...], cbv_ref[...]
    rc = min(FFN_ROW_CHUNK, tm)
    for c in range(tm // rc):
        lo = 0 if c == 0 else halo + c * rc
        hi = halo + (c + 1) * rc
        hc = h_ref[lo:hi, :]
        u_scr[0, lo:hi, :] = jnp.dot(hc, wg, preferred_element_type=F32)
        u_scr[1, lo:hi, :] = jnp.dot(hc, wv, preferred_element_type=F32)
        base = halo + c * rc
        taps = [[u_scr[s, pl.ds(base - k, rc), :] for k in range(CONV_WIDTH)] for s in range(2)]
        act = _conv_gate(*taps[0], *taps[1], cwg, cwv, cbg, cbv)
        for n0 in range(0, wd.shape[1], MXU_COLS):
            o_ref[c * rc:(c + 1) * rc, n0:n0 + MXU_COLS] += jnp.dot(act, wd[:, n0:n0 + MXU_COLS],
                                                                   preferred_element_type=F32)
    sg_ref[...] = u_scr[0, halo + tm - SUBLANES:halo + tm, :]
    sv_ref[...] = u_scr[1, halo + tm - SUBLANES:halo + tm, :]

    @pl.when(j == nj - 1)
    def _():
        o_ref[...] = x_ref[...] + _rms(o_ref[...], gn_ref[...])


def _ffn_step_kernel(x_ref, g_ref, wg_ref, wv_ref, g1_ref, v1_ref, g2_ref, v2_ref,
                     cwg_ref, cwv_ref, cbg_ref, cbv_ref, wd_ref, gn_ref, o_ref, sg_ref, sv_ref, h_ref, *, nj):
    j = pl.program_id(1)

    @pl.when(j == 0)
    def _():
        h_ref[...] = _rms(x_ref[...], g_ref[...]).astype(BF16)
        o_ref[...] = jnp.zeros(o_ref.shape, F32)

    h = h_ref[...]
    ug = jnp.dot(h, wg_ref[...], preferred_element_type=F32)
    uv = jnp.dot(h, wv_ref[...], preferred_element_type=F32)
    sg_ref[...] = ug
    sv_ref[...] = uv
    act = _conv_gate(ug, g1_ref[...], g2_ref[...], uv, v1_ref[...], v2_ref[...],
                     cwg_ref[...], cwv_ref[...], cbg_ref[...], cbv_ref[...])
    o_ref[...] += jnp.dot(act, wd_ref[...], preferred_element_type=F32)

    @pl.when(j == nj - 1)
    def _():
        o_ref[...] = x_ref[...] + _rms(o_ref[...], gn_ref[...])


def conv_ffn(x, g_in, g_out, w_all, cw_all, cb_all, wd_all, layer, *, tm, tn, seq_len=None, prev_all=None):
    m, d = x.shape
    n = w_all.shape[3]
    assert m % tm == 0 and n % tn == 0
    nj = n // tn
    gv = lambda rows: [pl.BlockSpec((None, None, rows, tn), lambda i, j, s=s: (layer, s, 0, j)) for s in range(2)]
    w_specs = gv(d)
    c_specs = gv(CONV_WIDTH) + gv(1)
    c_args = [cw_all, cw_all, cb_all, cb_all]
    d_specs = [pl.BlockSpec((None, tn, d), lambda i, j: (layer, j, 0)), pl.BlockSpec((1, d), lambda i, j: (0, 0))]
    d_args = [wd_all, g_out.reshape(1, d)]
    g_spec = pl.BlockSpec((1, d), lambda i, j: (0, 0))
    tile = pl.BlockSpec((tm, tn), lambda i, j: (i, j))
    if prev_all is None:
        assert seq_len % tm == 0 and tm % BF16_ROWS == 0
        hb = tm // BF16_ROWS
        x_spec = pl.BlockSpec((tm, d), lambda i, j: (i, 0), pipeline_mode=pl.Buffered(1))
        halo_spec = pl.BlockSpec((BF16_ROWS, d), lambda i, j: (jnp.maximum(i * hb - 1, 0), 0))
        body = functools.partial(_ffn_seq_kernel, tiles_per_seq=seq_len // tm, nj=nj)
        in_specs = [x_spec, halo_spec, g_spec] + w_specs + c_specs + d_specs
        args = [x, x, g_in.reshape(1, d), w_all, w_all] + c_args + d_args
        st_rows, st_spec = (m // tm) * SUBLANES, pl.BlockSpec((SUBLANES, tn), lambda i, j: (i, j))
        scratch = [pltpu.VMEM((tm + BF16_ROWS, d), BF16), pltpu.VMEM((2, tm + BF16_ROWS, tn), F32)]
    else:
        x_spec = pl.BlockSpec((tm, d), lambda i, j: (i, 0))
        body = functools.partial(_ffn_step_kernel, nj=nj)
        prev_specs = [pl.BlockSpec((None, None, None, tm, tn), lambda i, j, r=r, s=s: (layer, r, s, i, j))
                      for r in (1, 0) for s in range(2)]
        in_specs = [x_spec, g_spec] + w_specs + prev_specs + c_specs + d_specs
        args = [x, g_in.reshape(1, d), w_all, w_all] + [prev_all] * 4 + c_args + d_args
        st_rows, st_spec = m, tile
        scratch = [pltpu.VMEM((tm, d), BF16)]
    st_shape = jax.ShapeDtypeStruct((st_rows, n), F32)
    return pl.pallas_call(
        body,
        out_shape=(jax.ShapeDtypeStruct((m, d), F32), st_shape, st_shape),
        grid=(m // tm, nj),
        in_specs=in_specs,
        out_specs=(pl.BlockSpec((tm, d), lambda i, j: (i, 0)), st_spec, st_spec),
        scratch_shapes=scratch,
        compiler_params=_cparams(("parallel", "arbitrary")),
        name="conv_ffn",
    )(*args)


def _ssm_params_kernel(lre_ref, lim_ref, ldt_ref, bre_ref, bim_ref, are_ref, aim_ref, bbre_ref, bbim_ref):
    lre = jnp.minimum(lre_ref[...], -1e-4)
    lim = lim_ref[...]
    dt = jnp.exp(ldt_ref[...])
    mag = jnp.exp(lre * dt)
    a_re = mag * jnp.cos(lim * dt)
    a_im = mag * jnp.sin(lim * dt)
    are_ref[...] = a_re
    aim_ref[...] = a_im
    x, y = a_re - 1.0, a_im
    den = lre * lre + lim * lim
    c_re = (x * lre + y * lim) / den
    c_im = (y * lre - x * lim) / den
    bre, bim = bre_ref[...], bim_ref[...]
    bbre_ref[...] = c_re * bre - c_im * bim
    bbim_ref[...] = c_re * bim + c_im * bre


def ssm_params(lam_re, lam_im, log_dt, b_re, b_im):
    g, p = lam_re.shape
    shp_a = jax.ShapeDtypeStruct((g, 1, p), F32)
    shp_b = jax.ShapeDtypeStruct((g, SSM_CH, p), F32)
    a_re, a_im, bb_re, bb_im = pl.pallas_call(
        _ssm_params_kernel,
        out_shape=(shp_a, shp_a, shp_b, shp_b),
        compiler_params=pltpu.CompilerParams(vmem_limit_bytes=VMEM_LIMIT),
        name="ssm_params",
    )(lam_re.reshape(g, 1, p), lam_im.reshape(g, 1, p), log_dt.reshape(g, 1, 1),
      jnp.swapaxes(b_re, 1, 2), jnp.swapaxes(b_im, 1, 2))
    return a_re.reshape(g, p), a_im.reshape(g, p), bb_re, bb_im


def _gelu_tanh(y):
    return 0.5 * y * (1.0 + jnp.tanh(math.sqrt(2.0 / math.pi) * (y + 0.044715 * (y * y * y))))


def _ssm_scan_kernel(u_ref, bm_ref, cm_ref, are_ref, aim_ref, d_ref, h0r_ref, h0i_ref,
                     z_ref, sr_ref, si_ref, bu_scr, xr_scr, xi_scr, *, n_chunks):
    c = pl.program_id(1)
    nb, tc = u_ref.shape[0], u_ref.shape[1]
    pitch = SSM_PITCH
    half = SSM_SLABS // 2

    @pl.when(c == 0)
    def _():
        xr_scr[...] = h0r_ref[...]
        xi_scr[...] = h0i_ref[...]

    bm = bm_ref[...]
    for b in range(nb):
        r = jnp.dot(u_ref[b].astype(BF16), bm, preferred_element_type=F32)
        for k in range(SSM_SLABS):
            bu_scr[b, k * pitch:k * pitch + tc, :] = r[:, k * LANES:(k + 1) * LANES]

    ar = are_ref[...]
    ai = aim_ref[...]

    def step(t, carry):
        new = []
        for b in range(nb):
            xr, xi = carry[b]
            re_rows = pl.ds(t, half, stride=pitch)
            im_rows = pl.ds(half * pitch + t, half, stride=pitch)
            nr = ar * xr - ai * xi + bu_scr[b, re_rows, :]
            ni = ar * xi + ai * xr + bu_scr[b, im_rows, :]
            bu_scr[b, re_rows, :] = nr
            bu_scr[b, im_rows, :] = ni
            new.append((nr, ni))
        return tuple(new)

    init = tuple((xr_scr[b], xi_scr[b]) for b in range(nb))
    fin = lax.fori_loop(0, tc, step, init, unroll=8)
    for b in range(nb):
        xr_scr[b] = fin[b][0]
        xi_scr[b] = fin[b][1]

    cm = cm_ref[...]
    for b in range(nb):
        xs = jnp.concatenate([bu_scr[b, k * pitch:k * pitch + tc, :] for k in range(SSM_SLABS)], axis=-1)
        y = jnp.dot(xs.astype(BF16), cm, preferred_element_type=F32) + d_ref[...] * u_ref[b]
        z_ref[b] = _gelu_tanh(y).astype(z_ref.dtype)

    @pl.when(c == n_chunks - 1)
    def _():
        sr_ref[...] = xr_scr[...]
        si_ref[...] = xi_scr[...]


def ssm_scan(u, bmat, cmat, a_re, a_im, d_skip, h0_re, h0_im):
    b, t, d = u.shape
    tc = SSM_CHUNK
    n_chunks = t // tc
    half = SSM_SLABS // 2
    st_spec = pl.BlockSpec((b, None, half, LANES), lambda j, c: (0, j, 0, 0))
    st_shape = jax.ShapeDtypeStruct((b, SSM_TILES, half, LANES), F32)
    return pl.pallas_call(
        functools.partial(_ssm_scan_kernel, n_chunks=n_chunks),
        out_shape=(jax.ShapeDtypeStruct((b, t, d), BF16), st_shape, st_shape),
        grid=(SSM_TILES, n_chunks),
        in_specs=[
            pl.BlockSpec((b, tc, SSM_TILE_CH), lambda j, c: (0, c, j)),
            pl.BlockSpec((None, SSM_TILE_CH, 2 * SSM_TILE_ST), lambda j, c: (j, 0, 0)),
            pl.BlockSpec((None, 2 * SSM_TILE_ST, SSM_TILE_CH), lambda j, c: (j, 0, 0)),
            pl.BlockSpec((None, half, LANES), lambda j, c: (j, 0, 0)),
            pl.BlockSpec((None, half, LANES), lambda j, c: (j, 0, 0)),
            pl.BlockSpec((1, SSM_TILE_CH), lambda j, c: (0, j)),
            st_spec, st_spec,
        ],
        out_specs=(pl.BlockSpec((b, tc, SSM_TILE_CH), lambda j, c: (0, c, j)), st_spec, st_spec),
        scratch_shapes=[
            pltpu.VMEM((b, SSM_SLABS * SSM_PITCH, LANES), F32),
            pltpu.VMEM((b, half, LANES), F32),
            pltpu.VMEM((b, half, LANES), F32),
        ],
        compiler_params=_cparams(("parallel", "arbitrary")),
        name="ssm_scan",
    )(u, bmat, cmat, a_re.reshape(SSM_TILES, half, LANES), a_im.reshape(SSM_TILES, half, LANES),
      d_skip.reshape(1, d), h0_re, h0_im)


def _ssm_step_kernel(u_ref, bm_ref, cm_ref, are_ref, aim_ref, d_ref, h0r_ref, h0i_ref, z_ref, sr_ref, si_ref):
    u = u_ref[...]
    bu = jnp.dot(u.astype(BF16), bm_ref[...], preferred_element_type=F32)
    ar, ai = are_ref[...], aim_ref[...]
    hr, hi = h0r_ref[...], h0i_ref[...]
    xr = ar * hr - ai * hi + bu[:, :SSM_TILE_ST]
    xi = ar * hi + ai * hr + bu[:, SSM_TILE_ST:]
    sr_ref[...] = xr
    si_ref[...] = xi
    xs = jnp.concatenate([xr, xi], axis=-1).astype(BF16)
    y = jnp.dot(xs, cm_ref[...], preferred_element_type=F32) + d_ref[...] * u
    z_ref[...] = _gelu_tanh(y).astype(z_ref.dtype)


def ssm_step(u, bmat, cmat, a_re, a_im, d_skip, h0_re, h0_im):
    b, d = u.shape
    st_spec = pl.BlockSpec((None, b, SSM_TILE_ST), lambda j: (j, 0, 0))
    a_spec = pl.BlockSpec((None, 1, SSM_TILE_ST), lambda j: (j, 0, 0))
    st_shape = jax.ShapeDtypeStruct((SSM_TILES, b, SSM_TILE_ST), F32)
    return pl.pallas_call(
        _ssm_step_kernel,
        out_shape=(jax.ShapeDtypeStruct((b, d), BF16), st_shape, st_shape),
        grid=(SSM_TILES,),
        in_specs=[
            pl.BlockSpec((b, SSM_TILE_CH), lambda j: (0, j)),
            pl.BlockSpec((None, SSM_TILE_CH, 2 * SSM_TILE_ST), lambda j: (j, 0, 0)),
            pl.BlockSpec((None, 2 * SSM_TILE_ST, SSM_TILE_CH), lambda j: (j, 0, 0)),
            a_spec, a_spec,
            pl.BlockSpec((1, SSM_TILE_CH), lambda j: (0, j)),
            st_spec, st_spec,
        ],
        out_specs=(pl.BlockSpec((b, SSM_TILE_CH), lambda j: (0, j)), st_spec, st_spec),
        compiler_params=_cparams(("parallel",)),
        name="ssm_step",
    )(u, bmat, cmat, a_re.reshape(SSM_TILES, 1, SSM_TILE_ST), a_im.reshape(SSM_TILES, 1, SSM_TILE_ST),
      d_skip.reshape(1, d), h0_re, h0_im)


def _block_diag_mats(bb_re, bb_im, c_re, c_im):
    ch_group = jnp.arange(SSM_TILE_CH) // SSM_CH
    st_group = jnp.arange(SSM_TILE_ST) // SSM_STATE
    same = ch_group[:, None] == st_group[None, :]

    def bmat(bb):
        rows = bb.reshape(SSM_TILES, SSM_TILE_CH, SSM_STATE)
        return jnp.where(same, jnp.tile(rows, (1, 1, SSM_TILE_G)), 0.0)

    def cmat(cc):
        cols = jnp.swapaxes(cc.reshape(SSM_TILES, SSM_TILE_CH, SSM_STATE), 1, 2)
        return jnp.where(same.T, jnp.tile(cols, (1, SSM_TILE_G, 1)), 0.0)

    b_all = jnp.concatenate([bmat(bb_re), bmat(bb_im)], axis=2).astype(BF16)
    c_all = jnp.concatenate([cmat(c_re), cmat(-c_im)], axis=1).astype(BF16)
    return b_all, c_all


def _cast_pad_kernel(x_ref, o_ref):
    r, c = x_ref.shape
    rr, cc = o_ref.shape
    o_ref[0:r, 0:c] = x_ref[...].astype(o_ref.dtype)
    if cc > c:
        o_ref[:, c:] = jnp.zeros((rr, cc - c), o_ref.dtype)
    if rr > r:
        o_ref[r:, 0:c] = jnp.zeros((rr - r, c), o_ref.dtype)


def _cast_call(x, out_shape, grid, in_spec, out_spec, name):
    return pl.pallas_call(
        _cast_pad_kernel,
        out_shape=jax.ShapeDtypeStruct(out_shape, BF16),
        grid=grid,
        in_specs=[in_spec],
        out_specs=out_spec,
        compiler_params=_cparams(("parallel",) * len(grid)),
        name=name,
    )(x)


def prep_w_qkv(w):
    n_layers, d, n = w.shape
    tr = 512

    def src_block(ob):
        kv = ob - N_DIL
        return jnp.where(ob < N_DIL, ob, N_DIL + (kv % 2) * N_DIL + kv // 2)

    return _cast_call(
        w, (n_layers, d, n), (n_layers, d // tr, n // ATTN_W),
        pl.BlockSpec((None, tr, ATTN_W), lambda l, r, ob: (l, r, src_block(ob))),
        pl.BlockSpec((None, tr, ATTN_W), lambda l, r, ob: (l, r, ob)), "prep_w_qkv")


def prep_w_up(w):
    n_layers, d, _ = w.shape
    tr = 256
    return _cast_call(
        w, (n_layers, 2, d, D_FF_PAD), (n_layers, 2, d // tr),
        pl.BlockSpec((None, tr, D_FF), lambda l, s, r: (l, r, s)),
        pl.BlockSpec((None, None, tr, D_FF_PAD), lambda l, s, r: (l, s, r, 0)), "prep_w_up")


def prep_w_down(w):
    n_layers, _, d = w.shape
    tc = 256
    return _cast_call(
        w, (n_layers, D_FF_PAD, d), (n_layers, d // tc),
        pl.BlockSpec((None, D_FF, tc), lambda l, c: (l, 0, c)),
        pl.BlockSpec((None, D_FF_PAD, tc), lambda l, c: (l, 0, c)), "prep_w_down")


def _ff_split(a, axis):
    a = a.reshape(a.shape[:-1] + (2, D_FF))
    a = jnp.pad(a, [(0, 0)] * (a.ndim - 1) + [(0, D_FF_PAD - D_FF)])
    return jnp.moveaxis(a, -2, axis)


def _ff_join(gate, val):
    return jnp.concatenate([gate[..., :D_FF], val[..., :D_FF]], axis=-1)


def kernel(x_prompt, x_sample, cache_kv_g0, cache_kv_g1, cache_kv_g2, state_ssm, state_conv, norm_g, w_qkv,
           w_attn_o, w_ssm_in, lambda_re, lambda_im, log_dt, b_re, b_im, c_re, c_im, d_skip, w_glu, w_up,
           conv_w, conv_b, w_down):
    bp, t, d = x_prompt.shape
    bs = x_sample.shape[0]
    mp = bp * t
    xp = x_prompt.reshape(mp, d)
    xs = x_sample.reshape(bs, d)
    caches = (cache_kv_g0, cache_kv_g1, cache_kv_g2)
    tm_big, tm = 1024, 512

    wq_all = prep_w_qkv(w_qkv)
    wo_all = w_attn_o.astype(BF16)
    win_all = w_ssm_in.astype(BF16)
    wglu_all = w_glu.astype(BF16)
    wu_all = prep_w_up(w_up)
    wd_all = prep_w_down(w_down)
    cw_all = _ff_split(conv_w, 1)
    cb_all = _ff_split(conv_b.reshape(DEPTH, 1, 2 * D_FF), 1)
    prev_all = jnp.transpose(_ff_split(state_conv, 3), (0, 2, 3, 1, 4))

    rope_p = rope_tables(t, 0)
    rope_s = rope_tables(SUBLANES, PAST_LEN)
    rope_s = tuple(jnp.broadcast_to(r[0:1], (bs, LANES)) for r in rope_s)

    kv_wide = None
    kv_p = [[] for _ in range(N_DIL)]
    kv_s = [[] for _ in range(N_DIL)]
    ssm_p, ssm_s, conv_p, conv_s = [], [], [], []

    for i in range(DEPTH):
        li = i // N_MIXERS
        if i % N_MIXERS == 0:
            assert DIL_GROUPS[-1][0] >= t
            qkv_p, kv_wide = qkv_project(xp, norm_g[i, 0], wq_all, li, rope_p, tm=tm_big,
                                         stack=(w_qkv.shape[0], kv_wide))
            qkv_s = qkv_project(xs, norm_g[i, 0], wq_all, li, rope_s, tm=bs)
            op = attn_prompt(qkv_p.reshape(bp, t, QKV_COLS))
            os_ = attn_sample(qkv_s, caches, li)
            xp = matmul_residual(op.reshape(mp, ATTN_W), wo_all, li, xp, norm_g[i, 1], tm=tm, tn=d, name="attn_out")
            xs = matmul_residual(os_, wo_all, li, xs, norm_g[i, 1], tm=bs, tn=d, name="attn_out")
            for g, (win, _) in enumerate(DIL_GROUPS):
                keep = min(win, t)
                c0 = _kv_col(g)
                if g < N_DIL - 1:
                    kv_p[g].append(qkv_p.reshape(bp, t, QKV_COLS)[:, t - keep:, c0:c0 + KV_COLS]
                                   .reshape(bp, keep, 2, HEADS, HEAD_DIM))
                kv_s[g].append(qkv_s[:, c0:c0 + KV_COLS].reshape(bs, 1, 2, HEADS, HEAD_DIM))
        else:
            a_re, a_im, bb_re, bb_im = ssm_params(lambda_re[li], lambda_im[li], log_dt[li], b_re[li], b_im[li])
            bmat, cmat = _block_diag_mats(bb_re, bb_im, c_re[li], c_im[li])
            half = SSM_SLABS // 2
            up_ = norm_matmul(xp, norm_g[i, 0], win_all, li, tm=tm_big, tn=1024, name="ssm_in")
            us_ = norm_matmul(xs, norm_g[i, 0], win_all, li, tm=bs, tn=1024, name="ssm_in")
            zeros = jnp.zeros((bp, SSM_TILES, half, LANES), F32)
            zp, sp_re, sp_im = ssm_scan(up_.reshape(bp, t, d), bmat, cmat, a_re, a_im, d_skip[li], zeros, zeros)
            h0 = state_ssm[li].reshape(bs, SSM_TILES, SSM_TILE_ST, 2)
            h0_re = jnp.swapaxes(h0[..., 0], 0, 1)
            h0_im = jnp.swapaxes(h0[..., 1], 0, 1)
            zs, ss_re, ss_im = ssm_step(us_, bmat, cmat, a_re, a_im, d_skip[li], h0_re, h0_im)
            xp = matmul_residual(zp.reshape(mp, d), wglu_all, li, xp, norm_g[i, 1], tm=tm, tn=512, gated=True,
                                 name="ssm_out")
            xs = matmul_residual(zs, wglu_all, li, xs, norm_g[i, 1], tm=bs, tn=512, gated=True, name="ssm_out")
            ssm_p.append(jnp.stack([sp_re.reshape(bp, SSM_GROUPS, SSM_STATE),
                                    sp_im.reshape(bp, SSM_GROUPS, SSM_STATE)], axis=-1))
            ssm_s.append(jnp.stack([jnp.swapaxes(ss_re, 0, 1).reshape(bs, SSM_GROUPS, SSM_STATE),
                                    jnp.swapaxes(ss_im, 0, 1).reshape(bs, SSM_GROUPS, SSM_STATE)], axis=-1))

        xp, sg_p, sv_p = conv_ffn(xp, norm_g[i, 2], norm_g[i, 3], wu_all, cw_all, cb_all, wd_all, i,
                                  tm=tm_big, tn=256, seq_len=t)
        xs, sg_s, sv_s = conv_ffn(xs, norm_g[i, 2], norm_g[i, 3], wu_all, cw_all, cb_all, wd_all, i,
                                  tm=bs, tn=512, prev_all=prev_all)
        tiles = t // tm_big
        tail = _ff_join(sg_p, sv_p).reshape(bp, tiles, SUBLANES, 2 * D_FF)
        conv_p.append(tail[:, tiles - 1, SUBLANES - (CONV_WIDTH - 1):])
        conv_s.append(jnp.concatenate([state_conv[i][:, 1:], _ff_join(sg_s, sv_s)[:, None]], axis=1))

    return (xp.reshape(bp, t, d), xs.reshape(bs, 1, d),
            jnp.stack(kv_p[0]), jnp.stack(kv_s[0]),
            jnp.stack(kv_p[1]), jnp.stack(kv_s[1]),
            kv_wide.reshape(w_qkv.shape[0], bp, t, 2, HEADS, HEAD_DIM), jnp.stack(kv_s[2]),
            jnp.stack(ssm_p), jnp.stack(ssm_s),
            jnp.stack(conv_p), jnp.stack(conv_s))
```

```python
import functools
import math

import jax
import jax.numpy as jnp
from jax import lax
from jax.experimental import pallas as pl
from jax.experimental.pallas import tpu as pltpu

F32 = jnp.float32
BF16 = jnp.bfloat16

D_MODEL = 2048
DEPTH = 4
PAST_LEN = 16384
N_MIXERS = 2
HEAD_DIM = 128
HEADS = D_MODEL // (2 * HEAD_DIM)
DIL_GROUPS = ((128, 1), (512, 4), (2048, 16))
N_DIL = len(DIL_GROUPS)
N_BACK = 128
ROT_DIM = HEAD_DIM // 4
ROPE_THETA = 500000.0
NEG_INF = -1e30
SSM_CH = 16
SSM_GROUPS = D_MODEL // SSM_CH
SSM_STATE = 64
D_FF = ((8 * D_MODEL // 3 + 127) // 128) * 128
CONV_WIDTH = 3
RMS_EPS = 1e-6

ATTN_W = HEADS * HEAD_DIM
Q_COLS = N_DIL * ATTN_W
KV_COLS = 2 * ATTN_W
QKV_COLS = 3 * N_DIL * ATTN_W

LANES = 128
SUBLANES = 8
BF16_ROWS = 16
MXU_COLS = 256
VMEM_LIMIT = 52 * 1024 * 1024
ROW_CHUNK = 256
FFN_ROW_CHUNK = 512

FF_ALIGN = 512
D_FF_PAD = ((D_FF + FF_ALIGN - 1) // FF_ALIGN) * FF_ALIGN

SSM_TILE_G = 16
SSM_TILES = SSM_GROUPS // SSM_TILE_G
SSM_TILE_CH = SSM_TILE_G * SSM_CH
SSM_TILE_ST = SSM_TILE_G * SSM_STATE
SSM_SLABS = 2 * SSM_TILE_ST // LANES
SSM_CHUNK = 256
SSM_PITCH = SSM_CHUNK + SUBLANES


def _cparams(sem):
    return pltpu.CompilerParams(dimension_semantics=sem, vmem_limit_bytes=VMEM_LIMIT)


def _rms(x, g):
    ms = jnp.mean(x * x, axis=-1, keepdims=True)
    return x * lax.rsqrt(ms + RMS_EPS) * g


def _rope_table_kernel(inv_ref, cos_ref, sa_ref, sb_ref, *, pos_base):
    rows = cos_ref.shape[0]
    pos = (lax.broadcasted_iota(jnp.int32, (rows, LANES), 0) + pos_base).astype(F32)
    lane = lax.broadcasted_iota(jnp.int32, (rows, LANES), 1)
    ang = pos * inv_ref[...]
    c = jnp.cos(ang)
    s = jnp.sin(ang)
    half = ROT_DIM // 2
    cos_ref[...] = jnp.where(lane < ROT_DIM, c, 1.0)
    sa_ref[...] = jnp.where(lane < half, -s, 0.0)
    sb_ref[...] = jnp.where((lane >= half) & (lane < ROT_DIM), s, 0.0)


def rope_tables(rows, pos_base):
    half = ROT_DIM // 2
    inv = ROPE_THETA ** (-(jnp.arange(half, dtype=F32) * (2.0 / ROT_DIM)))
    inv_lane = jnp.tile(inv, LANES // half).reshape(1, LANES)
    shp = jax.ShapeDtypeStruct((rows, LANES), F32)
    return pl.pallas_call(
        functools.partial(_rope_table_kernel, pos_base=pos_base),
        out_shape=(shp, shp, shp),
        name="rope_tables",
    )(inv_lane)


QKV_TN = 512
Q_TILES = Q_COLS // QKV_TN
KV_TILES = KV_COLS // QKV_TN
assert KV_TILES & (KV_TILES - 1) == 0


def _qkv_kernel(x_ref, g_ref, w_ref, cos_ref, sa_ref, sb_ref, *refs, stacked):
    if stacked:
        o_ref, s_ref, h_ref = refs[-3:]
    else:
        (o_ref, h_ref), s_ref = refs, None
    j = pl.program_id(1)

    @pl.when(j == 0)
    def _():
        h_ref[...] = _rms(x_ref[...], g_ref[...]).astype(BF16)

    is_rope = (j < Q_TILES) | (((j - Q_TILES) & (KV_TILES - 1)) < KV_TILES // 2)
    w = w_ref[...]
    half = ROT_DIM // 2
    tm = x_ref.shape[0]
    rc = min(ROW_CHUNK, tm)
    for c in range(tm // rc):
        rows = slice(c * rc, (c + 1) * rc)
        y = jnp.dot(h_ref[rows, :], w, preferred_element_type=F32)
        cs, sa, sb = cos_ref[rows, :], sa_ref[rows, :], sb_ref[rows, :]
        for h in range(QKV_TN // HEAD_DIM):
            xh = y[:, h * HEAD_DIM:(h + 1) * HEAD_DIM]
            up = pltpu.roll(xh, HEAD_DIM - half, axis=1)
            dn = pltpu.roll(xh, half, axis=1)
            out = jnp.where(is_rope, xh * cs + up * sa + dn * sb, xh)
            o_ref[rows, h * HEAD_DIM:(h + 1) * HEAD_DIM] = out
            if s_ref is not None:
                s_ref[rows, h * HEAD_DIM:(h + 1) * HEAD_DIM] = out


def qkv_project(x, g, w_all, layer, rope, *, tm, stack=None):
    m, d = x.shape
    cos, sa, sb = rope
    tab_tiles = cos.shape[0] // tm
    assert m % tm == 0 and cos.shape[0] % tm == 0 and w_all.shape[2] == QKV_COLS
    tab_spec = pl.BlockSpec((tm, LANES), lambda i, j: (i % tab_tiles, 0))
    in_specs = [
        pl.BlockSpec((tm, d), lambda i, j: (i, 0)),
        pl.BlockSpec((1, d), lambda i, j: (0, 0)),
        pl.BlockSpec((None, d, QKV_TN), lambda i, j: (layer, 0, j)),
        tab_spec, tab_spec, tab_spec,
    ]
    args = [x, g.reshape(1, d), w_all, cos, sa, sb]
    out_shape = jax.ShapeDtypeStruct((m, QKV_COLS), F32)
    out_specs = pl.BlockSpec((tm, QKV_TN), lambda i, j: (i, j))
    aliases = {}
    if stack is not None:
        n_layers, earlier = stack
        base = QKV_COLS // QKV_TN - KV_TILES
        out_shape = (out_shape, jax.ShapeDtypeStruct((n_layers, m, KV_COLS), F32))
        out_specs = (out_specs, pl.BlockSpec((None, tm, QKV_TN), lambda i, j: (layer, i, jnp.maximum(j - base, 0))))
        if earlier is not None:
            aliases = {len(args): 1}
            in_specs.append(pl.BlockSpec(memory_space=pl.ANY))
            args.append(earlier)
    return pl.pallas_call(
        functools.partial(_qkv_kernel, stacked=stack is not None),
        out_shape=out_shape,
        grid=(m // tm, QKV_COLS // QKV_TN),
        in_specs=in_specs,
        out_specs=out_specs,
        scratch_shapes=[pltpu.VMEM((tm, d), BF16)],
        input_output_aliases=aliases,
        compiler_params=_cparams(("parallel", "arbitrary")),
        name="qkv_project",
    )(*args)


def _permute_qkv_cols(w):
    parts = [w[..., :Q_COLS]]
    for g in range(N_DIL):
        parts.append(w[..., Q_COLS + g * ATTN_W:Q_COLS + (g + 1) * ATTN_W])
        parts.append(w[..., 2 * Q_COLS + g * ATTN_W:2 * Q_COLS + (g + 1) * ATTN_W])
    return jnp.concatenate(parts, axis=-1)


def _kv_col(g):
    return Q_COLS + g * KV_COLS


def _norm_mm_kernel(x_ref, g_ref, w_ref, o_ref, h_ref):
    @pl.when(pl.program_id(1) == 0)
    def _():
        h_ref[...] = _rms(x_ref[...], g_ref[...]).astype(BF16)

    o_ref[...] = jnp.dot(h_ref[...], w_ref[...], preferred_element_type=F32).astype(o_ref.dtype)


def norm_matmul(x, g, w_all, layer, *, tm, tn, out_dtype=F32, name="norm_matmul"):
    m, d = x.shape
    n = w_all.shape[2]
    assert m % tm == 0 and n % tn == 0
    return pl.pallas_call(
        _norm_mm_kernel,
        out_shape=jax.ShapeDtypeStruct((m, n), out_dtype),
        grid=(m // tm, n // tn),
        in_specs=[
            pl.BlockSpec((tm, d), lambda i, j: (i, 0)),
            pl.BlockSpec((1, d), lambda i, j: (0, 0)),
            pl.BlockSpec((None, d, tn), lambda i, j: (layer, 0, j)),
        ],
        out_specs=pl.BlockSpec((tm, tn), lambda i, j: (i, j)),
        scratch_shapes=[pltpu.VMEM((tm, d), BF16)],
        compiler_params=_cparams(("parallel", "arbitrary")),
        name=name,
    )(x, g.reshape(1, d), w_all)


ATTN_BLK = 128
assert ATTN_BLK == N_BACK


def _residue_major(ref, dil):
    t = ref.shape[0]
    if dil == 1:
        return ref[...]
    return jnp.concatenate([ref[pl.ds(r, t // dil, stride=dil), :] for r in range(dil)], axis=0)


def _store_token_order(ref, g, dil, val):
    nb = val.shape[0]
    if dil == 1:
        ref[g] = val.reshape(nb * ATTN_BLK, LANES)
        return
    per = nb // dil
    for r in range(dil):
        ref[g, pl.ds(r, per * ATTN_BLK, stride=dil), :] = val[r * per:(r + 1) * per].reshape(per * ATTN_BLK, LANES)


def _attn_prompt_kernel(q0, q1, q2, k0, k1, k2, v0, v1, v2, o_ref, o_scr, l_scr, d_scr):
    qs, ks, vs = (q0, q1, q2), (k0, k1, k2), (v0, v1, v2)
    t = q0.shape[0]
    nb = t // ATTN_BLK
    blk = ATTN_BLK

    for g, (_, dil) in enumerate(DIL_GROUPS):
        per = nb // dil
        to_blocks = lambda a: a.reshape(nb, blk, HEAD_DIM)
        qb = to_blocks((_residue_major(qs[g], dil) * (HEAD_DIM ** -0.5)).astype(BF16))
        kb = to_blocks(_residue_major(ks[g], dil).astype(BF16))
        vb = to_blocks(_residue_major(vs[g], dil).astype(BF16))
        if per > 1:
            kk = jnp.concatenate([jnp.concatenate([kb[:1], kb[:-1]], axis=0), kb], axis=1)
            vv = jnp.concatenate([jnp.concatenate([vb[:1], vb[:-1]], axis=0), vb], axis=1)
            nk = 2 * blk
            bi = lax.broadcasted_iota(jnp.int32, (nb, 1, nk), 0)
            ci = lax.broadcasted_iota(jnp.int32, (nb, 1, nk), 2)
            no_prev = jnp.where(((bi & (per - 1)) == 0) & (ci < blk), NEG_INF, 0.0)
        else:
            kk, vv, nk = kb, vb, blk
            no_prev = None
        rel = (lax.broadcasted_iota(jnp.int32, (blk, nk), 0) + (nk - blk)
               - lax.broadcasted_iota(jnp.int32, (blk, nk), 1))
        band = jnp.where((rel >= 0) & (rel <= N_BACK), 0.0, NEG_INF)

        s = jnp.einsum('bqd,bkd->bqk', qb, kk, preferred_element_type=F32) + band[None]
        if no_prev is not None:
            s = s + no_prev
        m = jnp.max(s, axis=-1, keepdims=True)
        p = jnp.exp(s - m)
        l = jnp.sum(p, axis=-1, keepdims=True)
        o = jnp.einsum('bqk,bkd->bqd', p.astype(BF16), vv, preferred_element_type=F32)

        @pl.when(pl.program_id(1) == 0)
        def _(g=g, l=l):
            d_scr[g] = jnp.broadcast_to(l, (nb, blk, LANES))

        _store_token_order(o_scr, g, dil, o / d_scr[g])
        _store_token_order(l_scr, g, dil, jnp.broadcast_to(m + jnp.log(l), (nb, blk, LANES)))

    l0, l1, l2 = l_scr[0], l_scr[1], l_scr[2]
    m = jnp.maximum(jnp.maximum(l0, l1), l2)
    e0, e1, e2 = jnp.exp(l0 - m), jnp.exp(l1 - m), jnp.exp(l2 - m)
    num = e0 * o_scr[0] + e1 * o_scr[1] + e2 * o_scr[2]
    o_ref[...] = (num / (e0 + e1 + e2)).astype(o_ref.dtype)


def attn_prompt(qkv):
    b, t, _ = qkv.shape
    assert t % (ATTN_BLK * DIL_GROUPS[-1][1]) == 0

    def spec(col0):
        return pl.BlockSpec((None, t, HEAD_DIM), lambda bi, h: (bi, 0, col0 // HEAD_DIM + h))

    in_specs = ([spec(g * ATTN_W) for g in range(N_DIL)]
                + [spec(_kv_col(g)) for g in range(N_DIL)]
                + [spec(_kv_col(g) + ATTN_W) for g in range(N_DIL)])
    nb = t // ATTN_BLK
    return pl.pallas_call(
        _attn_prompt_kernel,
        out_shape=jax.ShapeDtypeStruct((b, t, ATTN_W), BF16),
        grid=(b, HEADS),
        in_specs=in_specs,
        out_specs=pl.BlockSpec((None, t, HEAD_DIM), lambda bi, h: (bi, 0, h)),
        scratch_shapes=[pltpu.VMEM((N_DIL, t, LANES), F32), pltpu.VMEM((N_DIL, t, LANES), F32),
                        pltpu.VMEM((N_DIL, nb, ATTN_BLK, LANES), F32)],
        compiler_params=_cparams(("parallel", "arbitrary")),
        name="attn_prompt",
    )(*([qkv] * 9))


def _attn_sample_kernel(x_ref, c0, c1, c2, o_ref):
    caches = (c0, c1, c2)
    outs, lses = [], []
    for g in range(N_DIL):
        r0 = _kv_col(g) // HEAD_DIM
        q = x_ref[g * HEADS:(g + 1) * HEADS, :] * (HEAD_DIM ** -0.5)
        k_new = x_ref[r0:r0 + HEADS, :]
        v_new = x_ref[r0 + HEADS:r0 + 2 * HEADS, :]
        kc = caches[g][:, 0:HEADS, :]
        vc = caches[g][:, HEADS:2 * HEADS, :]
        s = jnp.sum(kc * q[None], axis=-1, keepdims=True)
        s_new = jnp.sum(q * k_new, axis=-1, keepdims=True)
        m = jnp.maximum(jnp.max(s, axis=0), s_new)
        p = jnp.exp(s - m[None])
        p_new = jnp.exp(s_new - m)
        l = jnp.sum(p, axis=0) + p_new
        o = jnp.sum(p * vc, axis=0) + p_new * v_new
        outs.append(o / l[0:1])
        lses.append(m + jnp.log(l))
    m = jnp.maximum(jnp.maximum(lses[0], lses[1]), lses[2])
    es = [jnp.exp(x - m) for x in lses]
    num = es[0] * outs[0] + es[1] * outs[1] + es[2] * outs[2]
    o_ref[...] = num / (es[0] + es[1] + es[2])


def attn_sample(qkv, caches, layer):
    b = qkv.shape[0]
    rows = QKV_COLS // HEAD_DIM
    in_specs = [pl.BlockSpec((None, rows, HEAD_DIM), lambda bi: (bi, 0, 0))]
    args = [qkv.reshape(b, rows, HEAD_DIM)]
    for (win, dil), cache in zip(DIL_GROUPS, caches):
        n_layers, _, length = cache.shape[:3]
        assert length == win and length // dil == N_BACK
        args.append(cache.reshape(n_layers, b, N_BACK, dil * 2 * HEADS, HEAD_DIM))
        in_specs.append(pl.BlockSpec((None, None, N_BACK, 2 * HEADS, HEAD_DIM), lambda bi: (layer, bi, 0, 0, 0)))
    out = pl.pallas_call(
        _attn_sample_kernel,
        out_shape=jax.ShapeDtypeStruct((b, HEADS, HEAD_DIM), F32),
        grid=(b,),
        in_specs=in_specs,
        out_specs=pl.BlockSpec((None, HEADS, HEAD_DIM), lambda bi: (bi, 0, 0)),
        compiler_params=_cparams(("parallel",)),
        name="attn_sample",
    )(*args)
    return out.reshape(b, ATTN_W).astype(BF16)


def _mm_res_kernel(a_ref, *refs, nj, gated):
    if gated:
        wv_ref, wg_ref, r_ref, g_ref, o_ref, y_scr = refs
    else:
        wv_ref, r_ref, g_ref, o_ref, y_scr = refs
    j = pl.program_id(1)
    a = a_ref[...]
    y = jnp.dot(a, wv_ref[...], preferred_element_type=F32)
    if gated:
        y = y * jax.nn.sigmoid(jnp.dot(a, wg_ref[...], preferred_element_type=F32))
    if nj == 1:
        o_ref[...] = r_ref[...] + _rms(y, g_ref[...])
        return
    y_scr[j] = y

    @pl.when(j == nj - 1)
    def _():
        full = jnp.concatenate([y_scr[t] for t in range(nj)], axis=-1)
        o_ref[...] = r_ref[...] + _rms(full, g_ref[...])


def matmul_residual(a, w_all, layer, resid, g, *, tm, tn, gated=False, name="matmul_residual"):
    m, k = a.shape
    d = resid.shape[1]
    assert m % tm == 0 and d % tn == 0 and w_all.shape[2] == (2 * d if gated else d)
    nj = d // tn
    w_specs = [pl.BlockSpec((None, k, tn), lambda i, j: (layer, 0, j))]
    if gated:
        w_specs.append(pl.BlockSpec((None, k, tn), lambda i, j: (layer, 0, nj + j)))
    scr_shape = (nj, tm, tn) if nj > 1 else (1, SUBLANES, LANES)
    return pl.pallas_call(
        functools.partial(_mm_res_kernel, nj=nj, gated=gated),
        out_shape=jax.ShapeDtypeStruct((m, d), F32),
        grid=(m // tm, nj),
        in_specs=[pl.BlockSpec((tm, k), lambda i, j: (i, 0))] + w_specs + [
            pl.BlockSpec((tm, d), lambda i, j: (i, 0)),
            pl.BlockSpec((1, d), lambda i, j: (0, 0)),
        ],
        out_specs=pl.BlockSpec((tm, d), lambda i, j: (i, 0)),
        scratch_shapes=[pltpu.VMEM(scr_shape, F32)],
        compiler_params=_cparams(("parallel", "arbitrary")),
        name=name,
    )(a, *([w_all] * len(w_specs)), resid, g.reshape(1, d))


def _conv_gate(ug, ug1, ug2, uv, uv1, uv2, cwg, cwv, cbg, cbv):
    gate = cbg + cwg[0:1] * ug2 + cwg[1:2] * ug1 + cwg[2:3] * ug
    val = cbv + cwv[0:1] * uv2 + cwv[1:2] * uv1 + cwv[2:3] * uv
    return (gate * jax.nn.sigmoid(gate) * val).astype(BF16)


def _ffn_seq_kernel(x_ref, xh_ref, g_ref, wg_ref, wv_ref, cwg_ref, cwv_ref, cbg_ref, cbv_ref, wd_ref, gn_ref,
                    o_ref, sg_ref, sv_ref, h_ref, u_scr, *, tiles_per_seq, nj):
    halo = BF16_ROWS
    tm = x_ref.shape[0]
    j = pl.program_id(1)

    @pl.when(j == 0)
    def _():
        g = g_ref[...]
        keep = jnp.where(pl.program_id(0) % tiles_per_seq == 0, 0.0, 1.0)
        h_ref[0:halo] = (_rms(xh_ref[...], g) * keep).astype(BF16)
        h_ref[halo:] = _rms(x_ref[...], g).astype(BF16)
        o_ref[...] = jnp.zeros(o_ref.shape, F32)

    wg, wv, wd = wg_ref[...], wv_ref[...], wd_ref[...]
    cwg, cwv, cbg, cbv = cwg_ref[...], cwv_ref[...], cbg_ref[...], cbv_ref[...]
    rc = min(FFN_ROW_CHUNK, tm)
    for c in range(tm // rc):
        lo = 0 if c == 0 else halo + c * rc
        hi = halo + (c + 1) * rc
        hc = h_ref[lo:hi, :]
        u_scr[0, lo:hi, :] = jnp.dot(hc, wg, preferred_element_type=F32)
        u_scr[1, lo:hi, :] = jnp.dot(hc, wv, preferred_element_type=F32)
        base = halo + c * rc
        taps = [[u_scr[s, pl.ds(base - k, rc), :] for k in range(CONV_WIDTH)] for s in range(2)]
        act = _conv_gate(*taps[0], *taps[1], cwg, cwv, cbg, cbv)
        for n0 in range(0, wd.shape[1], MXU_COLS):
            o_ref[c * rc:(c + 1) * rc, n0:n0 + MXU_COLS] += jnp.dot(act, wd[:, n0:n0 + MXU_COLS],
                                                                   preferred_element_type=F32)
    sg_ref[...] = u_scr[0, halo + tm - SUBLANES:halo + tm, :]
    sv_ref[...] = u_scr[1, halo + tm - SUBLANES:halo + tm, :]

    @pl.when(j == nj - 1)
    def _():
        o_ref[...] = x_ref[...] + _rms(o_ref[...], gn_ref[...])


def _ffn_step_kernel(x_ref, g_ref, wg_ref, wv_ref, g1_ref, v1_ref, g2_ref, v2_ref,
                     cwg_ref, cwv_ref, cbg_ref, cbv_ref, wd_ref, gn_ref, o_ref, sg_ref, sv_ref, h_ref, *, nj):
    j = pl.program_id(1)

    @pl.when(j == 0)
    def _():
        h_ref[...] = _rms(x_ref[...], g_ref[...]).astype(BF16)
        o_ref[...] = jnp.zeros(o_ref.shape, F32)

    h = h_ref[...]
    ug = jnp.dot(h, wg_ref[...], preferred_element_type=F32)
    uv = jnp.dot(h, wv_ref[...], preferred_element_type=F32)
    sg_ref[...] = ug
    sv_ref[...] = uv
    act = _conv_gate(ug, g1_ref[...], g2_ref[...], uv, v1_ref[...], v2_ref[...],
                     cwg_ref[...], cwv_ref[...], cbg_ref[...], cbv_ref[...])
    o_ref[...] += jnp.dot(act, wd_ref[...], preferred_element_type=F32)

    @pl.when(j == nj - 1)
    def _():
        o_ref[...] = x_ref[...] + _rms(o_ref[...], gn_ref[...])


def conv_ffn(x, g_in, g_out, w_all, cw_all, cb_all, wd_all, layer, *, tm, tn, seq_len=None, prev_all=None):
    m, d = x.shape
    n = w_all.shape[3]
    assert m % tm == 0 and n % tn == 0
    nj = n // tn
    gv = lambda rows: [pl.BlockSpec((None, None, rows, tn), lambda i, j, s=s: (layer, s, 0, j)) for s in range(2)]
    w_specs = gv(d)
    c_specs = gv(CONV_WIDTH) + gv(1)
    c_args = [cw_all, cw_all, cb_all, cb_all]
    d_specs = [pl.BlockSpec((None, tn, d), lambda i, j: (layer, j, 0)), pl.BlockSpec((1, d), lambda i, j: (0, 0))]
    d_args = [wd_all, g_out.reshape(1, d)]
    g_spec = pl.BlockSpec((1, d), lambda i, j: (0, 0))
    tile = pl.BlockSpec((tm, tn), lambda i, j: (i, j))
    if prev_all is None:
        assert seq_len % tm == 0 and tm % BF16_ROWS == 0
        hb = tm // BF16_ROWS
        x_spec = pl.BlockSpec((tm, d), lambda i, j: (i, 0), pipeline_mode=pl.Buffered(1))
        halo_spec = pl.BlockSpec((BF16_ROWS, d), lambda i, j: (jnp.maximum(i * hb - 1, 0), 0))
        body = functools.partial(_ffn_seq_kernel, tiles_per_seq=seq_len // tm, nj=nj)
        in_specs = [x_spec, halo_spec, g_spec] + w_specs + c_specs + d_specs
        args = [x, x, g_in.reshape(1, d), w_all, w_all] + c_args + d_args
        st_rows, st_spec = (m // tm) * SUBLANES, pl.BlockSpec((SUBLANES, tn), lambda i, j: (i, j))
        scratch = [pltpu.VMEM((tm + BF16_ROWS, d), BF16), pltpu.VMEM((2, tm + BF16_ROWS, tn), F32)]
    else:
        x_spec = pl.BlockSpec((tm, d), lambda i, j: (i, 0))
        body = functools.partial(_ffn_step_kernel, nj=nj)
        prev_specs = [pl.BlockSpec((None, None, None, tm, tn), lambda i, j, r=r, s=s: (layer, r, s, i, j))
                      for r in (1, 0) for s in range(2)]
        in_specs = [x_spec, g_spec] + w_specs + prev_specs + c_specs + d_specs
        args = [x, g_in.reshape(1, d), w_all, w_all] + [prev_all] * 4 + c_args + d_args
        st_rows, st_spec = m, tile
        scratch = [pltpu.VMEM((tm, d), BF16)]
    st_shape = jax.ShapeDtypeStruct((st_rows, n), F32)
    return pl.pallas_call(
        body,
        out_shape=(jax.ShapeDtypeStruct((m, d), F32), st_shape, st_shape),
        grid=(m // tm, nj),
        in_specs=in_specs,
        out_specs=(pl.BlockSpec((tm, d), lambda i, j: (i, 0)), st_spec, st_spec),
        scratch_shapes=scratch,
        compiler_params=_cparams(("parallel", "arbitrary")),
        name="conv_ffn",
    )(*args)


def _ssm_params_kernel(lre_ref, lim_ref, ldt_ref, bre_ref, bim_ref, are_ref, aim_ref, bbre_ref, bbim_ref):
    lre = jnp.minimum(lre_ref[...], -1e-4)
    lim = lim_ref[...]
    dt = jnp.exp(ldt_ref[...])
    mag = jnp.exp(lre * dt)
    a_re = mag * jnp.cos(lim * dt)
    a_im = mag * jnp.sin(lim * dt)
    are_ref[...] = a_re
    aim_ref[...] = a_im
    x, y = a_re - 1.0, a_im
    den = lre * lre + lim * lim
    c_re = (x * lre + y * lim) / den
    c_im = (y * lre - x * lim) / den
    bre, bim = bre_ref[...], bim_ref[...]
    bbre_ref[...] = c_re * bre - c_im * bim
    bbim_ref[...] = c_re * bim + c_im * bre


def ssm_params(lam_re, lam_im, log_dt, b_re, b_im):
    g, p = lam_re.shape
    shp_a = jax.ShapeDtypeStruct((g, 1, p), F32)
    shp_b = jax.ShapeDtypeStruct((g, SSM_CH, p), F32)
    a_re, a_im, bb_re, bb_im = pl.pallas_call(
        _ssm_params_kernel,
        out_shape=(shp_a, shp_a, shp_b, shp_b),
        compiler_params=pltpu.CompilerParams(vmem_limit_bytes=VMEM_LIMIT),
        name="ssm_params",
    )(lam_re.reshape(g, 1, p), lam_im.reshape(g, 1, p), log_dt.reshape(g, 1, 1),
      jnp.swapaxes(b_re, 1, 2), jnp.swapaxes(b_im, 1, 2))
    return a_re.reshape(g, p), a_im.reshape(g, p), bb_re, bb_im


def _gelu_tanh(y):
    return 0.5 * y * (1.0 + jnp.tanh(math.sqrt(2.0 / math.pi) * (y + 0.044715 * (y * y * y))))


def _ssm_scan_kernel(u_ref, bm_ref, cm_ref, are_ref, aim_ref, d_ref, h0r_ref, h0i_ref,
                     z_ref, sr_ref, si_ref, buf_a, buf_b):
    nb, t = u_ref.shape[0], u_ref.shape[1]
    tc, pitch, half = SSM_CHUNK, SSM_PITCH, SSM_SLABS // 2
    n_chunks = t // tc
    assert n_chunks % 2 == 0 and n_chunks >= 4
    bm, cm = bm_ref[...], cm_ref[...]
    ar, ai = are_ref[...], aim_ref[...]
    dsk = d_ref[...]

    def chunk_rows(c):
        return pl.ds(pl.multiple_of(c * tc, tc), tc)

    def project_in(c, buf):
        for b in range(nb):
            r = jnp.dot(u_ref[b, chunk_rows(c), :].astype(BF16), bm, preferred_element_type=F32)
            for k in range(SSM_SLABS):
                buf[b, k * pitch:k * pitch + tc, :] = r[:, k * LANES:(k + 1) * LANES]

    def project_out(c, buf):
        for b in range(nb):
            xs = jnp.concatenate([buf[b, k * pitch:k * pitch + tc, :] for k in range(SSM_SLABS)], axis=-1)
            y = jnp.dot(xs.astype(BF16), cm, preferred_element_type=F32) + dsk * u_ref[b, chunk_rows(c), :]
            z_ref[b, chunk_rows(c), :] = _gelu_tanh(y).astype(z_ref.dtype)

    def recur(buf, carry):
        carry = list(carry)
        for step in range(tc):
            re_rows = pl.ds(step, half, stride=pitch)
            im_rows = pl.ds(half * pitch + step, half, stride=pitch)
            for b in range(nb):
                xr, xi = carry[b]
                nr = ar * xr - ai * xi + buf[b, re_rows, :]
                ni = ar * xi + ai * xr + buf[b, im_rows, :]
                buf[b, re_rows, :] = nr
                buf[b, im_rows, :] = ni
                carry[b] = (nr, ni)
        return tuple(carry)

    carry = tuple((h0r_ref[b], h0i_ref[b]) for b in range(nb))
    project_in(0, buf_a)
    carry = recur(buf_a, carry)
    project_in(1, buf_b)

    def pair(s, carry):
        c = 2 * s + 1
        carry = recur(buf_b, carry)
        project_out(c - 1, buf_a)
        project_in(c + 1, buf_a)
        carry = recur(buf_a, carry)
        project_out(c, buf_b)
        project_in(c + 2, buf_b)
        return carry

    carry = lax.fori_loop(0, n_chunks // 2 - 1, pair, carry)
    carry = recur(buf_b, carry)
    project_out(n_chunks - 2, buf_a)
    project_out(n_chunks - 1, buf_b)
    for b in range(nb):
        sr_ref[b] = carry[b][0]
        si_ref[b] = carry[b][1]


def ssm_scan(u, bmat, cmat, a_re, a_im, d_skip, h0_re, h0_im):
    b, t, d = u.shape
    assert t % SSM_CHUNK == 0
    half = SSM_SLABS // 2
    st_spec = pl.BlockSpec((b, None, half, LANES), lambda j: (0, j, 0, 0))
    st_shape = jax.ShapeDtypeStruct((b, SSM_TILES, half, LANES), F32)
    seq_spec = lambda **kw: pl.BlockSpec((b, t, SSM_TILE_CH), lambda j: (0, 0, j), **kw)
    buf = pltpu.VMEM((b, SSM_SLABS * SSM_PITCH, LANES), F32)
    return pl.pallas_call(
        _ssm_scan_kernel,
        out_shape=(jax.ShapeDtypeStruct((b, t, d), BF16), st_shape, st_shape),
        grid=(SSM_TILES,),
        in_specs=[
            seq_spec(pipeline_mode=pl.Buffered(1)),
            pl.BlockSpec((None, SSM_TILE_CH, 2 * SSM_TILE_ST), lambda j: (j, 0, 0)),
            pl.BlockSpec((None, 2 * SSM_TILE_ST, SSM_TILE_CH), lambda j: (j, 0, 0)),
            pl.BlockSpec((None, half, LANES), lambda j: (j, 0, 0)),
            pl.BlockSpec((None, half, LANES), lambda j: (j, 0, 0)),
            pl.BlockSpec((1, SSM_TILE_CH), lambda j: (0, j)),
            st_spec, st_spec,
        ],
        out_specs=(seq_spec(), st_spec, st_spec),
        scratch_shapes=[buf, buf],
        compiler_params=_cparams(("parallel",)),
        name="ssm_scan",
    )(u, bmat, cmat, a_re.reshape(SSM_TILES, half, LANES), a_im.reshape(SSM_TILES, half, LANES),
      d_skip.reshape(1, d), h0_re, h0_im)


def _ssm_step_kernel(u_ref, bm_ref, cm_ref, are_ref, aim_ref, d_ref, h0r_ref, h0i_ref, z_ref, sr_ref, si_ref):
    u = u_ref[...]
    bu = jnp.dot(u.astype(BF16), bm_ref[...], preferred_element_type=F32)
    ar, ai = are_ref[...], aim_ref[...]
    hr, hi = h0r_ref[...], h0i_ref[...]
    xr = ar * hr - ai * hi + bu[:, :SSM_TILE_ST]
    xi = ar * hi + ai * hr + bu[:, SSM_TILE_ST:]
    sr_ref[...] = xr
    si_ref[...] = xi
    xs = jnp.concatenate([xr, xi], axis=-1).astype(BF16)
    y = jnp.dot(xs, cm_ref[...], preferred_element_type=F32) + d_ref[...] * u
    z_ref[...] = _gelu_tanh(y).astype(z_ref.dtype)


def ssm_step(u, bmat, cmat, a_re, a_im, d_skip, h0_re, h0_im):
    b, d = u.shape
    st_spec = pl.BlockSpec((None, b, SSM_TILE_ST), lambda j: (j, 0, 0))
    a_spec = pl.BlockSpec((None, 1, SSM_TILE_ST), lambda j: (j, 0, 0))
    st_shape = jax.ShapeDtypeStruct((SSM_TILES, b, SSM_TILE_ST), F32)
    return pl.pallas_call(
        _ssm_step_kernel,
        out_shape=(jax.ShapeDtypeStruct((b, d), BF16), st_shape, st_shape),
        grid=(SSM_TILES,),
        in_specs=[
            pl.BlockSpec((b, SSM_TILE_CH), lambda j: (0, j)),
            pl.BlockSpec((None, SSM_TILE_CH, 2 * SSM_TILE_ST), lambda j: (j, 0, 0)),
            pl.BlockSpec((None, 2 * SSM_TILE_ST, SSM_TILE_CH), lambda j: (j, 0, 0)),
            a_spec, a_spec,
            pl.BlockSpec((1, SSM_TILE_CH), lambda j: (0, j)),
            st_spec, st_spec,
        ],
        out_specs=(pl.BlockSpec((b, SSM_TILE_CH), lambda j: (0, j)), st_spec, st_spec),
        compiler_params=_cparams(("parallel",)),
        name="ssm_step",
    )(u, bmat, cmat, a_re.reshape(SSM_TILES, 1, SSM_TILE_ST), a_im.reshape(SSM_TILES, 1, SSM_TILE_ST),
      d_skip.reshape(1, d), h0_re, h0_im)


def _block_diag_mats(bb_re, bb_im, c_re, c_im):
    ch_group = jnp.arange(SSM_TILE_CH) // SSM_CH
    st_group = jnp.arange(SSM_TILE_ST) // SSM_STATE
    same = ch_group[:, None] == st_group[None, :]

    def bmat(bb):
        rows = bb.reshape(SSM_TILES, SSM_TILE_CH, SSM_STATE)
        return jnp.where(same, jnp.tile(rows, (1, 1, SSM_TILE_G)), 0.0)

    def cmat(cc):
        cols = jnp.swapaxes(cc.reshape(SSM_TILES, SSM_TILE_CH, SSM_STATE), 1, 2)
        return jnp.where(same.T, jnp.tile(cols, (1, SSM_TILE_G, 1)), 0.0)

    b_all = jnp.concatenate([bmat(bb_re), bmat(bb_im)], axis=2).astype(BF16)
    c_all = jnp.concatenate([cmat(c_re), cmat(-c_im)], axis=1).astype(BF16)
    return b_all, c_all


def _cast_pad_kernel(x_ref, o_ref):
    r, c = x_ref.shape
    rr, cc = o_ref.shape
    o_ref[0:r, 0:c] = x_ref[...].astype(o_ref.dtype)
    if cc > c:
        o_ref[:, c:] = jnp.zeros((rr, cc - c), o_ref.dtype)
    if rr > r:
        o_ref[r:, 0:c] = jnp.zeros((rr - r, c), o_ref.dtype)


def _cast_call(x, out_shape, grid, in_spec, out_spec, name):
    return pl.pallas_call(
        _cast_pad_kernel,
        out_shape=jax.ShapeDtypeStruct(out_shape, BF16),
        grid=grid,
        in_specs=[in_spec],
        out_specs=out_spec,
        compiler_params=_cparams(("parallel",) * len(grid)),
        name=name,
    )(x)


def prep_w_qkv(w):
    n_layers, d, n = w.shape
    tr = 512

    def src_block(ob):
        kv = ob - N_DIL
        return jnp.where(ob < N_DIL, ob, N_DIL + (kv % 2) * N_DIL + kv // 2)

    return _cast_call(
        w, (n_layers, d, n), (n_layers, d // tr, n // ATTN_W),
        pl.BlockSpec((None, tr, ATTN_W), lambda l, r, ob: (l, r, src_block(ob))),
        pl.BlockSpec((None, tr, ATTN_W), lambda l, r, ob: (l, r, ob)), "prep_w_qkv")


def prep_w_up(w):
    n_layers, d, _ = w.shape
    tr = 256
    return _cast_call(
        w, (n_layers, 2, d, D_FF_PAD), (n_layers, 2, d // tr),
        pl.BlockSpec((None, tr, D_FF), lambda l, s, r: (l, r, s)),
        pl.BlockSpec((None, None, tr, D_FF_PAD), lambda l, s, r: (l, s, r, 0)), "prep_w_up")


def prep_w_down(w):
    n_layers, _, d = w.shape
    tc = 256
    return _cast_call(
        w, (n_layers, D_FF_PAD, d), (n_layers, d // tc),
        pl.BlockSpec((None, D_FF, tc), lambda l, c: (l, 0, c)),
        pl.BlockSpec((None, D_FF_PAD, tc), lambda l, c: (l, 0, c)), "prep_w_down")


def _ff_split(a, axis):
    a = a.reshape(a.shape[:-1] + (2, D_FF))
    a = jnp.pad(a, [(0, 0)] * (a.ndim - 1) + [(0, D_FF_PAD - D_FF)])
    return jnp.moveaxis(a, -2, axis)


def _ff_join(gate, val):
    return jnp.concatenate([gate[..., :D_FF], val[..., :D_FF]], axis=-1)


def kernel(x_prompt, x_sample, cache_kv_g0, cache_kv_g1, cache_kv_g2, state_ssm, state_conv, norm_g, w_qkv,
           w_attn_o, w_ssm_in, lambda_re, lambda_im, log_dt, b_re, b_im, c_re, c_im, d_skip, w_glu, w_up,
           conv_w, conv_b, w_down):
    bp, t, d = x_prompt.shape
    bs = x_sample.shape[0]
    mp = bp * t
    xp = x_prompt.reshape(mp, d)
    xs = x_sample.reshape(bs, d)
    caches = (cache_kv_g0, cache_kv_g1, cache_kv_g2)
    tm_big, tm = 1024, 512

    wq_all = prep_w_qkv(w_qkv)
    wo_all = w_attn_o.astype(BF16)
    win_all = w_ssm_in.astype(BF16)
    wglu_all = w_glu.astype(BF16)
    wu_all = prep_w_up(w_up)
    wd_all = prep_w_down(w_down)
    cw_all = _ff_split(conv_w, 1)
    cb_all = _ff_split(conv_b.reshape(DEPTH, 1, 2 * D_FF), 1)
    prev_all = jnp.transpose(_ff_split(state_conv, 3), (0, 2, 3, 1, 4))

    rope_p = rope_tables(t, 0)
    rope_s = rope_tables(SUBLANES, PAST_LEN)
    rope_s = tuple(jnp.broadcast_to(r[0:1], (bs, LANES)) for r in rope_s)

    kv_wide = None
    kv_p = [[] for _ in range(N_DIL)]
    kv_s = [[] for _ in range(N_DIL)]
    ssm_p, ssm_s, conv_p, conv_s = [], [], [], []

    for i in range(DEPTH):
        li = i // N_MIXERS
        if i % N_MIXERS == 0:
            assert DIL_GROUPS[-1][0] >= t
            qkv_p, kv_wide = qkv_project(xp, norm_g[i, 0], wq_all, li, rope_p, tm=tm_big,
                                         stack=(w_qkv.shape[0], kv_wide))
            qkv_s = qkv_project(xs, norm_g[i, 0], wq_all, li, rope_s, tm=bs)
            op = attn_prompt(qkv_p.reshape(bp, t, QKV_COLS))
            os_ = attn_sample(qkv_s, caches, li)
            xp = matmul_residual(op.reshape(mp, ATTN_W), wo_all, li, xp, norm_g[i, 1], tm=tm, tn=d, name="attn_out")
            xs = matmul_residual(os_, wo_all, li, xs, norm_g[i, 1], tm=bs, tn=d, name="attn_out")
            for g, (win, _) in enumerate(DIL_GROUPS):
                keep = min(win, t)
                c0 = _kv_col(g)
                if g < N_DIL - 1:
                    kv_p[g].append(qkv_p.reshape(bp, t, QKV_COLS)[:, t - keep:, c0:c0 + KV_COLS]
                                   .reshape(bp, keep, 2, HEADS, HEAD_DIM))
                kv_s[g].append(qkv_s[:, c0:c0 + KV_COLS].reshape(bs, 1, 2, HEADS, HEAD_DIM))
        else:
            a_re, a_im, bb_re, bb_im = ssm_params(lambda_re[li], lambda_im[li], log_dt[li], b_re[li], b_im[li])
            bmat, cmat = _block_diag_mats(bb_re, bb_im, c_re[li], c_im[li])
            half = SSM_SLABS // 2
            up_ = norm_matmul(xp, norm_g[i, 0], win_all, li, tm=tm_big, tn=1024, name="ssm_in")
            us_ = norm_matmul(xs, norm_g[i, 0], win_all, li, tm=bs, tn=1024, name="ssm_in")
            zeros = jnp.zeros((bp, SSM_TILES, half, LANES), F32)
            zp, sp_re, sp_im = ssm_scan(up_.reshape(bp, t, d), bmat, cmat, a_re, a_im, d_skip[li], zeros, zeros)
            h0 = state_ssm[li].reshape(bs, SSM_TILES, SSM_TILE_ST, 2)
            h0_re = jnp.swapaxes(h0[..., 0], 0, 1)
            h0_im = jnp.swapaxes(h0[..., 1], 0, 1)
            zs, ss_re, ss_im = ssm_step(us_, bmat, cmat, a_re, a_im, d_skip[li], h0_re, h0_im)
            xp = matmul_residual(zp.reshape(mp, d), wglu_all, li, xp, norm_g[i, 1], tm=tm, tn=512, gated=True,
                                 name="ssm_out")
            xs = matmul_residual(zs, wglu_all, li, xs, norm_g[i, 1], tm=bs, tn=512, gated=True, name="ssm_out")
            ssm_p.append(jnp.stack([sp_re.reshape(bp, SSM_GROUPS, SSM_STATE),
                                    sp_im.reshape(bp, SSM_GROUPS, SSM_STATE)], axis=-1))
            ssm_s.append(jnp.stack([jnp.swapaxes(ss_re, 0, 1).reshape(bs, SSM_GROUPS, SSM_STATE),
                                    jnp.swapaxes(ss_im, 0, 1).reshape(bs, SSM_GROUPS, SSM_STATE)], axis=-1))

        xp, sg_p, sv_p = conv_ffn(xp, norm_g[i, 2], norm_g[i, 3], wu_all, cw_all, cb_all, wd_all, i,
                                  tm=tm_big, tn=256, seq_len=t)
        xs, sg_s, sv_s = conv_ffn(xs, norm_g[i, 2], norm_g[i, 3], wu_all, cw_all, cb_all, wd_all, i,
                                  tm=bs, tn=512, prev_all=prev_all)
        tiles = t // tm_big
        tail = _ff_join(sg_p, sv_p).reshape(bp, tiles, SUBLANES, 2 * D_FF)
        conv_p.append(tail[:, tiles - 1, SUBLANES - (CONV_WIDTH - 1):])
        conv_s.append(jnp.concatenate([state_conv[i][:, 1:], _ff_join(sg_s, sv_s)[:, None]], axis=1))

    return (xp.reshape(bp, t, d), xs.reshape(bs, 1, d),
            jnp.stack(kv_p[0]), jnp.stack(kv_s[0]),
            jnp.stack(kv_p[1]), jnp.stack(kv_s[1]),
            kv_wide.reshape(w_qkv.shape[0], bp, t, 2, HEADS, HEAD_DIM), jnp.stack(kv_s[2]),
            jnp.stack(ssm_p), jnp.stack(ssm_s),
            jnp.stack(conv_p), jnp.stack(conv_s))
```

```python
import functools
import math

import jax
import jax.numpy as jnp
from jax import lax
from jax.experimental import pallas as pl
from jax.experimental.pallas import tpu as pltpu

F32 = jnp.float32
BF16 = jnp.bfloat16

D_MODEL = 2048
DEPTH = 4
PAST_LEN = 16384
N_MIXERS = 2
HEAD_DIM = 128
HEADS = D_MODEL // (2 * HEAD_DIM)
DIL_GROUPS = ((128, 1), (512, 4), (2048, 16))
N_DIL = len(DIL_GROUPS)
N_BACK = 128
ROT_DIM = HEAD_DIM // 4
ROPE_THETA = 500000.0
NEG_INF = -1e30
SSM_CH = 16
SSM_GROUPS = D_MODEL // SSM_CH
SSM_STATE = 64
D_FF = ((8 * D_MODEL // 3 + 127) // 128) * 128
CONV_WIDTH = 3
RMS_EPS = 1e-6

ATTN_W = HEADS * HEAD_DIM
Q_COLS = N_DIL * ATTN_W
KV_COLS = 2 * ATTN_W
QKV_COLS = 3 * N_DIL * ATTN_W

LANES = 128
SUBLANES = 8
BF16_ROWS = 16
MXU_COLS = 256
VMEM_LIMIT = 52 * 1024 * 1024
ROW_CHUNK = 256
FFN_ROW_CHUNK = 512

FF_ALIGN = 512
D_FF_PAD = ((D_FF + FF_ALIGN - 1) // FF_ALIGN) * FF_ALIGN

SSM_TILE_G = 16
SSM_TILES = SSM_GROUPS // SSM_TILE_G
SSM_TILE_CH = SSM_TILE_G * SSM_CH
SSM_TILE_ST = SSM_TILE_G * SSM_STATE
SSM_SLABS = 2 * SSM_TILE_ST // LANES
SSM_CHUNK = 256
SSM_PITCH = SSM_CHUNK + SUBLANES


def _cparams(sem):
    return pltpu.CompilerParams(dimension_semantics=sem, vmem_limit_bytes=VMEM_LIMIT)


def _rms(x, g):
    ms = jnp.mean(x * x, axis=-1, keepdims=True)
    return x * lax.rsqrt(ms + RMS_EPS) * g


def _rope_table_kernel(inv_ref, cos_ref, sa_ref, sb_ref, *, pos_base):
    rows = cos_ref.shape[0]
    pos = (lax.broadcasted_iota(jnp.int32, (rows, LANES), 0) + pos_base).astype(F32)
    lane = lax.broadcasted_iota(jnp.int32, (rows, LANES), 1)
    ang = pos * inv_ref[...]
    c = jnp.cos(ang)
    s = jnp.sin(ang)
    half = ROT_DIM // 2
    cos_ref[...] = jnp.where(lane < ROT_DIM, c, 1.0)
    sa_ref[...] = jnp.where(lane < half, -s, 0.0)
    sb_ref[...] = jnp.where((lane >= half) & (lane < ROT_DIM), s, 0.0)


def rope_tables(rows, pos_base):
    half = ROT_DIM // 2
    inv = ROPE_THETA ** (-(jnp.arange(half, dtype=F32) * (2.0 / ROT_DIM)))
    inv_lane = jnp.tile(inv, LANES // half).reshape(1, LANES)
    shp = jax.ShapeDtypeStruct((rows, LANES), F32)
    return pl.pallas_call(
        functools.partial(_rope_table_kernel, pos_base=pos_base),
        out_shape=(shp, shp, shp),
        name="rope_tables",
    )(inv_lane)


QKV_TN = 512
Q_TILES = Q_COLS // QKV_TN
KV_TILES = KV_COLS // QKV_TN
assert KV_TILES & (KV_TILES - 1) == 0


def _qkv_kernel(x_ref, g_ref, w_ref, cos_ref, sa_ref, sb_ref, *refs, stacked):
    if stacked:
        _, o_ref, s_ref, h_ref = refs
    else:
        (o_ref, h_ref), s_ref = refs, None
    j = pl.program_id(1)

    @pl.when(j == 0)
    def _():
        h_ref[...] = _rms(x_ref[...], g_ref[...]).astype(BF16)

    is_rope = (j < Q_TILES) | (((j - Q_TILES) & (KV_TILES - 1)) < KV_TILES // 2)
    w = w_ref[...]
    half = ROT_DIM // 2
    tm = x_ref.shape[0]
    rc = min(ROW_CHUNK, tm)
    for c in range(tm // rc):
        rows = slice(c * rc, (c + 1) * rc)
        y = jnp.dot(h_ref[rows, :], w, preferred_element_type=F32)
        cs, sa, sb = cos_ref[rows, :], sa_ref[rows, :], sb_ref[rows, :]
        for h in range(QKV_TN // HEAD_DIM):
            xh = y[:, h * HEAD_DIM:(h + 1) * HEAD_DIM]
            up = pltpu.roll(xh, HEAD_DIM - half, axis=1)
            dn = pltpu.roll(xh, half, axis=1)
            out = jnp.where(is_rope, xh * cs + up * sa + dn * sb, xh)
            o_ref[rows, h * HEAD_DIM:(h + 1) * HEAD_DIM] = out
            if s_ref is not None:
                s_ref[rows, h * HEAD_DIM:(h + 1) * HEAD_DIM] = out


def qkv_project(x, g, w_all, layer, rope, *, tm, stack=None):
    m, d = x.shape
    cos, sa, sb = rope
    tab_tiles = cos.shape[0] // tm
    assert m % tm == 0 and cos.shape[0] % tm == 0 and w_all.shape[2] == QKV_COLS
    tab_spec = pl.BlockSpec((tm, LANES), lambda i, j: (i % tab_tiles, 0))
    in_specs = [
        pl.BlockSpec((tm, d), lambda i, j: (i, 0)),
        pl.BlockSpec((1, d), lambda i, j: (0, 0)),
        pl.BlockSpec((None, d, QKV_TN), lambda i, j: (layer, 0, j)),
        tab_spec, tab_spec, tab_spec,
    ]
    args = [x, g.reshape(1, d), w_all, cos, sa, sb]
    out_shape = jax.ShapeDtypeStruct((m, QKV_COLS), F32)
    out_specs = pl.BlockSpec((tm, QKV_TN), lambda i, j: (i, j))
    aliases = {}
    if stack is not None:
        n_layers, earlier = stack
        base = QKV_COLS // QKV_TN - KV_TILES
        out_shape = (out_shape, jax.ShapeDtypeStruct((n_layers, m, KV_COLS), F32))
        out_specs = (out_specs, pl.BlockSpec((None, tm, QKV_TN), lambda i, j: (layer, i, jnp.maximum(j - base, 0))))
        assert earlier.shape == (n_layers, m, KV_COLS)
        aliases = {len(args): 1}
        in_specs.append(pl.BlockSpec(memory_space=pl.ANY))
        args.append(earlier)
    return pl.pallas_call(
        functools.partial(_qkv_kernel, stacked=stack is not None),
        out_shape=out_shape,
        grid=(m // tm, QKV_COLS // QKV_TN),
        in_specs=in_specs,
        out_specs=out_specs,
        scratch_shapes=[pltpu.VMEM((tm, d), BF16)],
        input_output_aliases=aliases,
        compiler_params=_cparams(("parallel", "arbitrary")),
        name="qkv_project",
    )(*args)


def _permute_qkv_cols(w):
    parts = [w[..., :Q_COLS]]
    for g in range(N_DIL):
        parts.append(w[..., Q_COLS + g * ATTN_W:Q_COLS + (g + 1) * ATTN_W])
        parts.append(w[..., 2 * Q_COLS + g * ATTN_W:2 * Q_COLS + (g + 1) * ATTN_W])
    return jnp.concatenate(parts, axis=-1)


def _kv_col(g):
    return Q_COLS + g * KV_COLS


def _norm_mm_kernel(x_ref, g_ref, w_ref, o_ref, h_ref):
    @pl.when(pl.program_id(1) == 0)
    def _():
        h_ref[...] = _rms(x_ref[...], g_ref[...]).astype(BF16)

    o_ref[...] = jnp.dot(h_ref[...], w_ref[...], preferred_element_type=F32).astype(o_ref.dtype)


def norm_matmul(x, g, w_all, layer, *, tm, tn, out_dtype=F32, name="norm_matmul"):
    m, d = x.shape
    n = w_all.shape[2]
    assert m % tm == 0 and n % tn == 0
    return pl.pallas_call(
        _norm_mm_kernel,
        out_shape=jax.ShapeDtypeStruct((m, n), out_dtype),
        grid=(m // tm, n // tn),
        in_specs=[
            pl.BlockSpec((tm, d), lambda i, j: (i, 0)),
            pl.BlockSpec((1, d), lambda i, j: (0, 0)),
            pl.BlockSpec((None, d, tn), lambda i, j: (layer, 0, j)),
        ],
        out_specs=pl.BlockSpec((tm, tn), lambda i, j: (i, j)),
        scratch_shapes=[pltpu.VMEM((tm, d), BF16)],
        compiler_params=_cparams(("parallel", "arbitrary")),
        name=name,
    )(x, g.reshape(1, d), w_all)


ATTN_BLK = 128
assert ATTN_BLK == N_BACK


def _residue_major(ref, dil):
    t = ref.shape[0]
    if dil == 1:
        return ref[...]
    return jnp.concatenate([ref[pl.ds(r, t // dil, stride=dil), :] for r in range(dil)], axis=0)


def _store_token_order(ref, g, dil, val):
    nb = val.shape[0]
    if dil == 1:
        ref[g] = val.reshape(nb * ATTN_BLK, LANES)
        return
    per = nb // dil
    for r in range(dil):
        ref[g, pl.ds(r, per * ATTN_BLK, stride=dil), :] = val[r * per:(r + 1) * per].reshape(per * ATTN_BLK, LANES)


def _attn_prompt_kernel(q0, q1, q2, k0, k1, k2, v0, v1, v2, o_ref, o_scr, l_scr, d_scr):
    qs, ks, vs = (q0, q1, q2), (k0, k1, k2), (v0, v1, v2)
    t = q0.shape[0]
    nb = t // ATTN_BLK
    blk = ATTN_BLK

    for g, (_, dil) in enumerate(DIL_GROUPS):
        per = nb // dil
        to_blocks = lambda a: a.reshape(nb, blk, HEAD_DIM)
        qb = to_blocks((_residue_major(qs[g], dil) * (HEAD_DIM ** -0.5)).astype(BF16))
        kb = to_blocks(_residue_major(ks[g], dil).astype(BF16))
        vb = to_blocks(_residue_major(vs[g], dil).astype(BF16))
        if per > 1:
            kk = jnp.concatenate([jnp.concatenate([kb[:1], kb[:-1]], axis=0), kb], axis=1)
            vv = jnp.concatenate([jnp.concatenate([vb[:1], vb[:-1]], axis=0), vb], axis=1)
            nk = 2 * blk
            bi = lax.broadcasted_iota(jnp.int32, (nb, 1, nk), 0)
            ci = lax.broadcasted_iota(jnp.int32, (nb, 1, nk), 2)
            no_prev = jnp.where(((bi & (per - 1)) == 0) & (ci < blk), NEG_INF, 0.0)
        else:
            kk, vv, nk = kb, vb, blk
            no_prev = None
        rel = (lax.broadcasted_iota(jnp.int32, (blk, nk), 0) + (nk - blk)
               - lax.broadcasted_iota(jnp.int32, (blk, nk), 1))
        band = jnp.where((rel >= 0) & (rel <= N_BACK), 0.0, NEG_INF)

        s = jnp.einsum('bqd,bkd->bqk', qb, kk, preferred_element_type=F32) + band[None]
        if no_prev is not None:
            s = s + no_prev
        m = jnp.max(s, axis=-1, keepdims=True)
        p = jnp.exp(s - m)
        l = jnp.sum(p, axis=-1, keepdims=True)
        o = jnp.einsum('bqk,bkd->bqd', p.astype(BF16), vv, preferred_element_type=F32)

        @pl.when(pl.program_id(1) == 0)
        def _(g=g, l=l):
            d_scr[g] = jnp.broadcast_to(l, (nb, blk, LANES))

        _store_token_order(o_scr, g, dil, o / d_scr[g])
        _store_token_order(l_scr, g, dil, jnp.broadcast_to(m + jnp.log(l), (nb, blk, LANES)))

    l0, l1, l2 = l_scr[0], l_scr[1], l_scr[2]
    m = jnp.maximum(jnp.maximum(l0, l1), l2)
    e0, e1, e2 = jnp.exp(l0 - m), jnp.exp(l1 - m), jnp.exp(l2 - m)
    num = e0 * o_scr[0] + e1 * o_scr[1] + e2 * o_scr[2]
    o_ref[...] = (num / (e0 + e1 + e2)).astype(o_ref.dtype)


def attn_prompt(qkv):
    b, t, _ = qkv.shape
    assert t % (ATTN_BLK * DIL_GROUPS[-1][1]) == 0

    def spec(col0):
        return pl.BlockSpec((None, t, HEAD_DIM), lambda bi, h: (bi, 0, col0 // HEAD_DIM + h))

    in_specs = ([spec(g * ATTN_W) for g in range(N_DIL)]
                + [spec(_kv_col(g)) for g in range(N_DIL)]
                + [spec(_kv_col(g) + ATTN_W) for g in range(N_DIL)])
    nb = t // ATTN_BLK
    return pl.pallas_call(
        _attn_prompt_kernel,
        out_shape=jax.ShapeDtypeStruct((b, t, ATTN_W), BF16),
        grid=(b, HEADS),
        in_specs=in_specs,
        out_specs=pl.BlockSpec((None, t, HEAD_DIM), lambda bi, h: (bi, 0, h)),
        scratch_shapes=[pltpu.VMEM((N_DIL, t, LANES), F32), pltpu.VMEM((N_DIL, t, LANES), F32),
                        pltpu.VMEM((N_DIL, nb, ATTN_BLK, LANES), F32)],
        compiler_params=_cparams(("parallel", "arbitrary")),
        name="attn_prompt",
    )(*([qkv] * 9))


def _attn_sample_kernel(x_ref, c0, c1, c2, o_ref):
    caches = (c0, c1, c2)
    outs, lses = [], []
    for g in range(N_DIL):
        r0 = _kv_col(g) // HEAD_DIM
        q = x_ref[g * HEADS:(g + 1) * HEADS, :] * (HEAD_DIM ** -0.5)
        k_new = x_ref[r0:r0 + HEADS, :]
        v_new = x_ref[r0 + HEADS:r0 + 2 * HEADS, :]
        kc = caches[g][:, 0:HEADS, :]
        vc = caches[g][:, HEADS:2 * HEADS, :]
        s = jnp.sum(kc * q[None], axis=-1, keepdims=True)
        s_new = jnp.sum(q * k_new, axis=-1, keepdims=True)
        m = jnp.maximum(jnp.max(s, axis=0), s_new)
        p = jnp.exp(s - m[None])
        p_new = jnp.exp(s_new - m)
        l = jnp.sum(p, axis=0) + p_new
        o = jnp.sum(p * vc, axis=0) + p_new * v_new
        outs.append(o / l[0:1])
        lses.append(m + jnp.log(l))
    m = jnp.maximum(jnp.maximum(lses[0], lses[1]), lses[2])
    es = [jnp.exp(x - m) for x in lses]
    num = es[0] * outs[0] + es[1] * outs[1] + es[2] * outs[2]
    o_ref[...] = num / (es[0] + es[1] + es[2])


def attn_sample(qkv, caches, layer):
    b = qkv.shape[0]
    rows = QKV_COLS // HEAD_DIM
    in_specs = [pl.BlockSpec((None, rows, HEAD_DIM), lambda bi: (bi, 0, 0))]
    args = [qkv.reshape(b, rows, HEAD_DIM)]
    for (win, dil), cache in zip(DIL_GROUPS, caches):
        n_layers, _, length = cache.shape[:3]
        assert length == win and length // dil == N_BACK
        args.append(cache.reshape(n_layers, b, N_BACK, dil * 2 * HEADS, HEAD_DIM))
        in_specs.append(pl.BlockSpec((None, None, N_BACK, 2 * HEADS, HEAD_DIM), lambda bi: (layer, bi, 0, 0, 0)))
    out = pl.pallas_call(
        _attn_sample_kernel,
        out_shape=jax.ShapeDtypeStruct((b, HEADS, HEAD_DIM), F32),
        grid=(b,),
        in_specs=in_specs,
        out_specs=pl.BlockSpec((None, HEADS, HEAD_DIM), lambda bi: (bi, 0, 0)),
        compiler_params=_cparams(("parallel",)),
        name="attn_sample",
    )(*args)
    return out.reshape(b, ATTN_W).astype(BF16)


def _mm_res_kernel(a_ref, *refs, nj, gated):
    if gated:
        wv_ref, wg_ref, r_ref, g_ref, o_ref, y_scr = refs
    else:
        wv_ref, r_ref, g_ref, o_ref, y_scr = refs
    j = pl.program_id(1)
    a = a_ref[...]
    y = jnp.dot(a, wv_ref[...], preferred_element_type=F32)
    if gated:
        y = y * jax.nn.sigmoid(jnp.dot(a, wg_ref[...], preferred_element_type=F32))
    if nj == 1:
        o_ref[...] = r_ref[...] + _rms(y, g_ref[...])
        return
    y_scr[j] = y

    @pl.when(j == nj - 1)
    def _():
        full = jnp.concatenate([y_scr[t] for t in range(nj)], axis=-1)
        o_ref[...] = r_ref[...] + _rms(full, g_ref[...])


def matmul_residual(a, w_all, layer, resid, g, *, tm, tn, gated=False, name="matmul_residual"):
    m, k = a.shape
    d = resid.shape[1]
    assert m % tm == 0 and d % tn == 0 and w_all.shape[2] == (2 * d if gated else d)
    nj = d // tn
    w_specs = [pl.BlockSpec((None, k, tn), lambda i, j: (layer, 0, j))]
    if gated:
        w_specs.append(pl.BlockSpec((None, k, tn), lambda i, j: (layer, 0, nj + j)))
    scr_shape = (nj, tm, tn) if nj > 1 else (1, SUBLANES, LANES)
    return pl.pallas_call(
        functools.partial(_mm_res_kernel, nj=nj, gated=gated),
        out_shape=jax.ShapeDtypeStruct((m, d), F32),
        grid=(m // tm, nj),
        in_specs=[pl.BlockSpec((tm, k), lambda i, j: (i, 0))] + w_specs + [
            pl.BlockSpec((tm, d), lambda i, j: (i, 0)),
            pl.BlockSpec((1, d), lambda i, j: (0, 0)),
        ],
        out_specs=pl.BlockSpec((tm, d), lambda i, j: (i, 0)),
        scratch_shapes=[pltpu.VMEM(scr_shape, F32)],
        compiler_params=_cparams(("parallel", "arbitrary")),
        name=name,
    )(a, *([w_all] * len(w_specs)), resid, g.reshape(1, d))


def _conv_gate(ug, ug1, ug2, uv, uv1, uv2, cwg, cwv, cbg, cbv):
    gate = cbg + cwg[0:1] * ug2 + cwg[1:2] * ug1 + cwg[2:3] * ug
    val = cbv + cwv[0:1] * uv2 + cwv[1:2] * uv1 + cwv[2:3] * uv
    return (gate * jax.nn.sigmoid(gate) * val).astype(BF16)


def _ffn_seq_kernel(x_ref, xh_ref, g_ref, wg_ref, wv_ref, cwg_ref, cwv_ref, cbg_ref, cbv_ref, wd_ref, gn_ref,
                    o_ref, sg_ref, sv_ref, h_ref, u_scr, *, tiles_per_seq, nj):
    halo = BF16_ROWS
    tm = x_ref.shape[0]
    j = pl.program_id(1)

    @pl.when(j == 0)
    def _():
        g = g_ref[...]
        keep = jnp.where(pl.program_id(0) % tiles_per_seq == 0, 0.0, 1.0)
        h_ref[0:halo] = (_rms(xh_ref[...], g) * keep).astype(BF16)
        h_ref[halo:] = _rms(x_ref[...], g).astype(BF16)
        o_ref[...] = jnp.zeros(o_ref.shape, F32)

    wg, wv, wd = wg_ref[...], wv_ref[...], wd_ref[...]
    cwg, cwv, cbg, cbv = cwg_ref[...], cwv_ref[...], cbg_ref[...], cbv_ref[...]
    rc = min(FFN_ROW_CHUNK, tm)
    for c in range(tm // rc):
        lo = 0 if c == 0 else halo + c * rc
        hi = halo + (c + 1) * rc
        hc = h_ref[lo:hi, :]
        u_scr[0, lo:hi, :] = jnp.dot(hc, wg, preferred_element_type=F32)
        u_scr[1, lo:hi, :] = jnp.dot(hc, wv, preferred_element_type=F32)
        base = halo + c * rc
        taps = [[u_scr[s, pl.ds(base - k, rc), :] for k in range(CONV_WIDTH)] for s in range(2)]
        act = _conv_gate(*taps[0], *taps[1], cwg, cwv, cbg, cbv)
        for n0 in range(0, wd.shape[1], MXU_COLS):
            o_ref[c * rc:(c + 1) * rc, n0:n0 + MXU_COLS] += jnp.dot(act, wd[:, n0:n0 + MXU_COLS],
                                                                   preferred_element_type=F32)
    sg_ref[...] = u_scr[0, halo + tm - SUBLANES:halo + tm, :]
    sv_ref[...] = u_scr[1, halo + tm - SUBLANES:halo + tm, :]

    @pl.when(j == nj - 1)
    def _():
        o_ref[...] = x_ref[...] + _rms(o_ref[...], gn_ref[...])


def _ffn_step_kernel(x_ref, g_ref, wg_ref, wv_ref, g1_ref, v1_ref, g2_ref, v2_ref,
                     cwg_ref, cwv_ref, cbg_ref, cbv_ref, wd_ref, gn_ref, o_ref, sg_ref, sv_ref, h_ref, *, nj):
    j = pl.program_id(1)

    @pl.when(j == 0)
    def _():
        h_ref[...] = _rms(x_ref[...], g_ref[...]).astype(BF16)
        o_ref[...] = jnp.zeros(o_ref.shape, F32)

    h = h_ref[...]
    ug = jnp.dot(h, wg_ref[...], preferred_element_type=F32)
    uv = jnp.dot(h, wv_ref[...], preferred_element_type=F32)
    sg_ref[...] = ug
    sv_ref[...] = uv
    act = _conv_gate(ug, g1_ref[...], g2_ref[...], uv, v1_ref[...], v2_ref[...],
                     cwg_ref[...], cwv_ref[...], cbg_ref[...], cbv_ref[...])
    o_ref[...] += jnp.dot(act, wd_ref[...], preferred_element_type=F32)

    @pl.when(j == nj - 1)
    def _():
        o_ref[...] = x_ref[...] + _rms(o_ref[...], gn_ref[...])


def conv_ffn(x, g_in, g_out, w_all, cw_all, cb_all, wd_all, layer, *, tm, tn, seq_len=None, prev_all=None):
    m, d = x.shape
    n = w_all.shape[3]
    assert m % tm == 0 and n % tn == 0
    nj = n // tn
    gv = lambda rows: [pl.BlockSpec((None, None, rows, tn), lambda i, j, s=s: (layer, s, 0, j)) for s in range(2)]
    w_specs = gv(d)
    c_specs = gv(CONV_WIDTH) + gv(1)
    c_args = [cw_all, cw_all, cb_all, cb_all]
    d_specs = [pl.BlockSpec((None, tn, d), lambda i, j: (layer, j, 0)), pl.BlockSpec((1, d), lambda i, j: (0, 0))]
    d_args = [wd_all, g_out.reshape(1, d)]
    g_spec = pl.BlockSpec((1, d), lambda i, j: (0, 0))
    tile = pl.BlockSpec((tm, tn), lambda i, j: (i, j))
    if prev_all is None:
        assert seq_len % tm == 0 and tm % BF16_ROWS == 0
        hb = tm // BF16_ROWS
        x_spec = pl.BlockSpec((tm, d), lambda i, j: (i, 0), pipeline_mode=pl.Buffered(1))
        halo_spec = pl.BlockSpec((BF16_ROWS, d), lambda i, j: (jnp.maximum(i * hb - 1, 0), 0))
        body = functools.partial(_ffn_seq_kernel, tiles_per_seq=seq_len // tm, nj=nj)
        in_specs = [x_spec, halo_spec, g_spec] + w_specs + c_specs + d_specs
        args = [x, x, g_in.reshape(1, d), w_all, w_all] + c_args + d_args
        st_rows, st_spec = (m // tm) * SUBLANES, pl.BlockSpec((SUBLANES, tn), lambda i, j: (i, j))
        scratch = [pltpu.VMEM((tm + BF16_ROWS, d), BF16), pltpu.VMEM((2, tm + BF16_ROWS, tn), F32)]
    else:
        x_spec = pl.BlockSpec((tm, d), lambda i, j: (i, 0))
        body = functools.partial(_ffn_step_kernel, nj=nj)
        prev_specs = [pl.BlockSpec((None, None, None, tm, tn), lambda i, j, r=r, s=s: (layer, r, s, i, j))
                      for r in (1, 0) for s in range(2)]
        in_specs = [x_spec, g_spec] + w_specs + prev_specs + c_specs + d_specs
        args = [x, g_in.reshape(1, d), w_all, w_all] + [prev_all] * 4 + c_args + d_args
        st_rows, st_spec = m, tile
        scratch = [pltpu.VMEM((tm, d), BF16)]
    st_shape = jax.ShapeDtypeStruct((st_rows, n), F32)
    return pl.pallas_call(
        body,
        out_shape=(jax.ShapeDtypeStruct((m, d), F32), st_shape, st_shape),
        grid=(m // tm, nj),
        in_specs=in_specs,
        out_specs=(pl.BlockSpec((tm, d), lambda i, j: (i, 0)), st_spec, st_spec),
        scratch_shapes=scratch,
        compiler_params=_cparams(("parallel", "arbitrary")),
        name="conv_ffn",
    )(*args)


def _ssm_params_kernel(lre_ref, lim_ref, ldt_ref, bre_ref, bim_ref, are_ref, aim_ref, bbre_ref, bbim_ref):
    lre = jnp.minimum(lre_ref[...], -1e-4)
    lim = lim_ref[...]
    dt = jnp.exp(ldt_ref[...])
    mag = jnp.exp(lre * dt)
    a_re = mag * jnp.cos(lim * dt)
    a_im = mag * jnp.sin(lim * dt)
    are_ref[...] = a_re
    aim_ref[...] = a_im
    x, y = a_re - 1.0, a_im
    den = lre * lre + lim * lim
    c_re = (x * lre + y * lim) / den
    c_im = (y * lre - x * lim) / den
    bre, bim = bre_ref[...], bim_ref[...]
    bbre_ref[...] = c_re * bre - c_im * bim
    bbim_ref[...] = c_re * bim + c_im * bre


def ssm_params(lam_re, lam_im, log_dt, b_re, b_im):
    g, p = lam_re.shape
    shp_a = jax.ShapeDtypeStruct((g, 1, p), F32)
    shp_b = jax.ShapeDtypeStruct((g, SSM_CH, p), F32)
    a_re, a_im, bb_re, bb_im = pl.pallas_call(
        _ssm_params_kernel,
        out_shape=(shp_a, shp_a, shp_b, shp_b),
        compiler_params=pltpu.CompilerParams(vmem_limit_bytes=VMEM_LIMIT),
        name="ssm_params",
    )(lam_re.reshape(g, 1, p), lam_im.reshape(g, 1, p), log_dt.reshape(g, 1, 1),
      jnp.swapaxes(b_re, 1, 2), jnp.swapaxes(b_im, 1, 2))
    return a_re.reshape(g, p), a_im.reshape(g, p), bb_re, bb_im


def _gelu_tanh(y):
    return 0.5 * y * (1.0 + jnp.tanh(math.sqrt(2.0 / math.pi) * (y + 0.044715 * (y * y * y))))


def _ssm_scan_kernel(u_ref, bm_ref, cm_ref, are_ref, aim_ref, d_ref, h0r_ref, h0i_ref,
                     z_ref, sr_ref, si_ref, buf_a, buf_b):
    nb, t = u_ref.shape[0], u_ref.shape[1]
    tc, pitch, half = SSM_CHUNK, SSM_PITCH, SSM_SLABS // 2
    n_chunks = t // tc
    assert n_chunks % 2 == 0 and n_chunks >= 4
    bm, cm = bm_ref[...], cm_ref[...]
    ar, ai = are_ref[...], aim_ref[...]
    dsk = d_ref[...]

    def chunk_rows(c):
        return pl.ds(pl.multiple_of(c * tc, tc), tc)

    def project_in(c, buf):
        for b in range(nb):
            r = jnp.dot(u_ref[b, chunk_rows(c), :].astype(BF16), bm, preferred_element_type=F32)
            for k in range(SSM_SLABS):
                buf[b, k * pitch:k * pitch + tc, :] = r[:, k * LANES:(k + 1) * LANES]

    def project_out(c, buf):
        for b in range(nb):
            xs = jnp.concatenate([buf[b, k * pitch:k * pitch + tc, :] for k in range(SSM_SLABS)], axis=-1)
            y = jnp.dot(xs.astype(BF16), cm, preferred_element_type=F32) + dsk * u_ref[b, chunk_rows(c), :]
            z_ref[b, chunk_rows(c), :] = _gelu_tanh(y).astype(z_ref.dtype)

    def recur(buf, carry):
        carry = list(carry)
        for step in range(tc):
            re_rows = pl.ds(step, half, stride=pitch)
            im_rows = pl.ds(half * pitch + step, half, stride=pitch)
            for b in range(nb):
                xr, xi = carry[b]
                nr = ar * xr - ai * xi + buf[b, re_rows, :]
                ni = ar * xi + ai * xr + buf[b, im_rows, :]
                buf[b, re_rows, :] = nr
                buf[b, im_rows, :] = ni
                carry[b] = (nr, ni)
        return tuple(carry)

    carry = tuple((h0r_ref[b], h0i_ref[b]) for b in range(nb))
    project_in(0, buf_a)
    carry = recur(buf_a, carry)
    project_in(1, buf_b)

    def pair(s, carry):
        c = 2 * s + 1
        carry = recur(buf_b, carry)
        project_out(c - 1, buf_a)
        project_in(c + 1, buf_a)
        carry = recur(buf_a, carry)
        project_out(c, buf_b)
        project_in(c + 2, buf_b)
        return carry

    carry = lax.fori_loop(0, n_chunks // 2 - 1, pair, carry)
    carry = recur(buf_b, carry)
    project_out(n_chunks - 2, buf_a)
    project_out(n_chunks - 1, buf_b)
    for b in range(nb):
        sr_ref[b] = carry[b][0]
        si_ref[b] = carry[b][1]


def ssm_scan(u, bmat, cmat, a_re, a_im, d_skip, h0_re, h0_im):
    b, t, d = u.shape
    assert t % SSM_CHUNK == 0
    half = SSM_SLABS // 2
    st_spec = pl.BlockSpec((b, None, half, LANES), lambda j: (0, j, 0, 0))
    st_shape = jax.ShapeDtypeStruct((b, SSM_TILES, half, LANES), F32)
    seq_spec = lambda **kw: pl.BlockSpec((b, t, SSM_TILE_CH), lambda j: (0, 0, j), **kw)
    buf = pltpu.VMEM((b, SSM_SLABS * SSM_PITCH, LANES), F32)
    return pl.pallas_call(
        _ssm_scan_kernel,
        out_shape=(jax.ShapeDtypeStruct((b, t, d), BF16), st_shape, st_shape),
        grid=(SSM_TILES,),
        in_specs=[
            seq_spec(pipeline_mode=pl.Buffered(1)),
            pl.BlockSpec((None, SSM_TILE_CH, 2 * SSM_TILE_ST), lambda j: (j, 0, 0)),
            pl.BlockSpec((None, 2 * SSM_TILE_ST, SSM_TILE_CH), lambda j: (j, 0, 0)),
            pl.BlockSpec((None, half, LANES), lambda j: (j, 0, 0)),
            pl.BlockSpec((None, half, LANES), lambda j: (j, 0, 0)),
            pl.BlockSpec((1, SSM_TILE_CH), lambda j: (0, j)),
            st_spec, st_spec,
        ],
        out_specs=(seq_spec(), st_spec, st_spec),
        scratch_shapes=[buf, buf],
        compiler_params=_cparams(("parallel",)),
        name="ssm_scan",
    )(u, bmat, cmat, a_re.reshape(SSM_TILES, half, LANES), a_im.reshape(SSM_TILES, half, LANES),
      d_skip.reshape(1, d), h0_re, h0_im)


def _ssm_step_kernel(u_ref, bm_ref, cm_ref, are_ref, aim_ref, d_ref, h0r_ref, h0i_ref, z_ref, sr_ref, si_ref):
    u = u_ref[...]
    bu = jnp.dot(u.astype(BF16), bm_ref[...], preferred_element_type=F32)
    ar, ai = are_ref[...], aim_ref[...]
    hr, hi = h0r_ref[...], h0i_ref[...]
    xr = ar * hr - ai * hi + bu[:, :SSM_TILE_ST]
    xi = ar * hi + ai * hr + bu[:, SSM_TILE_ST:]
    sr_ref[...] = xr
    si_ref[...] = xi
    xs = jnp.concatenate([xr, xi], axis=-1).astype(BF16)
    y = jnp.dot(xs, cm_ref[...], preferred_element_type=F32) + d_ref[...] * u
    z_ref[...] = _gelu_tanh(y).astype(z_ref.dtype)


def ssm_step(u, bmat, cmat, a_re, a_im, d_skip, h0_re, h0_im):
    b, d = u.shape
    st_spec = pl.BlockSpec((None, b, SSM_TILE_ST), lambda j: (j, 0, 0))
    a_spec = pl.BlockSpec((None, 1, SSM_TILE_ST), lambda j: (j, 0, 0))
    st_shape = jax.ShapeDtypeStruct((SSM_TILES, b, SSM_TILE_ST), F32)
    return pl.pallas_call(
        _ssm_step_kernel,
        out_shape=(jax.ShapeDtypeStruct((b, d), BF16), st_shape, st_shape),
        grid=(SSM_TILES,),
        in_specs=[
            pl.BlockSpec((b, SSM_TILE_CH), lambda j: (0, j)),
            pl.BlockSpec((None, SSM_TILE_CH, 2 * SSM_TILE_ST), lambda j: (j, 0, 0)),
            pl.BlockSpec((None, 2 * SSM_TILE_ST, SSM_TILE_CH), lambda j: (j, 0, 0)),
            a_spec, a_spec,
            pl.BlockSpec((1, SSM_TILE_CH), lambda j: (0, j)),
            st_spec, st_spec,
        ],
        out_specs=(pl.BlockSpec((b, SSM_TILE_CH), lambda j: (0, j)), st_spec, st_spec),
        compiler_params=_cparams(("parallel",)),
        name="ssm_step",
    )(u, bmat, cmat, a_re.reshape(SSM_TILES, 1, SSM_TILE_ST), a_im.reshape(SSM_TILES, 1, SSM_TILE_ST),
      d_skip.reshape(1, d), h0_re, h0_im)


def _block_diag_mats(bb_re, bb_im, c_re, c_im):
    ch_group = jnp.arange(SSM_TILE_CH) // SSM_CH
    st_group = jnp.arange(SSM_TILE_ST) // SSM_STATE
    same = ch_group[:, None] == st_group[None, :]

    def bmat(bb):
        rows = bb.reshape(SSM_TILES, SSM_TILE_CH, SSM_STATE)
        return jnp.where(same, jnp.tile(rows, (1, 1, SSM_TILE_G)), 0.0)

    def cmat(cc):
        cols = jnp.swapaxes(cc.reshape(SSM_TILES, SSM_TILE_CH, SSM_STATE), 1, 2)
        return jnp.where(same.T, jnp.tile(cols, (1, SSM_TILE_G, 1)), 0.0)

    b_all = jnp.concatenate([bmat(bb_re), bmat(bb_im)], axis=2).astype(BF16)
    c_all = jnp.concatenate([cmat(c_re), cmat(-c_im)], axis=1).astype(BF16)
    return b_all, c_all


def _cast_pad_kernel(x_ref, o_ref):
    r, c = x_ref.shape
    rr, cc = o_ref.shape
    o_ref[0:r, 0:c] = x_ref[...].astype(o_ref.dtype)
    if cc > c:
        o_ref[:, c:] = jnp.zeros((rr, cc - c), o_ref.dtype)
    if rr > r:
        o_ref[r:, 0:c] = jnp.zeros((rr - r, c), o_ref.dtype)


def _cast_call(x, out_shape, grid, in_spec, out_spec, name):
    return pl.pallas_call(
        _cast_pad_kernel,
        out_shape=jax.ShapeDtypeStruct(out_shape, BF16),
        grid=grid,
        in_specs=[in_spec],
        out_specs=out_spec,
        compiler_params=_cparams(("parallel",) * len(grid)),
        name=name,
    )(x)


def prep_w_qkv(w):
    n_layers, d, n = w.shape
    tr = 512

    def src_block(ob):
        kv = ob - N_DIL
        return jnp.where(ob < N_DIL, ob, N_DIL + (kv % 2) * N_DIL + kv // 2)

    return _cast_call(
        w, (n_layers, d, n), (n_layers, d // tr, n // ATTN_W),
        pl.BlockSpec((None, tr, ATTN_W), lambda l, r, ob: (l, r, src_block(ob))),
        pl.BlockSpec((None, tr, ATTN_W), lambda l, r, ob: (l, r, ob)), "prep_w_qkv")


def prep_w_up(w):
    n_layers, d, _ = w.shape
    tr = 256
    return _cast_call(
        w, (n_layers, 2, d, D_FF_PAD), (n_layers, 2, d // tr),
        pl.BlockSpec((None, tr, D_FF), lambda l, s, r: (l, r, s)),
        pl.BlockSpec((None, None, tr, D_FF_PAD), lambda l, s, r: (l, s, r, 0)), "prep_w_up")


def prep_w_down(w):
    n_layers, _, d = w.shape
    tc = 256
    return _cast_call(
        w, (n_layers, D_FF_PAD, d), (n_layers, d // tc),
        pl.BlockSpec((None, D_FF, tc), lambda l, c: (l, 0, c)),
        pl.BlockSpec((None, D_FF_PAD, tc), lambda l, c: (l, 0, c)), "prep_w_down")


def _ff_split(a, axis):
    a = a.reshape(a.shape[:-1] + (2, D_FF))
    a = jnp.pad(a, [(0, 0)] * (a.ndim - 1) + [(0, D_FF_PAD - D_FF)])
    return jnp.moveaxis(a, -2, axis)


def _ff_join(gate, val):
    return jnp.concatenate([gate[..., :D_FF], val[..., :D_FF]], axis=-1)


def kernel(x_prompt, x_sample, cache_kv_g0, cache_kv_g1, cache_kv_g2, state_ssm, state_conv, norm_g, w_qkv,
           w_attn_o, w_ssm_in, lambda_re, lambda_im, log_dt, b_re, b_im, c_re, c_im, d_skip, w_glu, w_up,
           conv_w, conv_b, w_down):
    bp, t, d = x_prompt.shape
    bs = x_sample.shape[0]
    mp = bp * t
    xp = x_prompt.reshape(mp, d)
    xs = x_sample.reshape(bs, d)
    caches = (cache_kv_g0, cache_kv_g1, cache_kv_g2)
    tm_big, tm = 1024, 512
    tm_ffn, tn_ffn = 512, 512

    wq_all = prep_w_qkv(w_qkv)
    wo_all = w_attn_o.astype(BF16)
    win_all = w_ssm_in.astype(BF16)
    wglu_all = w_glu.astype(BF16)
    wu_all = prep_w_up(w_up)
    wd_all = prep_w_down(w_down)
    cw_all = _ff_split(conv_w, 1)
    cb_all = _ff_split(conv_b.reshape(DEPTH, 1, 2 * D_FF), 1)
    prev_all = jnp.transpose(_ff_split(state_conv, 3), (0, 2, 3, 1, 4))

    rope_p = rope_tables(t, 0)
    rope_s = rope_tables(SUBLANES, PAST_LEN)
    rope_s = tuple(jnp.broadcast_to(r[0:1], (bs, LANES)) for r in rope_s)

    kv_wide = jnp.zeros((w_qkv.shape[0], mp, KV_COLS), F32)
    kv_p = [[] for _ in range(N_DIL)]
    kv_s = [[] for _ in range(N_DIL)]
    ssm_p, ssm_s, conv_p, conv_s = [], [], [], []

    for i in range(DEPTH):
        li = i // N_MIXERS
        if i % N_MIXERS == 0:
            assert DIL_GROUPS[-1][0] >= t
            qkv_p, kv_wide = qkv_project(xp, norm_g[i, 0], wq_all, li, rope_p, tm=tm_big,
                                         stack=(w_qkv.shape[0], kv_wide))
            qkv_s = qkv_project(xs, norm_g[i, 0], wq_all, li, rope_s, tm=bs)
            op = attn_prompt(qkv_p.reshape(bp, t, QKV_COLS))
            os_ = attn_sample(qkv_s, caches, li)
            xp = matmul_residual(op.reshape(mp, ATTN_W), wo_all, li, xp, norm_g[i, 1], tm=tm, tn=d, name="attn_out")
            xs = matmul_residual(os_, wo_all, li, xs, norm_g[i, 1], tm=bs, tn=d, name="attn_out")
            for g, (win, _) in enumerate(DIL_GROUPS):
                keep = min(win, t)
                c0 = _kv_col(g)
                if g < N_DIL - 1:
                    kv_p[g].append(qkv_p.reshape(bp, t, QKV_COLS)[:, t - keep:, c0:c0 + KV_COLS]
                                   .reshape(bp, keep, 2, HEADS, HEAD_DIM))
                kv_s[g].append(qkv_s[:, c0:c0 + KV_COLS].reshape(bs, 1, 2, HEADS, HEAD_DIM))
        else:
            a_re, a_im, bb_re, bb_im = ssm_params(lambda_re[li], lambda_im[li], log_dt[li], b_re[li], b_im[li])
            bmat, cmat = _block_diag_mats(bb_re, bb_im, c_re[li], c_im[li])
            half = SSM_SLABS // 2
            up_ = norm_matmul(xp, norm_g[i, 0], win_all, li, tm=tm_big, tn=1024, name="ssm_in")
            us_ = norm_matmul(xs, norm_g[i, 0], win_all, li, tm=bs, tn=1024, name="ssm_in")
            zeros = jnp.zeros((bp, SSM_TILES, half, LANES), F32)
            zp, sp_re, sp_im = ssm_scan(up_.reshape(bp, t, d), bmat, cmat, a_re, a_im, d_skip[li], zeros, zeros)
            h0 = state_ssm[li].reshape(bs, SSM_TILES, SSM_TILE_ST, 2)
            h0_re = jnp.swapaxes(h0[..., 0], 0, 1)
            h0_im = jnp.swapaxes(h0[..., 1], 0, 1)
            zs, ss_re, ss_im = ssm_step(us_, bmat, cmat, a_re, a_im, d_skip[li], h0_re, h0_im)
            xp = matmul_residual(zp.reshape(mp, d), wglu_all, li, xp, norm_g[i, 1], tm=tm, tn=512, gated=True,
                                 name="ssm_out")
            xs = matmul_residual(zs, wglu_all, li, xs, norm_g[i, 1], tm=bs, tn=512, gated=True, name="ssm_out")
            ssm_p.append(jnp.stack([sp_re.reshape(bp, SSM_GROUPS, SSM_STATE),
                                    sp_im.reshape(bp, SSM_GROUPS, SSM_STATE)], axis=-1))
            ssm_s.append(jnp.stack([jnp.swapaxes(ss_re, 0, 1).reshape(bs, SSM_GROUPS, SSM_STATE),
                                    jnp.swapaxes(ss_im, 0, 1).reshape(bs, SSM_GROUPS, SSM_STATE)], axis=-1))

        xp, sg_p, sv_p = conv_ffn(xp, norm_g[i, 2], norm_g[i, 3], wu_all, cw_all, cb_all, wd_all, i,
                                  tm=tm_ffn, tn=tn_ffn, seq_len=t)
        xs, sg_s, sv_s = conv_ffn(xs, norm_g[i, 2], norm_g[i, 3], wu_all, cw_all, cb_all, wd_all, i,
                                  tm=bs, tn=512, prev_all=prev_all)
        tiles = t // tm_ffn
        tail = _ff_join(sg_p, sv_p).reshape(bp, tiles, SUBLANES, 2 * D_FF)
        conv_p.append(tail[:, tiles - 1, SUBLANES - (CONV_WIDTH - 1):])
        conv_s.append(jnp.concatenate([state_conv[i][:, 1:], _ff_join(sg_s, sv_s)[:, None]], axis=1))

    return (xp.reshape(bp, t, d), xs.reshape(bs, 1, d),
            jnp.stack(kv_p[0]), jnp.stack(kv_s[0]),
            jnp.stack(kv_p[1]), jnp.stack(kv_s[1]),
            kv_wide.reshape(w_qkv.shape[0], bp, t, 2, HEADS, HEAD_DIM), jnp.stack(kv_s[2]),
            jnp.stack(ssm_p), jnp.stack(ssm_s),
            jnp.stack(conv_p), jnp.stack(conv_s))
```

```python
import functools
import math

import jax
import jax.numpy as jnp
from jax import lax
from jax.experimental import pallas as pl
from jax.experimental.pallas import tpu as pltpu

F32 = jnp.float32
BF16 = jnp.bfloat16

D_MODEL = 2048
DEPTH = 4
PAST_LEN = 16384
N_MIXERS = 2
HEAD_DIM = 128
HEADS = D_MODEL // (2 * HEAD_DIM)
DIL_GROUPS = ((128, 1), (512, 4), (2048, 16))
N_DIL = len(DIL_GROUPS)
N_BACK = 128
ROT_DIM = HEAD_DIM // 4
ROPE_THETA = 500000.0
NEG_INF = -1e30
SSM_CH = 16
SSM_GROUPS = D_MODEL // SSM_CH
SSM_STATE = 64
D_FF = ((8 * D_MODEL // 3 + 127) // 128) * 128
CONV_WIDTH = 3
RMS_EPS = 1e-6

ATTN_W = HEADS * HEAD_DIM
Q_COLS = N_DIL * ATTN_W
KV_COLS = 2 * ATTN_W
QKV_COLS = 3 * N_DIL * ATTN_W

LANES = 128
SUBLANES = 8
BF16_ROWS = 16
MXU_COLS = 256
VMEM_LIMIT = 52 * 1024 * 1024
ROW_CHUNK = 256
FFN_ROW_CHUNK = 512

FF_ALIGN = 512
D_FF_PAD = ((D_FF + FF_ALIGN - 1) // FF_ALIGN) * FF_ALIGN

SSM_TILE_G = 16
SSM_TILES = SSM_GROUPS // SSM_TILE_G
SSM_TILE_CH = SSM_TILE_G * SSM_CH
SSM_TILE_ST = SSM_TILE_G * SSM_STATE
SSM_SLABS = 2 * SSM_TILE_ST // LANES
SSM_CHUNK = 256
SSM_PITCH = SSM_CHUNK + SUBLANES


def _cparams(sem):
    return pltpu.CompilerParams(dimension_semantics=sem, vmem_limit_bytes=VMEM_LIMIT)


def _rms(x, g):
    ms = jnp.mean(x * x, axis=-1, keepdims=True)
    return x * lax.rsqrt(ms + RMS_EPS) * g


def _rope_table_kernel(inv_ref, cos_ref, sa_ref, sb_ref, *, pos_base):
    rows = cos_ref.shape[0]
    pos = (lax.broadcasted_iota(jnp.int32, (rows, LANES), 0) + pos_base).astype(F32)
    lane = lax.broadcasted_iota(jnp.int32, (rows, LANES), 1)
    ang = pos * inv_ref[...]
    c = jnp.cos(ang)
    s = jnp.sin(ang)
    half = ROT_DIM // 2
    cos_ref[...] = jnp.where(lane < ROT_DIM, c, 1.0)
    sa_ref[...] = jnp.where(lane < half, -s, 0.0)
    sb_ref[...] = jnp.where((lane >= half) & (lane < ROT_DIM), s, 0.0)


def rope_tables(rows, pos_base):
    half = ROT_DIM // 2
    inv = ROPE_THETA ** (-(jnp.arange(half, dtype=F32) * (2.0 / ROT_DIM)))
    inv_lane = jnp.tile(inv, LANES // half).reshape(1, LANES)
    shp = jax.ShapeDtypeStruct((rows, LANES), F32)
    return pl.pallas_call(
        functools.partial(_rope_table_kernel, pos_base=pos_base),
        out_shape=(shp, shp, shp),
        name="rope_tables",
    )(inv_lane)


QKV_TN = 512
Q_TILES = Q_COLS // QKV_TN
KV_TILES = KV_COLS // QKV_TN
assert KV_TILES & (KV_TILES - 1) == 0


def _qkv_kernel(x_ref, g_ref, w_ref, cos_ref, sa_ref, sb_ref, *refs, stacked):
    if stacked:
        o_ref, s_ref, h_ref = refs[-3:]
    else:
        (o_ref, h_ref), s_ref = refs, None
    j = pl.program_id(1)

    @pl.when(j == 0)
    def _():
        h_ref[...] = _rms(x_ref[...], g_ref[...]).astype(BF16)

    is_rope = (j < Q_TILES) | (((j - Q_TILES) & (KV_TILES - 1)) < KV_TILES // 2)
    w = w_ref[...]
    half = ROT_DIM // 2
    tm = x_ref.shape[0]
    rc = min(ROW_CHUNK, tm)
    for c in range(tm // rc):
        rows = slice(c * rc, (c + 1) * rc)
        y = jnp.dot(h_ref[rows, :], w, preferred_element_type=F32)
        cs, sa, sb = cos_ref[rows, :], sa_ref[rows, :], sb_ref[rows, :]
        for h in range(QKV_TN // HEAD_DIM):
            xh = y[:, h * HEAD_DIM:(h + 1) * HEAD_DIM]
            up = pltpu.roll(xh, HEAD_DIM - half, axis=1)
            dn = pltpu.roll(xh, half, axis=1)
            out = jnp.where(is_rope, xh * cs + up * sa + dn * sb, xh)
            o_ref[rows, h * HEAD_DIM:(h + 1) * HEAD_DIM] = out
            if s_ref is not None:
                cols = slice(h * HEAD_DIM, (h + 1) * HEAD_DIM)
                if len(s_ref.shape) == 3:
                    for l in range(s_ref.shape[0]):
                        s_ref[l, rows, cols] = out
                else:
                    s_ref[rows, cols] = out


def qkv_project(x, g, w_all, layer, rope, *, tm, stack=None):
    m, d = x.shape
    cos, sa, sb = rope
    tab_tiles = cos.shape[0] // tm
    assert m % tm == 0 and cos.shape[0] % tm == 0 and w_all.shape[2] == QKV_COLS
    tab_spec = pl.BlockSpec((tm, LANES), lambda i, j: (i % tab_tiles, 0))
    in_specs = [
        pl.BlockSpec((tm, d), lambda i, j: (i, 0)),
        pl.BlockSpec((1, d), lambda i, j: (0, 0)),
        pl.BlockSpec((None, d, QKV_TN), lambda i, j: (layer, 0, j)),
        tab_spec, tab_spec, tab_spec,
    ]
    args = [x, g.reshape(1, d), w_all, cos, sa, sb]
    out_shape = jax.ShapeDtypeStruct((m, QKV_COLS), F32)
    out_specs = pl.BlockSpec((tm, QKV_TN), lambda i, j: (i, j))
    aliases = {}
    if stack is not None:
        n_layers, earlier = stack
        base = QKV_COLS // QKV_TN - KV_TILES
        out_shape = (out_shape, jax.ShapeDtypeStruct((n_layers, m, KV_COLS), F32))
        if earlier is None:
            out_specs = (out_specs, pl.BlockSpec((n_layers, tm, QKV_TN), lambda i, j: (0, i, jnp.maximum(j - base, 0))))
        else:
            out_specs = (out_specs, pl.BlockSpec((None, tm, QKV_TN), lambda i, j: (layer, i, jnp.maximum(j - base, 0))))
            assert earlier.shape == (n_layers, m, KV_COLS)
            aliases = {len(args): 1}
            in_specs.append(pl.BlockSpec(memory_space=pl.ANY))
            args.append(earlier)
    return pl.pallas_call(
        functools.partial(_qkv_kernel, stacked=stack is not None),
        out_shape=out_shape,
        grid=(m // tm, QKV_COLS // QKV_TN),
        in_specs=in_specs,
        out_specs=out_specs,
        scratch_shapes=[pltpu.VMEM((tm, d), BF16)],
        input_output_aliases=aliases,
        compiler_params=_cparams(("parallel", "arbitrary")),
        name="qkv_project",
    )(*args)


def _permute_qkv_cols(w):
    parts = [w[..., :Q_COLS]]
    for g in range(N_DIL):
        parts.append(w[..., Q_COLS + g * ATTN_W:Q_COLS + (g + 1) * ATTN_W])
        parts.append(w[..., 2 * Q_COLS + g * ATTN_W:2 * Q_COLS + (g + 1) * ATTN_W])
    return jnp.concatenate(parts, axis=-1)


def _kv_col(g):
    return Q_COLS + g * KV_COLS


def _norm_mm_kernel(x_ref, g_ref, w_ref, o_ref, h_ref):
    @pl.when(pl.program_id(1) == 0)
    def _():
        h_ref[...] = _rms(x_ref[...], g_ref[...]).astype(BF16)

    o_ref[...] = jnp.dot(h_ref[...], w_ref[...], preferred_element_type=F32).astype(o_ref.dtype)


def norm_matmul(x, g, w_all, layer, *, tm, tn, out_dtype=F32, name="norm_matmul"):
    m, d = x.shape
    n = w_all.shape[2]
    assert m % tm == 0 and n % tn == 0
    return pl.pallas_call(
        _norm_mm_kernel,
        out_shape=jax.ShapeDtypeStruct((m, n), out_dtype),
        grid=(m // tm, n // tn),
        in_specs=[
            pl.BlockSpec((tm, d), lambda i, j: (i, 0)),
            pl.BlockSpec((1, d), lambda i, j: (0, 0)),
            pl.BlockSpec((None, d, tn), lambda i, j: (layer, 0, j)),
        ],
        out_specs=pl.BlockSpec((tm, tn), lambda i, j: (i, j)),
        scratch_shapes=[pltpu.VMEM((tm, d), BF16)],
        compiler_params=_cparams(("parallel", "arbitrary")),
        name=name,
    )(x, g.reshape(1, d), w_all)


ATTN_BLK = 128
assert ATTN_BLK == N_BACK


def _residue_major(ref, dil):
    t = ref.shape[0]
    if dil == 1:
        return ref[...]
    return jnp.concatenate([ref[pl.ds(r, t // dil, stride=dil), :] for r in range(dil)], axis=0)


def _store_token_order(ref, g, dil, val):
    nb = val.shape[0]
    if dil == 1:
        ref[g] = val.reshape(nb * ATTN_BLK, LANES)
        return
    per = nb // dil
    for r in range(dil):
        ref[g, pl.ds(r, per * ATTN_BLK, stride=dil), :] = val[r * per:(r + 1) * per].reshape(per * ATTN_BLK, LANES)


def _attn_prompt_kernel(q0, q1, q2, k0, k1, k2, v0, v1, v2, o_ref, o_scr, l_scr, d_scr):
    qs, ks, vs = (q0, q1, q2), (k0, k1, k2), (v0, v1, v2)
    t = q0.shape[0]
    nb = t // ATTN_BLK
    blk = ATTN_BLK

    for g, (_, dil) in enumerate(DIL_GROUPS):
        per = nb // dil
        to_blocks = lambda a: a.reshape(nb, blk, HEAD_DIM)
        qb = to_blocks((_residue_major(qs[g], dil) * (HEAD_DIM ** -0.5)).astype(BF16))
        kb = to_blocks(_residue_major(ks[g], dil).astype(BF16))
        vb = to_blocks(_residue_major(vs[g], dil).astype(BF16))
        if per > 1:
            kk = jnp.concatenate([jnp.concatenate([kb[:1], kb[:-1]], axis=0), kb], axis=1)
            vv = jnp.concatenate([jnp.concatenate([vb[:1], vb[:-1]], axis=0), vb], axis=1)
            nk = 2 * blk
            bi = lax.broadcasted_iota(jnp.int32, (nb, 1, nk), 0)
            ci = lax.broadcasted_iota(jnp.int32, (nb, 1, nk), 2)
            no_prev = jnp.where(((bi & (per - 1)) == 0) & (ci < blk), NEG_INF, 0.0)
        else:
            kk, vv, nk = kb, vb, blk
            no_prev = None
        rel = (lax.broadcasted_iota(jnp.int32, (blk, nk), 0) + (nk - blk)
               - lax.broadcasted_iota(jnp.int32, (blk, nk), 1))
        band = jnp.where((rel >= 0) & (rel <= N_BACK), 0.0, NEG_INF)

        s = jnp.einsum('bqd,bkd->bqk', qb, kk, preferred_element_type=F32) + band[None]
        if no_prev is not None:
            s = s + no_prev
        m = jnp.max(s, axis=-1, keepdims=True)
        p = jnp.exp(s - m)
        l = jnp.sum(p, axis=-1, keepdims=True)
        o = jnp.einsum('bqk,bkd->bqd', p.astype(BF16), vv, preferred_element_type=F32)

        @pl.when(pl.program_id(1) == 0)
        def _(g=g, l=l):
            d_scr[g] = jnp.broadcast_to(l, (nb, blk, LANES))

        _store_token_order(o_scr, g, dil, o / d_scr[g])
        _store_token_order(l_scr, g, dil, jnp.broadcast_to(m + jnp.log(l), (nb, blk, LANES)))

    l0, l1, l2 = l_scr[0], l_scr[1], l_scr[2]
    m = jnp.maximum(jnp.maximum(l0, l1), l2)
    e0, e1, e2 = jnp.exp(l0 - m), jnp.exp(l1 - m), jnp.exp(l2 - m)
    num = e0 * o_scr[0] + e1 * o_scr[1] + e2 * o_scr[2]
    o_ref[...] = (num / (e0 + e1 + e2)).astype(o_ref.dtype)


def attn_prompt(qkv):
    b, t, _ = qkv.shape
    assert t % (ATTN_BLK * DIL_GROUPS[-1][1]) == 0

    def spec(col0):
        return pl.BlockSpec((None, t, HEAD_DIM), lambda bi, h: (bi, 0, col0 // HEAD_DIM + h))

    in_specs = ([spec(g * ATTN_W) for g in range(N_DIL)]
                + [spec(_kv_col(g)) for g in range(N_DIL)]
                + [spec(_kv_col(g) + ATTN_W) for g in range(N_DIL)])
    nb = t // ATTN_BLK
    return pl.pallas_call(
        _attn_prompt_kernel,
        out_shape=jax.ShapeDtypeStruct((b, t, ATTN_W), BF16),
        grid=(b, HEADS),
        in_specs=in_specs,
        out_specs=pl.BlockSpec((None, t, HEAD_DIM), lambda bi, h: (bi, 0, h)),
        scratch_shapes=[pltpu.VMEM((N_DIL, t, LANES), F32), pltpu.VMEM((N_DIL, t, LANES), F32),
                        pltpu.VMEM((N_DIL, nb, ATTN_BLK, LANES), F32)],
        compiler_params=_cparams(("parallel", "arbitrary")),
        name="attn_prompt",
    )(*([qkv] * 9))


def _attn_sample_kernel(x_ref, c0, c1, c2, o_ref):
    caches = (c0, c1, c2)
    outs, lses = [], []
    for g in range(N_DIL):
        r0 = _kv_col(g) // HEAD_DIM
        q = x_ref[g * HEADS:(g + 1) * HEADS, :] * (HEAD_DIM ** -0.5)
        k_new = x_ref[r0:r0 + HEADS, :]
        v_new = x_ref[r0 + HEADS:r0 + 2 * HEADS, :]
        kc = caches[g][:, 0:HEADS, :]
        vc = caches[g][:, HEADS:2 * HEADS, :]
        s = jnp.sum(kc * q[None], axis=-1, keepdims=True)
        s_new = jnp.sum(q * k_new, axis=-1, keepdims=True)
        m = jnp.maximum(jnp.max(s, axis=0), s_new)
        p = jnp.exp(s - m[None])
        p_new = jnp.exp(s_new - m)
        l = jnp.sum(p, axis=0) + p_new
        o = jnp.sum(p * vc, axis=0) + p_new * v_new
        outs.append(o / l[0:1])
        lses.append(m + jnp.log(l))
    m = jnp.maximum(jnp.maximum(lses[0], lses[1]), lses[2])
    es = [jnp.exp(x - m) for x in lses]
    num = es[0] * outs[0] + es[1] * outs[1] + es[2] * outs[2]
    o_ref[...] = num / (es[0] + es[1] + es[2])


def attn_sample(qkv, caches, layer):
    b = qkv.shape[0]
    rows = QKV_COLS // HEAD_DIM
    in_specs = [pl.BlockSpec((None, rows, HEAD_DIM), lambda bi: (bi, 0, 0))]
    args = [qkv.reshape(b, rows, HEAD_DIM)]
    for (win, dil), cache in zip(DIL_GROUPS, caches):
        n_layers, _, length = cache.shape[:3]
        assert length == win and length // dil == N_BACK
        args.append(cache.reshape(n_layers, b, N_BACK, dil * 2 * HEADS, HEAD_DIM))
        in_specs.append(pl.BlockSpec((None, None, N_BACK, 2 * HEADS, HEAD_DIM), lambda bi: (layer, bi, 0, 0, 0)))
    out = pl.pallas_call(
        _attn_sample_kernel,
        out_shape=jax.ShapeDtypeStruct((b, HEADS, HEAD_DIM), F32),
        grid=(b,),
        in_specs=in_specs,
        out_specs=pl.BlockSpec((None, HEADS, HEAD_DIM), lambda bi: (bi, 0, 0)),
        compiler_params=_cparams(("parallel",)),
        name="attn_sample",
    )(*args)
    return out.reshape(b, ATTN_W).astype(BF16)


def _mm_res_kernel(a_ref, *refs, nj, gated):
    if gated:
        wv_ref, wg_ref, r_ref, g_ref, o_ref, y_scr = refs
    else:
        wv_ref, r_ref, g_ref, o_ref, y_scr = refs
    j = pl.program_id(1)
    a = a_ref[...]
    y = jnp.dot(a, wv_ref[...], preferred_element_type=F32)
    if gated:
        y = y * jax.nn.sigmoid(jnp.dot(a, wg_ref[...], preferred_element_type=F32))
    if nj == 1:
        o_ref[...] = r_ref[...] + _rms(y, g_ref[...])
        return
    y_scr[j] = y

    @pl.when(j == nj - 1)
    def _():
        full = jnp.concatenate([y_scr[t] for t in range(nj)], axis=-1)
        o_ref[...] = r_ref[...] + _rms(full, g_ref[...])


def matmul_residual(a, w_all, layer, resid, g, *, tm, tn, gated=False, name="matmul_residual"):
    m, k = a.shape
    d = resid.shape[1]
    assert m % tm == 0 and d % tn == 0 and w_all.shape[2] == (2 * d if gated else d)
    nj = d // tn
    w_specs = [pl.BlockSpec((None, k, tn), lambda i, j: (layer, 0, j))]
    if gated:
        w_specs.append(pl.BlockSpec((None, k, tn), lambda i, j: (layer, 0, nj + j)))
    scr_shape = (nj, tm, tn) if nj > 1 else (1, SUBLANES, LANES)
    return pl.pallas_call(
        functools.partial(_mm_res_kernel, nj=nj, gated=gated),
        out_shape=jax.ShapeDtypeStruct((m, d), F32),
        grid=(m // tm, nj),
        in_specs=[pl.BlockSpec((tm, k), lambda i, j: (i, 0))] + w_specs + [
            pl.BlockSpec((tm, d), lambda i, j: (i, 0)),
            pl.BlockSpec((1, d), lambda i, j: (0, 0)),
        ],
        out_specs=pl.BlockSpec((tm, d), lambda i, j: (i, 0)),
        scratch_shapes=[pltpu.VMEM(scr_shape, F32)],
        compiler_params=_cparams(("parallel", "arbitrary")),
        name=name,
    )(a, *([w_all] * len(w_specs)), resid, g.reshape(1, d))


def _conv_gate(ug, ug1, ug2, uv, uv1, uv2, cwg, cwv, cbg, cbv):
    gate = cbg + cwg[0:1] * ug2 + cwg[1:2] * ug1 + cwg[2:3] * ug
    val = cbv + cwv[0:1] * uv2 + cwv[1:2] * uv1 + cwv[2:3] * uv
    return (gate * jax.nn.sigmoid(gate) * val).astype(BF16)


def _ffn_seq_kernel(x_ref, xh_ref, g_ref, wg_ref, wv_ref, cwg_ref, cwv_ref, cbg_ref, cbv_ref, wd_ref, gn_ref,
                    o_ref, sg_ref, sv_ref, h_ref, u_scr, *, tiles_per_seq, nj):
    halo = BF16_ROWS
    tm = x_ref.shape[0]
    j = pl.program_id(1)

    @pl.when(j == 0)
    def _():
        g = g_ref[...]
        keep = jnp.where(pl.program_id(0) % tiles_per_seq == 0, 0.0, 1.0)
        h_ref[0:halo] = (_rms(xh_ref[...], g) * keep).astype(BF16)
        h_ref[halo:] = _rms(x_ref[...], g).astype(BF16)
        o_ref[...] = jnp.zeros(o_ref.shape, F32)

    wg, wv, wd = wg_ref[...], wv_ref[...], wd_ref[...]
    cwg, cwv, cbg, cbv = cwg_ref[...], cwv_ref[...], cbg_ref[...], cbv_ref[...]
    rc = min(FFN_ROW_CHUNK, tm)
    for c in range(tm // rc):
        lo = 0 if c == 0 else halo + c * rc
        hi = halo + (c + 1) * rc
        hc = h_ref[lo:hi, :]
        u_scr[0, lo:hi, :] = jnp.dot(hc, wg, preferred_element_type=F32)
        u_scr[1, lo:hi, :] = jnp.dot(hc, wv, preferred_element_type=F32)
        base = halo + c * rc
        taps = [[u_scr[s, pl.ds(base - k, rc), :] for k in range(CONV_WIDTH)] for s in range(2)]
        act = _conv_gate(*taps[0], *taps[1], cwg, cwv, cbg, cbv)
        for n0 in range(0, wd.shape[1], MXU_COLS):
            o_ref[c * rc:(c + 1) * rc, n0:n0 + MXU_COLS] += jnp.dot(act, wd[:, n0:n0 + MXU_COLS],
                                                                   preferred_element_type=F32)
    sg_ref[...] = u_scr[0, halo + tm - SUBLANES:halo + tm, :]
    sv_ref[...] = u_scr[1, halo + tm - SUBLANES:halo + tm, :]

    @pl.when(j == nj - 1)
    def _():
        o_ref[...] = x_ref[...] + _rms(o_ref[...], gn_ref[...])


def _ffn_step_kernel(x_ref, g_ref, wg_ref, wv_ref, g1_ref, v1_ref, g2_ref, v2_ref,
                     cwg_ref, cwv_ref, cbg_ref, cbv_ref, wd_ref, gn_ref, o_ref, sg_ref, sv_ref, h_ref, *, nj):
    j = pl.program_id(1)

    @pl.when(j == 0)
    def _():
        h_ref[...] = _rms(x_ref[...], g_ref[...]).astype(BF16)
        o_ref[...] = jnp.zeros(o_ref.shape, F32)

    h = h_ref[...]
    ug = jnp.dot(h, wg_ref[...], preferred_element_type=F32)
    uv = jnp.dot(h, wv_ref[...], preferred_element_type=F32)
    sg_ref[...] = ug
    sv_ref[...] = uv
    act = _conv_gate(ug, g1_ref[...], g2_ref[...], uv, v1_ref[...], v2_ref[...],
                     cwg_ref[...], cwv_ref[...], cbg_ref[...], cbv_ref[...])
    o_ref[...] += jnp.dot(act, wd_ref[...], preferred_element_type=F32)

    @pl.when(j == nj - 1)
    def _():
        o_ref[...] = x_ref[...] + _rms(o_ref[...], gn_ref[...])


def conv_ffn(x, g_in, g_out, w_all, cw_all, cb_all, wd_all, layer, *, tm, tn, seq_len=None, prev_all=None):
    m, d = x.shape
    n = w_all.shape[3]
    assert m % tm == 0 and n % tn == 0
    nj = n // tn
    gv = lambda rows: [pl.BlockSpec((None, None, rows, tn), lambda i, j, s=s: (layer, s, 0, j)) for s in range(2)]
    w_specs = gv(d)
    c_specs = gv(CONV_WIDTH) + gv(1)
    c_args = [cw_all, cw_all, cb_all, cb_all]
    d_specs = [pl.BlockSpec((None, tn, d), lambda i, j: (layer, j, 0)), pl.BlockSpec((1, d), lambda i, j: (0, 0))]
    d_args = [wd_all, g_out.reshape(1, d)]
    g_spec = pl.BlockSpec((1, d), lambda i, j: (0, 0))
    tile = pl.BlockSpec((tm, tn), lambda i, j: (i, j))
    if prev_all is None:
        assert seq_len % tm == 0 and tm % BF16_ROWS == 0
        hb = tm // BF16_ROWS
        x_spec = pl.BlockSpec((tm, d), lambda i, j: (i, 0), pipeline_mode=pl.Buffered(1))
        halo_spec = pl.BlockSpec((BF16_ROWS, d), lambda i, j: (jnp.maximum(i * hb - 1, 0), 0))
        body = functools.partial(_ffn_seq_kernel, tiles_per_seq=seq_len // tm, nj=nj)
        in_specs = [x_spec, halo_spec, g_spec] + w_specs + c_specs + d_specs
        args = [x, x, g_in.reshape(1, d), w_all, w_all] + c_args + d_args
        st_rows, st_spec = (m // tm) * SUBLANES, pl.BlockSpec((SUBLANES, tn), lambda i, j: (i, j))
        scratch = [pltpu.VMEM((tm + BF16_ROWS, d), BF16), pltpu.VMEM((2, tm + BF16_ROWS, tn), F32)]
    else:
        x_spec = pl.BlockSpec((tm, d), lambda i, j: (i, 0))
        body = functools.partial(_ffn_step_kernel, nj=nj)
        prev_specs = [pl.BlockSpec((None, None, None, tm, tn), lambda i, j, r=r, s=s: (layer, r, s, i, j))
                      for r in (1, 0) for s in range(2)]
        in_specs = [x_spec, g_spec] + w_specs + prev_specs + c_specs + d_specs
        args = [x, g_in.reshape(1, d), w_all, w_all] + [prev_all] * 4 + c_args + d_args
        st_rows, st_spec = m, tile
        scratch = [pltpu.VMEM((tm, d), BF16)]
    st_shape = jax.ShapeDtypeStruct((st_rows, n), F32)
    return pl.pallas_call(
        body,
        out_shape=(jax.ShapeDtypeStruct((m, d), F32), st_shape, st_shape),
        grid=(m // tm, nj),
        in_specs=in_specs,
        out_specs=(pl.BlockSpec((tm, d), lambda i, j: (i, 0)), st_spec, st_spec),
        scratch_shapes=scratch,
        compiler_params=_cparams(("parallel", "arbitrary")),
        name="conv_ffn",
    )(*args)


def _ssm_params_kernel(lre_ref, lim_ref, ldt_ref, bre_ref, bim_ref, are_ref, aim_ref, bbre_ref, bbim_ref):
    lre = jnp.minimum(lre_ref[...], -1e-4)
    lim = lim_ref[...]
    dt = jnp.exp(ldt_ref[...])
    mag = jnp.exp(lre * dt)
    a_re = mag * jnp.cos(lim * dt)
    a_im = mag * jnp.sin(lim * dt)
    are_ref[...] = a_re
    aim_ref[...] = a_im
    x, y = a_re - 1.0, a_im
    den = lre * lre + lim * lim
    c_re = (x * lre + y * lim) / den
    c_im = (y * lre - x * lim) / den
    bre, bim = bre_ref[...], bim_ref[...]
    bbre_ref[...] = c_re * bre - c_im * bim
    bbim_ref[...] = c_re * bim + c_im * bre


def ssm_params(lam_re, lam_im, log_dt, b_re, b_im):
    g, p = lam_re.shape
    shp_a = jax.ShapeDtypeStruct((g, 1, p), F32)
    shp_b = jax.ShapeDtypeStruct((g, SSM_CH, p), F32)
    a_re, a_im, bb_re, bb_im = pl.pallas_call(
        _ssm_params_kernel,
        out_shape=(shp_a, shp_a, shp_b, shp_b),
        compiler_params=pltpu.CompilerParams(vmem_limit_bytes=VMEM_LIMIT),
        name="ssm_params",
    )(lam_re.reshape(g, 1, p), lam_im.reshape(g, 1, p), log_dt.reshape(g, 1, 1),
      jnp.swapaxes(b_re, 1, 2), jnp.swapaxes(b_im, 1, 2))
    return a_re.reshape(g, p), a_im.reshape(g, p), bb_re, bb_im


def _gelu_tanh(y):
    return 0.5 * y * (1.0 + jnp.tanh(math.sqrt(2.0 / math.pi) * (y + 0.044715 * (y * y * y))))


def _ssm_scan_kernel(u_ref, bm_ref, cm_ref, are_ref, aim_ref, d_ref, h0r_ref, h0i_ref,
                     z_ref, sr_ref, si_ref, buf_a, buf_b):
    nb, t = u_ref.shape[0], u_ref.shape[1]
    tc, pitch, half = SSM_CHUNK, SSM_PITCH, SSM_SLABS // 2
    n_chunks = t // tc
    assert n_chunks % 2 == 0 and n_chunks >= 4
    bm, cm = bm_ref[...], cm_ref[...]
    ar, ai = are_ref[...], aim_ref[...]
    dsk = d_ref[...]

    def chunk_rows(c):
        return pl.ds(pl.multiple_of(c * tc, tc), tc)

    def project_in(c, buf):
        for b in range(nb):
            r = jnp.dot(u_ref[b, chunk_rows(c), :].astype(BF16), bm, preferred_element_type=F32)
            for k in range(SSM_SLABS):
                buf[b, k * pitch:k * pitch + tc, :] = r[:, k * LANES:(k + 1) * LANES]

    def project_out(c, buf):
        for b in range(nb):
            xs = jnp.concatenate([buf[b, k * pitch:k * pitch + tc, :] for k in range(SSM_SLABS)], axis=-1)
            y = jnp.dot(xs.astype(BF16), cm, preferred_element_type=F32) + dsk * u_ref[b, chunk_rows(c), :]
            z_ref[b, chunk_rows(c), :] = _gelu_tanh(y).astype(z_ref.dtype)

    def recur(buf, carry):
        carry = list(carry)
        for step in range(tc):
            re_rows = pl.ds(step, half, stride=pitch)
            im_rows = pl.ds(half * pitch + step, half, stride=pitch)
            for b in range(nb):
                xr, xi = carry[b]
                nr = ar * xr - ai * xi + buf[b, re_rows, :]
                ni = ar * xi + ai * xr + buf[b, im_rows, :]
                buf[b, re_rows, :] = nr
                buf[b, im_rows, :] = ni
                carry[b] = (nr, ni)
        return tuple(carry)

    carry = tuple((h0r_ref[b], h0i_ref[b]) for b in range(nb))
    project_in(0, buf_a)
    carry = recur(buf_a, carry)
    project_in(1, buf_b)

    def pair(s, carry):
        c = 2 * s + 1
        carry = recur(buf_b, carry)
        project_out(c - 1, buf_a)
        project_in(c + 1, buf_a)
        carry = recur(buf_a, carry)
        project_out(c, buf_b)
        project_in(c + 2, buf_b)
        return carry

    carry = lax.fori_loop(0, n_chunks // 2 - 1, pair, carry)
    carry = recur(buf_b, carry)
    project_out(n_chunks - 2, buf_a)
    project_out(n_chunks - 1, buf_b)
    for b in range(nb):
        sr_ref[b] = carry[b][0]
        si_ref[b] = carry[b][1]


def ssm_scan(u, bmat, cmat, a_re, a_im, d_skip, h0_re, h0_im):
    b, t, d = u.shape
    assert t % SSM_CHUNK == 0
    half = SSM_SLABS // 2
    st_spec = pl.BlockSpec((b, None, half, LANES), lambda j: (0, j, 0, 0))
    st_shape = jax.ShapeDtypeStruct((b, SSM_TILES, half, LANES), F32)
    seq_spec = lambda **kw: pl.BlockSpec((b, t, SSM_TILE_CH), lambda j: (0, 0, j), **kw)
    buf = pltpu.VMEM((b, SSM_SLABS * SSM_PITCH, LANES), F32)
    return pl.pallas_call(
        _ssm_scan_kernel,
        out_shape=(jax.ShapeDtypeStruct((b, t, d), BF16), st_shape, st_shape),
        grid=(SSM_TILES,),
        in_specs=[
            seq_spec(),
            pl.BlockSpec((None, SSM_TILE_CH, 2 * SSM_TILE_ST), lambda j: (j, 0, 0)),
            pl.BlockSpec((None, 2 * SSM_TILE_ST, SSM_TILE_CH), lambda j: (j, 0, 0)),
            pl.BlockSpec((None, half, LANES), lambda j: (j, 0, 0)),
            pl.BlockSpec((None, half, LANES), lambda j: (j, 0, 0)),
            pl.BlockSpec((1, SSM_TILE_CH), lambda j: (0, j)),
            st_spec, st_spec,
        ],
        out_specs=(seq_spec(), st_spec, st_spec),
        scratch_shapes=[buf, buf],
        compiler_params=_cparams(("parallel",)),
        name="ssm_scan",
    )(u, bmat, cmat, a_re.reshape(SSM_TILES, half, LANES), a_im.reshape(SSM_TILES, half, LANES),
      d_skip.reshape(1, d), h0_re, h0_im)


def _ssm_step_kernel(u_ref, bm_ref, cm_ref, are_ref, aim_ref, d_ref, h0r_ref, h0i_ref, z_ref, sr_ref, si_ref):
    u = u_ref[...]
    bu = jnp.dot(u.astype(BF16), bm_ref[...], preferred_element_type=F32)
    ar, ai = are_ref[...], aim_ref[...]
    hr, hi = h0r_ref[...], h0i_ref[...]
    xr = ar * hr - ai * hi + bu[:, :SSM_TILE_ST]
    xi = ar * hi + ai * hr + bu[:, SSM_TILE_ST:]
    sr_ref[...] = xr
    si_ref[...] = xi
    xs = jnp.concatenate([xr, xi], axis=-1).astype(BF16)
    y = jnp.dot(xs, cm_ref[...], preferred_element_type=F32) + d_ref[...] * u
    z_ref[...] = _gelu_tanh(y).astype(z_ref.dtype)


def ssm_step(u, bmat, cmat, a_re, a_im, d_skip, h0_re, h0_im):
    b, d = u.shape
    st_spec = pl.BlockSpec((None, b, SSM_TILE_ST), lambda j: (j, 0, 0))
    a_spec = pl.BlockSpec((None, 1, SSM_TILE_ST), lambda j: (j, 0, 0))
    st_shape = jax.ShapeDtypeStruct((SSM_TILES, b, SSM_TILE_ST), F32)
    return pl.pallas_call(
        _ssm_step_kernel,
        out_shape=(jax.ShapeDtypeStruct((b, d), BF16), st_shape, st_shape),
        grid=(SSM_TILES,),
        in_specs=[
            pl.BlockSpec((b, SSM_TILE_CH), lambda j: (0, j)),
            pl.BlockSpec((None, SSM_TILE_CH, 2 * SSM_TILE_ST), lambda j: (j, 0, 0)),
            pl.BlockSpec((None, 2 * SSM_TILE_ST, SSM_TILE_CH), lambda j: (j, 0, 0)),
            a_spec, a_spec,
            pl.BlockSpec((1, SSM_TILE_CH), lambda j: (0, j)),
            st_spec, st_spec,
        ],
        out_specs=(pl.BlockSpec((b, SSM_TILE_CH), lambda j: (0, j)), st_spec, st_spec),
        compiler_params=_cparams(("parallel",)),
        name="ssm_step",
    )(u, bmat, cmat, a_re.reshape(SSM_TILES, 1, SSM_TILE_ST), a_im.reshape(SSM_TILES, 1, SSM_TILE_ST),
      d_skip.reshape(1, d), h0_re, h0_im)


def _block_diag_mats(bb_re, bb_im, c_re, c_im):
    ch_group = jnp.arange(SSM_TILE_CH) // SSM_CH
    st_group = jnp.arange(SSM_TILE_ST) // SSM_STATE
    same = ch_group[:, None] == st_group[None, :]

    def bmat(bb):
        rows = bb.reshape(SSM_TILES, SSM_TILE_CH, SSM_STATE)
        return jnp.where(same, jnp.tile(rows, (1, 1, SSM_TILE_G)), 0.0)

    def cmat(cc):
        cols = jnp.swapaxes(cc.reshape(SSM_TILES, SSM_TILE_CH, SSM_STATE), 1, 2)
        return jnp.where(same.T, jnp.tile(cols, (1, SSM_TILE_G, 1)), 0.0)

    b_all = jnp.concatenate([bmat(bb_re), bmat(bb_im)], axis=2).astype(BF16)
    c_all = jnp.concatenate([cmat(c_re), cmat(-c_im)], axis=1).astype(BF16)
    return b_all, c_all


def _cast_pad_kernel(x_ref, o_ref):
    r, c = x_ref.shape
    rr, cc = o_ref.shape
    o_ref[0:r, 0:c] = x_ref[...].astype(o_ref.dtype)
    if cc > c:
        o_ref[:, c:] = jnp.zeros((rr, cc - c), o_ref.dtype)
    if rr > r:
        o_ref[r:, 0:c] = jnp.zeros((rr - r, c), o_ref.dtype)


def _cast_call(x, out_shape, grid, in_spec, out_spec, name):
    return pl.pallas_call(
        _cast_pad_kernel,
        out_shape=jax.ShapeDtypeStruct(out_shape, BF16),
        grid=grid,
        in_specs=[in_spec],
        out_specs=out_spec,
        compiler_params=_cparams(("parallel",) * len(grid)),
        name=name,
    )(x)


def prep_w_qkv(w):
    n_layers, d, n = w.shape
    tr = 512

    def src_block(ob):
        kv = ob - N_DIL
        return jnp.where(ob < N_DIL, ob, N_DIL + (kv % 2) * N_DIL + kv // 2)

    return _cast_call(
        w, (n_layers, d, n), (n_layers, d // tr, n // ATTN_W),
        pl.BlockSpec((None, tr, ATTN_W), lambda l, r, ob: (l, r, src_block(ob))),
        pl.BlockSpec((None, tr, ATTN_W), lambda l, r, ob: (l, r, ob)), "prep_w_qkv")


def prep_w_up(w):
    n_layers, d, _ = w.shape
    tr = 256
    return _cast_call(
        w, (n_layers, 2, d, D_FF_PAD), (n_layers, 2, d // tr),
        pl.BlockSpec((None, tr, D_FF), lambda l, s, r: (l, r, s)),
        pl.BlockSpec((None, None, tr, D_FF_PAD), lambda l, s, r: (l, s, r, 0)), "prep_w_up")


def prep_w_down(w):
    n_layers, _, d = w.shape
    tc = 256
    return _cast_call(
        w, (n_layers, D_FF_PAD, d), (n_layers, d // tc),
        pl.BlockSpec((None, D_FF, tc), lambda l, c: (l, 0, c)),
        pl.BlockSpec((None, D_FF_PAD, tc), lambda l, c: (l, 0, c)), "prep_w_down")


def _ff_split(a, axis):
    a = a.reshape(a.shape[:-1] + (2, D_FF))
    a = jnp.pad(a, [(0, 0)] * (a.ndim - 1) + [(0, D_FF_PAD - D_FF)])
    return jnp.moveaxis(a, -2, axis)


def _ff_join(gate, val):
    return jnp.concatenate([gate[..., :D_FF], val[..., :D_FF]], axis=-1)


def kernel(x_prompt, x_sample, cache_kv_g0, cache_kv_g1, cache_kv_g2, state_ssm, state_conv, norm_g, w_qkv,
           w_attn_o, w_ssm_in, lambda_re, lambda_im, log_dt, b_re, b_im, c_re, c_im, d_skip, w_glu, w_up,
           conv_w, conv_b, w_down):
    bp, t, d = x_prompt.shape
    bs = x_sample.shape[0]
    mp = bp * t
    xp = x_prompt.reshape(mp, d)
    xs = x_sample.reshape(bs, d)
    caches = (cache_kv_g0, cache_kv_g1, cache_kv_g2)
    tm_big, tm = 1024, 512
    tm_ffn, tn_ffn = 512, 512

    wq_all = prep_w_qkv(w_qkv)
    wo_all = w_attn_o.astype(BF16)
    win_all = w_ssm_in.astype(BF16)
    wglu_all = w_glu.astype(BF16)
    wu_all = prep_w_up(w_up)
    wd_all = prep_w_down(w_down)
    cw_all = _ff_split(conv_w, 1)
    cb_all = _ff_split(conv_b.reshape(DEPTH, 1, 2 * D_FF), 1)
    prev_all = jnp.transpose(_ff_split(state_conv, 3), (0, 2, 3, 1, 4))

    rope_p = rope_tables(t, 0)
    rope_s = rope_tables(SUBLANES, PAST_LEN)
    rope_s = tuple(jnp.broadcast_to(r[0:1], (bs, LANES)) for r in rope_s)

    kv_wide = None
    kv_p = [[] for _ in range(N_DIL)]
    kv_s = [[] for _ in range(N_DIL)]
    ssm_p, ssm_s, conv_p, conv_s = [], [], [], []

    for i in range(DEPTH):
        li = i // N_MIXERS
        if i % N_MIXERS == 0:
            assert DIL_GROUPS[-1][0] >= t
            qkv_p, kv_wide = qkv_project(xp, norm_g[i, 0], wq_all, li, rope_p, tm=tm_big,
                                         stack=(w_qkv.shape[0], kv_wide))
            qkv_s = qkv_project(xs, norm_g[i, 0], wq_all, li, rope_s, tm=bs)
            op = attn_prompt(qkv_p.reshape(bp, t, QKV_COLS))
            os_ = attn_sample(qkv_s, caches, li)
            xp = matmul_residual(op.reshape(mp, ATTN_W), wo_all, li, xp, norm_g[i, 1], tm=tm, tn=d, name="attn_out")
            xs = matmul_residual(os_, wo_all, li, xs, norm_g[i, 1], tm=bs, tn=d, name="attn_out")
            for g, (win, _) in enumerate(DIL_GROUPS):
                keep = min(win, t)
                c0 = _kv_col(g)
                if g < N_DIL - 1:
                    kv_p[g].append(qkv_p.reshape(bp, t, QKV_COLS)[:, t - keep:, c0:c0 + KV_COLS]
                                   .reshape(bp, keep, 2, HEADS, HEAD_DIM))
                kv_s[g].append(qkv_s[:, c0:c0 + KV_COLS].reshape(bs, 1, 2, HEADS, HEAD_DIM))
        else:
            a_re, a_im, bb_re, bb_im = ssm_params(lambda_re[li], lambda_im[li], log_dt[li], b_re[li], b_im[li])
            bmat, cmat = _block_diag_mats(bb_re, bb_im, c_re[li], c_im[li])
            half = SSM_SLABS // 2
            up_ = norm_matmul(xp, norm_g[i, 0], win_all, li, tm=tm_big, tn=1024, name="ssm_in")
            us_ = norm_matmul(xs, norm_g[i, 0], win_all, li, tm=bs, tn=1024, name="ssm_in")
            zeros = jnp.zeros((bp, SSM_TILES, half, LANES), F32)
            zp, sp_re, sp_im = ssm_scan(up_.reshape(bp, t, d), bmat, cmat, a_re, a_im, d_skip[li], zeros, zeros)
            h0 = state_ssm[li].reshape(bs, SSM_TILES, SSM_TILE_ST, 2)
            h0_re = jnp.swapaxes(h0[..., 0], 0, 1)
            h0_im = jnp.swapaxes(h0[..., 1], 0, 1)
            zs, ss_re, ss_im = ssm_step(us_, bmat, cmat, a_re, a_im, d_skip[li], h0_re, h0_im)
            xp = matmul_residual(zp.reshape(mp, d), wglu_all, li, xp, norm_g[i, 1], tm=tm, tn=512, gated=True,
                                 name="ssm_out")
            xs = matmul_residual(zs, wglu_all, li, xs, norm_g[i, 1], tm=bs, tn=512, gated=True, name="ssm_out")
            ssm_p.append(jnp.stack([sp_re.reshape(bp, SSM_GROUPS, SSM_STATE),
                                    sp_im.reshape(bp, SSM_GROUPS, SSM_STATE)], axis=-1))
            ssm_s.append(jnp.stack([jnp.swapaxes(ss_re, 0, 1).reshape(bs, SSM_GROUPS, SSM_STATE),
                                    jnp.swapaxes(ss_im, 0, 1).reshape(bs, SSM_GROUPS, SSM_STATE)], axis=-1))

        xp, sg_p, sv_p = conv_ffn(xp, norm_g[i, 2], norm_g[i, 3], wu_all, cw_all, cb_all, wd_all, i,
                                  tm=tm_ffn, tn=tn_ffn, seq_len=t)
        xs, sg_s, sv_s = conv_ffn(xs, norm_g[i, 2], norm_g[i, 3], wu_all, cw_all, cb_all, wd_all, i,
                                  tm=bs, tn=512, prev_all=prev_all)
        tiles = t // tm_ffn
        tail = _ff_join(sg_p, sv_p).reshape(bp, tiles, SUBLANES, 2 * D_FF)
        conv_p.append(tail[:, tiles - 1, SUBLANES - (CONV_WIDTH - 1):])
        conv_s.append(jnp.concatenate([state_conv[i][:, 1:], _ff_join(sg_s, sv_s)[:, None]], axis=1))

    return (xp.reshape(bp, t, d), xs.reshape(bs, 1, d),
            jnp.stack(kv_p[0]), jnp.stack(kv_s[0]),
            jnp.stack(kv_p[1]), jnp.stack(kv_s[1]),
            kv_wide.reshape(w_qkv.shape[0], bp, t, 2, HEADS, HEAD_DIM), jnp.stack(kv_s[2]),
            jnp.stack(ssm_p), jnp.stack(ssm_s),
            jnp.stack(conv_p), jnp.stack(conv_s))
```

```python
import functools
import math

import jax
import jax.numpy as jnp
from jax import lax
from jax.experimental import pallas as pl
from jax.experimental.pallas import tpu as pltpu

F32 = jnp.float32
BF16 = jnp.bfloat16

D_MODEL = 2048
DEPTH = 4
PAST_LEN = 16384
N_MIXERS = 2
HEAD_DIM = 128
HEADS = D_MODEL // (2 * HEAD_DIM)
DIL_GROUPS = ((128, 1), (512, 4), (2048, 16))
N_DIL = len(DIL_GROUPS)
N_BACK = 128
ROT_DIM = HEAD_DIM // 4
ROPE_THETA = 500000.0
NEG_INF = -1e30
SSM_CH = 16
SSM_GROUPS = D_MODEL // SSM_CH
SSM_STATE = 64
D_FF = ((8 * D_MODEL // 3 + 127) // 128) * 128
CONV_WIDTH = 3
RMS_EPS = 1e-6

ATTN_W = HEADS * HEAD_DIM
Q_COLS = N_DIL * ATTN_W
KV_COLS = 2 * ATTN_W
QKV_COLS = 3 * N_DIL * ATTN_W

LANES = 128
SUBLANES = 8
BF16_ROWS = 16
MXU_COLS = 256
VMEM_LIMIT = 52 * 1024 * 1024
ROW_CHUNK = 256
FFN_ROW_CHUNK = 512

FF_ALIGN = 512
D_FF_PAD = ((D_FF + FF_ALIGN - 1) // FF_ALIGN) * FF_ALIGN

SSM_TILE_G = 16
SSM_TILES = SSM_GROUPS // SSM_TILE_G
SSM_TILE_CH = SSM_TILE_G * SSM_CH
SSM_TILE_ST = SSM_TILE_G * SSM_STATE
SSM_SLABS = 2 * SSM_TILE_ST // LANES
SSM_CHUNK = 256
SSM_PITCH = SSM_CHUNK + SUBLANES


def _cparams(sem):
    return pltpu.CompilerParams(dimension_semantics=sem, vmem_limit_bytes=VMEM_LIMIT)


def _rms(x, g):
    ms = jnp.mean(x * x, axis=-1, keepdims=True)
    return x * lax.rsqrt(ms + RMS_EPS) * g


def _rope_table_kernel(inv_ref, cos_ref, sa_ref, sb_ref, *, pos_base):
    rows = cos_ref.shape[0]
    pos = (lax.broadcasted_iota(jnp.int32, (rows, LANES), 0) + pos_base).astype(F32)
    lane = lax.broadcasted_iota(jnp.int32, (rows, LANES), 1)
    ang = pos * inv_ref[...]
    c = jnp.cos(ang)
    s = jnp.sin(ang)
    half = ROT_DIM // 2
    cos_ref[...] = jnp.where(lane < ROT_DIM, c, 1.0)
    sa_ref[...] = jnp.where(lane < half, -s, 0.0)
    sb_ref[...] = jnp.where((lane >= half) & (lane < ROT_DIM), s, 0.0)


def rope_tables(rows, pos_base):
    half = ROT_DIM // 2
    inv = ROPE_THETA ** (-(jnp.arange(half, dtype=F32) * (2.0 / ROT_DIM)))
    inv_lane = jnp.tile(inv, LANES // half).reshape(1, LANES)
    shp = jax.ShapeDtypeStruct((rows, LANES), F32)
    return pl.pallas_call(
        functools.partial(_rope_table_kernel, pos_base=pos_base),
        out_shape=(shp, shp, shp),
        name="rope_tables",
    )(inv_lane)


QKV_TN = 512
Q_TILES = Q_COLS // QKV_TN
KV_TILES = KV_COLS // QKV_TN
assert KV_TILES & (KV_TILES - 1) == 0


def _qkv_kernel(x_ref, g_ref, w_ref, cos_ref, sa_ref, sb_ref, *refs, stacked):
    if stacked:
        o_ref, s_ref, h_ref = refs[-3:]
    else:
        (o_ref, h_ref), s_ref = refs, None
    j = pl.program_id(1)

    @pl.when(j == 0)
    def _():
        h_ref[...] = _rms(x_ref[...], g_ref[...]).astype(BF16)

    is_rope = (j < Q_TILES) | (((j - Q_TILES) & (KV_TILES - 1)) < KV_TILES // 2)
    w = w_ref[...]
    half = ROT_DIM // 2
    tm = x_ref.shape[0]
    rc = min(ROW_CHUNK, tm)
    for c in range(tm // rc):
        rows = slice(c * rc, (c + 1) * rc)
        y = jnp.dot(h_ref[rows, :], w, preferred_element_type=F32)
        cs, sa, sb = cos_ref[rows, :], sa_ref[rows, :], sb_ref[rows, :]
        for h in range(QKV_TN // HEAD_DIM):
            xh = y[:, h * HEAD_DIM:(h + 1) * HEAD_DIM]
            up = pltpu.roll(xh, HEAD_DIM - half, axis=1)
            dn = pltpu.roll(xh, half, axis=1)
            out = jnp.where(is_rope, xh * cs + up * sa + dn * sb, xh)
            o_ref[rows, h * HEAD_DIM:(h + 1) * HEAD_DIM] = out
            if s_ref is not None:
                cols = slice(h * HEAD_DIM, (h + 1) * HEAD_DIM)
                if len(s_ref.shape) == 3:
                    for l in range(s_ref.shape[0]):
                        s_ref[l, rows, cols] = out
                else:
                    s_ref[rows, cols] = out


def qkv_project(x, g, w_all, layer, rope, *, tm, stack=None):
    m, d = x.shape
    cos, sa, sb = rope
    tab_tiles = cos.shape[0] // tm
    assert m % tm == 0 and cos.shape[0] % tm == 0 and w_all.shape[2] == QKV_COLS
    tab_spec = pl.BlockSpec((tm, LANES), lambda i, j: (i % tab_tiles, 0))
    in_specs = [
        pl.BlockSpec((tm, d), lambda i, j: (i, 0)),
        pl.BlockSpec((1, d), lambda i, j: (0, 0)),
        pl.BlockSpec((None, d, QKV_TN), lambda i, j: (layer, 0, j)),
        tab_spec, tab_spec, tab_spec,
    ]
    args = [x, g.reshape(1, d), w_all, cos, sa, sb]
    out_shape = jax.ShapeDtypeStruct((m, QKV_COLS), F32)
    out_specs = pl.BlockSpec((tm, QKV_TN), lambda i, j: (i, j))
    aliases = {}
    if stack is not None:
        n_layers, earlier = stack
        base = QKV_COLS // QKV_TN - KV_TILES
        out_shape = (out_shape, jax.ShapeDtypeStruct((n_layers, m, KV_COLS), F32))
        if earlier is None:
            out_specs = (out_specs, pl.BlockSpec((n_layers, tm, QKV_TN), lambda i, j: (0, i, jnp.maximum(j - base, 0))))
        else:
            out_specs = (out_specs, pl.BlockSpec((None, tm, QKV_TN), lambda i, j: (layer, i, jnp.maximum(j - base, 0))))
            assert earlier.shape == (n_layers, m, KV_COLS)
            aliases = {len(args): 1}
            in_specs.append(pl.BlockSpec(memory_space=pl.ANY))
            args.append(earlier)
    return pl.pallas_call(
        functools.partial(_qkv_kernel, stacked=stack is not None),
        out_shape=out_shape,
        grid=(m // tm, QKV_COLS // QKV_TN),
        in_specs=in_specs,
        out_specs=out_specs,
        scratch_shapes=[pltpu.VMEM((tm, d), BF16)],
        input_output_aliases=aliases,
        compiler_params=_cparams(("parallel", "arbitrary")),
        name="qkv_project",
    )(*args)


def _permute_qkv_cols(w):
    parts = [w[..., :Q_COLS]]
    for g in range(N_DIL):
        parts.append(w[..., Q_COLS + g * ATTN_W:Q_COLS + (g + 1) * ATTN_W])
        parts.append(w[..., 2 * Q_COLS + g * ATTN_W:2 * Q_COLS + (g + 1) * ATTN_W])
    return jnp.concatenate(parts, axis=-1)


def _kv_col(g):
    return Q_COLS + g * KV_COLS


def _norm_mm_kernel(x_ref, g_ref, w_ref, o_ref, h_ref):
    @pl.when(pl.program_id(1) == 0)
    def _():
        h_ref[...] = _rms(x_ref[...], g_ref[...]).astype(BF16)

    o_ref[...] = jnp.dot(h_ref[...], w_ref[...], preferred_element_type=F32).astype(o_ref.dtype)


def norm_matmul(x, g, w_all, layer, *, tm, tn, out_dtype=F32, name="norm_matmul"):
    m, d = x.shape
    n = w_all.shape[2]
    assert m % tm == 0 and n % tn == 0
    return pl.pallas_call(
        _norm_mm_kernel,
        out_shape=jax.ShapeDtypeStruct((m, n), out_dtype),
        grid=(m // tm, n // tn),
        in_specs=[
            pl.BlockSpec((tm, d), lambda i, j: (i, 0)),
            pl.BlockSpec((1, d), lambda i, j: (0, 0)),
            pl.BlockSpec((None, d, tn), lambda i, j: (layer, 0, j)),
        ],
        out_specs=pl.BlockSpec((tm, tn), lambda i, j: (i, j)),
        scratch_shapes=[pltpu.VMEM((tm, d), BF16)],
        compiler_params=_cparams(("parallel", "arbitrary")),
        name=name,
    )(x, g.reshape(1, d), w_all)


ATTN_BLK = 128
assert ATTN_BLK == N_BACK


def _residue_major(ref, dil):
    t = ref.shape[0]
    if dil == 1:
        return ref[...]
    return jnp.concatenate([ref[pl.ds(r, t // dil, stride=dil), :] for r in range(dil)], axis=0)


def _store_token_order(ref, g, dil, val):
    nb = val.shape[0]
    if dil == 1:
        ref[g] = val.reshape(nb * ATTN_BLK, LANES)
        return
    per = nb // dil
    for r in range(dil):
        ref[g, pl.ds(r, per * ATTN_BLK, stride=dil), :] = val[r * per:(r + 1) * per].reshape(per * ATTN_BLK, LANES)


def _attn_prompt_kernel(q0, q1, q2, k0, k1, k2, v0, v1, v2, o_ref, o_scr, l_scr, d_scr):
    qs, ks, vs = (q0, q1, q2), (k0, k1, k2), (v0, v1, v2)
    t = q0.shape[0]
    nb = t // ATTN_BLK
    blk = ATTN_BLK

    for g, (_, dil) in enumerate(DIL_GROUPS):
        per = nb // dil
        to_blocks = lambda a: a.reshape(nb, blk, HEAD_DIM)
        qb = to_blocks((_residue_major(qs[g], dil) * (HEAD_DIM ** -0.5)).astype(BF16))
        kb = to_blocks(_residue_major(ks[g], dil).astype(BF16))
        vb = to_blocks(_residue_major(vs[g], dil).astype(BF16))
        if per > 1:
            kk = jnp.concatenate([jnp.concatenate([kb[:1], kb[:-1]], axis=0), kb], axis=1)
            vv = jnp.concatenate([jnp.concatenate([vb[:1], vb[:-1]], axis=0), vb], axis=1)
            nk = 2 * blk
            bi = lax.broadcasted_iota(jnp.int32, (nb, 1, nk), 0)
            ci = lax.broadcasted_iota(jnp.int32, (nb, 1, nk), 2)
            no_prev = jnp.where(((bi & (per - 1)) == 0) & (ci < blk), NEG_INF, 0.0)
        else:
            kk, vv, nk = kb, vb, blk
            no_prev = None
        rel = (lax.broadcasted_iota(jnp.int32, (blk, nk), 0) + (nk - blk)
               - lax.broadcasted_iota(jnp.int32, (blk, nk), 1))
        band = jnp.where((rel >= 0) & (rel <= N_BACK), 0.0, NEG_INF)

        s = jnp.einsum('bqd,bkd->bqk', qb, kk, preferred_element_type=F32) + band[None]
        if no_prev is not None:
            s = s + no_prev
        m = jnp.max(s, axis=-1, keepdims=True)
        p = jnp.exp(s - m)
        l = jnp.sum(p, axis=-1, keepdims=True)
        o = jnp.einsum('bqk,bkd->bqd', p.astype(BF16), vv, preferred_element_type=F32)

        @pl.when(pl.program_id(1) == 0)
        def _(g=g, l=l):
            d_scr[g] = jnp.broadcast_to(l, (nb, blk, LANES))

        _store_token_order(o_scr, g, dil, o / d_scr[g])
        _store_token_order(l_scr, g, dil, jnp.broadcast_to(m + jnp.log(l), (nb, blk, LANES)))

    l0, l1, l2 = l_scr[0], l_scr[1], l_scr[2]
    m = jnp.maximum(jnp.maximum(l0, l1), l2)
    e0, e1, e2 = jnp.exp(l0 - m), jnp.exp(l1 - m), jnp.exp(l2 - m)
    num = e0 * o_scr[0] + e1 * o_scr[1] + e2 * o_scr[2]
    o_ref[...] = (num / (e0 + e1 + e2)).astype(o_ref.dtype)


def attn_prompt(qkv):
    b, t, _ = qkv.shape
    assert t % (ATTN_BLK * DIL_GROUPS[-1][1]) == 0

    def spec(col0):
        return pl.BlockSpec((None, t, HEAD_DIM), lambda bi, h: (bi, 0, col0 // HEAD_DIM + h))

    in_specs = ([spec(g * ATTN_W) for g in range(N_DIL)]
                + [spec(_kv_col(g)) for g in range(N_DIL)]
                + [spec(_kv_col(g) + ATTN_W) for g in range(N_DIL)])
    nb = t // ATTN_BLK
    return pl.pallas_call(
        _attn_prompt_kernel,
        out_shape=jax.ShapeDtypeStruct((b, t, ATTN_W), BF16),
        grid=(b, HEADS),
        in_specs=in_specs,
        out_specs=pl.BlockSpec((None, t, HEAD_DIM), lambda bi, h: (bi, 0, h)),
        scratch_shapes=[pltpu.VMEM((N_DIL, t, LANES), F32), pltpu.VMEM((N_DIL, t, LANES), F32),
                        pltpu.VMEM((N_DIL, nb, ATTN_BLK, LANES), F32)],
        compiler_params=_cparams(("parallel", "arbitrary")),
        name="attn_prompt",
    )(*([qkv] * 9))


def _attn_sample_kernel(x_ref, c0, c1, c2, o_ref):
    caches = (c0, c1, c2)
    outs, lses = [], []
    for g in range(N_DIL):
        r0 = _kv_col(g) // HEAD_DIM
        q = x_ref[g * HEADS:(g + 1) * HEADS, :] * (HEAD_DIM ** -0.5)
        k_new = x_ref[r0:r0 + HEADS, :]
        v_new = x_ref[r0 + HEADS:r0 + 2 * HEADS, :]
        kc = caches[g][:, 0:HEADS, :]
        vc = caches[g][:, HEADS:2 * HEADS, :]
        s = jnp.sum(kc * q[None], axis=-1, keepdims=True)
        s_new = jnp.sum(q * k_new, axis=-1, keepdims=True)
        m = jnp.maximum(jnp.max(s, axis=0), s_new)
        p = jnp.exp(s - m[None])
        p_new = jnp.exp(s_new - m)
        l = jnp.sum(p, axis=0) + p_new
        o = jnp.sum(p * vc, axis=0) + p_new * v_new
        outs.append(o / l[0:1])
        lses.append(m + jnp.log(l))
    m = jnp.maximum(jnp.maximum(lses[0], lses[1]), lses[2])
    es = [jnp.exp(x - m) for x in lses]
    num = es[0] * outs[0] + es[1] * outs[1] + es[2] * outs[2]
    o_ref[...] = num / (es[0] + es[1] + es[2])


def attn_sample(qkv, caches, layer):
    b = qkv.shape[0]
    rows = QKV_COLS // HEAD_DIM
    in_specs = [pl.BlockSpec((None, rows, HEAD_DIM), lambda bi: (bi, 0, 0))]
    args = [qkv.reshape(b, rows, HEAD_DIM)]
    for (win, dil), cache in zip(DIL_GROUPS, caches):
        n_layers, _, length = cache.shape[:3]
        assert length == win and length // dil == N_BACK
        args.append(cache.reshape(n_layers, b, N_BACK, dil * 2 * HEADS, HEAD_DIM))
        in_specs.append(pl.BlockSpec((None, None, N_BACK, 2 * HEADS, HEAD_DIM), lambda bi: (layer, bi, 0, 0, 0)))
    out = pl.pallas_call(
        _attn_sample_kernel,
        out_shape=jax.ShapeDtypeStruct((b, HEADS, HEAD_DIM), F32),
        grid=(b,),
        in_specs=in_specs,
        out_specs=pl.BlockSpec((None, HEADS, HEAD_DIM), lambda bi: (bi, 0, 0)),
        compiler_params=_cparams(("parallel",)),
        name="attn_sample",
    )(*args)
    return out.reshape(b, ATTN_W).astype(BF16)


def _mm_res_kernel(a_ref, *refs, nj, gated):
    if gated:
        wv_ref, wg_ref, r_ref, g_ref, o_ref, y_scr = refs
    else:
        wv_ref, r_ref, g_ref, o_ref, y_scr = refs
    j = pl.program_id(1)
    a = a_ref[...]
    y = jnp.dot(a, wv_ref[...], preferred_element_type=F32)
    if gated:
        y = y * jax.nn.sigmoid(jnp.dot(a, wg_ref[...], preferred_element_type=F32))
    if nj == 1:
        o_ref[...] = r_ref[...] + _rms(y, g_ref[...])
        return
    y_scr[j] = y

    @pl.when(j == nj - 1)
    def _():
        full = jnp.concatenate([y_scr[t] for t in range(nj)], axis=-1)
        o_ref[...] = r_ref[...] + _rms(full, g_ref[...])


def matmul_residual(a, w_all, layer, resid, g, *, tm, tn, gated=False, name="matmul_residual"):
    m, k = a.shape
    d = resid.shape[1]
    assert m % tm == 0 and d % tn == 0 and w_all.shape[2] == (2 * d if gated else d)
    nj = d // tn
    w_specs = [pl.BlockSpec((None, k, tn), lambda i, j: (layer, 0, j))]
    if gated:
        w_specs.append(pl.BlockSpec((None, k, tn), lambda i, j: (layer, 0, nj + j)))
    scr_shape = (nj, tm, tn) if nj > 1 else (1, SUBLANES, LANES)
    return pl.pallas_call(
        functools.partial(_mm_res_kernel, nj=nj, gated=gated),
        out_shape=jax.ShapeDtypeStruct((m, d), F32),
        grid=(m // tm, nj),
        in_specs=[pl.BlockSpec((tm, k), lambda i, j: (i, 0))] + w_specs + [
            pl.BlockSpec((tm, d), lambda i, j: (i, 0)),
            pl.BlockSpec((1, d), lambda i, j: (0, 0)),
        ],
        out_specs=pl.BlockSpec((tm, d), lambda i, j: (i, 0)),
        scratch_shapes=[pltpu.VMEM(scr_shape, F32)],
        compiler_params=_cparams(("parallel", "arbitrary")),
        name=name,
    )(a, *([w_all] * len(w_specs)), resid, g.reshape(1, d))


def _conv_gate(ug, ug1, ug2, uv, uv1, uv2, cwg, cwv, cbg, cbv):
    gate = cbg + cwg[0:1] * ug2 + cwg[1:2] * ug1 + cwg[2:3] * ug
    val = cbv + cwv[0:1] * uv2 + cwv[1:2] * uv1 + cwv[2:3] * uv
    return (gate * jax.nn.sigmoid(gate) * val).astype(BF16)


def _ffn_seq_kernel(x_ref, xh_ref, g_ref, wg_ref, wv_ref, cwg_ref, cwv_ref, cbg_ref, cbv_ref, wd_ref, gn_ref,
                    o_ref, sg_ref, sv_ref, h_ref, u_scr, *, tiles_per_seq, nj):
    halo = BF16_ROWS
    tm = x_ref.shape[0]
    j = pl.program_id(1)

    @pl.when(j == 0)
    def _():
        g = g_ref[...]
        keep = jnp.where(pl.program_id(0) % tiles_per_seq == 0, 0.0, 1.0)
        h_ref[0:halo] = (_rms(xh_ref[...], g) * keep).astype(BF16)
        h_ref[halo:] = _rms(x_ref[...], g).astype(BF16)
        o_ref[...] = jnp.zeros(o_ref.shape, F32)

    wg, wv, wd = wg_ref[...], wv_ref[...], wd_ref[...]
    cwg, cwv, cbg, cbv = cwg_ref[...], cwv_ref[...], cbg_ref[...], cbv_ref[...]
    rc = min(FFN_ROW_CHUNK, tm)
    for c in range(tm // rc):
        lo = 0 if c == 0 else halo + c * rc
        hi = halo + (c + 1) * rc
        hc = h_ref[lo:hi, :]
        u_scr[0, lo:hi, :] = jnp.dot(hc, wg, preferred_element_type=F32)
        u_scr[1, lo:hi, :] = jnp.dot(hc, wv, preferred_element_type=F32)
        base = halo + c * rc
        taps = [[u_scr[s, pl.ds(base - k, rc), :] for k in range(CONV_WIDTH)] for s in range(2)]
        act = _conv_gate(*taps[0], *taps[1], cwg, cwv, cbg, cbv)
        for n0 in range(0, wd.shape[1], MXU_COLS):
            o_ref[c * rc:(c + 1) * rc, n0:n0 + MXU_COLS] += jnp.dot(act, wd[:, n0:n0 + MXU_COLS],
                                                                   preferred_element_type=F32)
    sg_ref[...] = u_scr[0, halo + tm - SUBLANES:halo + tm, :]
    sv_ref[...] = u_scr[1, halo + tm - SUBLANES:halo + tm, :]

    @pl.when(j == nj - 1)
    def _():
        o_ref[...] = x_ref[...] + _rms(o_ref[...], gn_ref[...])


def _ffn_step_kernel(x_ref, g_ref, wg_ref, wv_ref, g1_ref, v1_ref, g2_ref, v2_ref,
                     cwg_ref, cwv_ref, cbg_ref, cbv_ref, wd_ref, gn_ref, o_ref, sg_ref, sv_ref, h_ref, *, nj):
    j = pl.program_id(1)

    @pl.when(j == 0)
    def _():
        h_ref[...] = _rms(x_ref[...], g_ref[...]).astype(BF16)
        o_ref[...] = jnp.zeros(o_ref.shape, F32)

    h = h_ref[...]
    ug = jnp.dot(h, wg_ref[...], preferred_element_type=F32)
    uv = jnp.dot(h, wv_ref[...], preferred_element_type=F32)
    sg_ref[...] = ug
    sv_ref[...] = uv
    act = _conv_gate(ug, g1_ref[...], g2_ref[...], uv, v1_ref[...], v2_ref[...],
                     cwg_ref[...], cwv_ref[...], cbg_ref[...], cbv_ref[...])
    o_ref[...] += jnp.dot(act, wd_ref[...], preferred_element_type=F32)

    @pl.when(j == nj - 1)
    def _():
        o_ref[...] = x_ref[...] + _rms(o_ref[...], gn_ref[...])


def conv_ffn(x, g_in, g_out, w_all, cw_all, cb_all, wd_all, layer, *, tm, tn, seq_len=None, prev_all=None):
    m, d = x.shape
    n = w_all.shape[3]
    assert m % tm == 0 and n % tn == 0
    nj = n // tn
    gv = lambda rows: [pl.BlockSpec((None, None, rows, tn), lambda i, j, s=s: (layer, s, 0, j)) for s in range(2)]
    w_specs = gv(d)
    c_specs = gv(CONV_WIDTH) + gv(1)
    c_args = [cw_all, cw_all, cb_all, cb_all]
    d_specs = [pl.BlockSpec((None, tn, d), lambda i, j: (layer, j, 0)), pl.BlockSpec((1, d), lambda i, j: (0, 0))]
    d_args = [wd_all, g_out.reshape(1, d)]
    g_spec = pl.BlockSpec((1, d), lambda i, j: (0, 0))
    tile = pl.BlockSpec((tm, tn), lambda i, j: (i, j))
    if prev_all is None:
        assert seq_len % tm == 0 and tm % BF16_ROWS == 0
        hb = tm // BF16_ROWS
        x_spec = pl.BlockSpec((tm, d), lambda i, j: (i, 0))
        halo_spec = pl.BlockSpec((BF16_ROWS, d), lambda i, j: (jnp.maximum(i * hb - 1, 0), 0))
        body = functools.partial(_ffn_seq_kernel, tiles_per_seq=seq_len // tm, nj=nj)
        in_specs = [x_spec, halo_spec, g_spec] + w_specs + c_specs + d_specs
        args = [x, x, g_in.reshape(1, d), w_all, w_all] + c_args + d_args
        st_rows, st_spec = (m // tm) * SUBLANES, pl.BlockSpec((SUBLANES, tn), lambda i, j: (i, j))
        scratch = [pltpu.VMEM((tm + BF16_ROWS, d), BF16), pltpu.VMEM((2, tm + BF16_ROWS, tn), F32)]
    else:
        x_spec = pl.BlockSpec((tm, d), lambda i, j: (i, 0))
        body = functools.partial(_ffn_step_kernel, nj=nj)
        prev_specs = [pl.BlockSpec((None, None, None, tm, tn), lambda i, j, r=r, s=s: (layer, r, s, i, j))
                      for r in (1, 0) for s in range(2)]
        in_specs = [x_spec, g_spec] + w_specs + prev_specs + c_specs + d_specs
        args = [x, g_in.reshape(1, d), w_all, w_all] + [prev_all] * 4 + c_args + d_args
        st_rows, st_spec = m, tile
        scratch = [pltpu.VMEM((tm, d), BF16)]
    st_shape = jax.ShapeDtypeStruct((st_rows, n), F32)
    return pl.pallas_call(
        body,
        out_shape=(jax.ShapeDtypeStruct((m, d), F32), st_shape, st_shape),
        grid=(m // tm, nj),
        in_specs=in_specs,
        out_specs=(pl.BlockSpec((tm, d), lambda i, j: (i, 0)), st_spec, st_spec),
        scratch_shapes=scratch,
        compiler_params=_cparams(("parallel", "arbitrary")),
        name="conv_ffn",
    )(*args)


def _ssm_params_kernel(lre_ref, lim_ref, ldt_ref, bre_ref, bim_ref, are_ref, aim_ref, bbre_ref, bbim_ref):
    lre = jnp.minimum(lre_ref[...], -1e-4)
    lim = lim_ref[...]
    dt = jnp.exp(ldt_ref[...])
    mag = jnp.exp(lre * dt)
    a_re = mag * jnp.cos(lim * dt)
    a_im = mag * jnp.sin(lim * dt)
    are_ref[...] = a_re
    aim_ref[...] = a_im
    x, y = a_re - 1.0, a_im
    den = lre * lre + lim * lim
    c_re = (x * lre + y * lim) / den
    c_im = (y * lre - x * lim) / den
    bre, bim = bre_ref[...], bim_ref[...]
    bbre_ref[...] = c_re * bre - c_im * bim
    bbim_ref[...] = c_re * bim + c_im * bre


def ssm_params(lam_re, lam_im, log_dt, b_re, b_im):
    g, p = lam_re.shape
    shp_a = jax.ShapeDtypeStruct((g, 1, p), F32)
    shp_b = jax.ShapeDtypeStruct((g, SSM_CH, p), F32)
    a_re, a_im, bb_re, bb_im = pl.pallas_call(
        _ssm_params_kernel,
        out_shape=(shp_a, shp_a, shp_b, shp_b),
        compiler_params=pltpu.CompilerParams(vmem_limit_bytes=VMEM_LIMIT),
        name="ssm_params",
    )(lam_re.reshape(g, 1, p), lam_im.reshape(g, 1, p), log_dt.reshape(g, 1, 1),
      jnp.swapaxes(b_re, 1, 2), jnp.swapaxes(b_im, 1, 2))
    return a_re.reshape(g, p), a_im.reshape(g, p), bb_re, bb_im


def _gelu_tanh(y):
    return 0.5 * y * (1.0 + jnp.tanh(math.sqrt(2.0 / math.pi) * (y + 0.044715 * (y * y * y))))


def _ssm_scan_kernel(u_ref, bm_ref, cm_ref, are_ref, aim_ref, d_ref, h0r_ref, h0i_ref,
                     z_ref, sr_ref, si_ref, buf_a, buf_b):
    nb, t = u_ref.shape[0], u_ref.shape[1]
    tc, pitch, half = SSM_CHUNK, SSM_PITCH, SSM_SLABS // 2
    n_chunks = t // tc
    assert n_chunks % 2 == 0 and n_chunks >= 4
    bm, cm = bm_ref[...], cm_ref[...]
    ar, ai = are_ref[...], aim_ref[...]
    dsk = d_ref[...]

    def chunk_rows(c):
        return pl.ds(pl.multiple_of(c * tc, tc), tc)

    def project_in(c, buf):
        for b in range(nb):
            r = jnp.dot(u_ref[b, chunk_rows(c), :].astype(BF16), bm, preferred_element_type=F32)
            for k in range(SSM_SLABS):
                buf[b, k * pitch:k * pitch + tc, :] = r[:, k * LANES:(k + 1) * LANES]

    def project_out(c, buf):
        for b in range(nb):
            xs = jnp.concatenate([buf[b, k * pitch:k * pitch + tc, :] for k in range(SSM_SLABS)], axis=-1)
            y = jnp.dot(xs.astype(BF16), cm, preferred_element_type=F32) + dsk * u_ref[b, chunk_rows(c), :]
            z_ref[b, chunk_rows(c), :] = _gelu_tanh(y).astype(z_ref.dtype)

    def recur(buf, carry):
        carry = list(carry)
        for step in range(tc):
            re_rows = pl.ds(step, half, stride=pitch)
            im_rows = pl.ds(half * pitch + step, half, stride=pitch)
            for b in range(nb):
                xr, xi = carry[b]
                nr = ar * xr - ai * xi + buf[b, re_rows, :]
                ni = ar * xi + ai * xr + buf[b, im_rows, :]
                buf[b, re_rows, :] = nr
                buf[b, im_rows, :] = ni
                carry[b] = (nr, ni)
        return tuple(carry)

    carry = tuple((h0r_ref[b], h0i_ref[b]) for b in range(nb))
    project_in(0, buf_a)
    carry = recur(buf_a, carry)
    project_in(1, buf_b)

    def pair(s, carry):
        c = 2 * s + 1
        carry = recur(buf_b, carry)
        project_out(c - 1, buf_a)
        project_in(c + 1, buf_a)
        carry = recur(buf_a, carry)
        project_out(c, buf_b)
        project_in(c + 2, buf_b)
        return carry

    carry = lax.fori_loop(0, n_chunks // 2 - 1, pair, carry)
    carry = recur(buf_b, carry)
    project_out(n_chunks - 2, buf_a)
    project_out(n_chunks - 1, buf_b)
    for b in range(nb):
        sr_ref[b] = carry[b][0]
        si_ref[b] = carry[b][1]


def ssm_scan(u, bmat, cmat, a_re, a_im, d_skip, h0_re, h0_im):
    b, t, d = u.shape
    assert t % SSM_CHUNK == 0
    half = SSM_SLABS // 2
    st_spec = pl.BlockSpec((b, None, half, LANES), lambda j: (0, j, 0, 0))
    st_shape = jax.ShapeDtypeStruct((b, SSM_TILES, half, LANES), F32)
    seq_spec = lambda **kw: pl.BlockSpec((b, t, SSM_TILE_CH), lambda j: (0, 0, j), **kw)
    buf = pltpu.VMEM((b, SSM_SLABS * SSM_PITCH, LANES), F32)
    return pl.pallas_call(
        _ssm_scan_kernel,
        out_shape=(jax.ShapeDtypeStruct((b, t, d), BF16), st_shape, st_shape),
        grid=(SSM_TILES,),
        in_specs=[
            seq_spec(),
            pl.BlockSpec((None, SSM_TILE_CH, 2 * SSM_TILE_ST), lambda j: (j, 0, 0)),
            pl.BlockSpec((None, 2 * SSM_TILE_ST, SSM_TILE_CH), lambda j: (j, 0, 0)),
            pl.BlockSpec((None, half, LANES), lambda j: (j, 0, 0)),
            pl.BlockSpec((None, half, LANES), lambda j: (j, 0, 0)),
            pl.BlockSpec((1, SSM_TILE_CH), lambda j: (0, j)),
            st_spec, st_spec,
        ],
        out_specs=(seq_spec(), st_spec, st_spec),
        scratch_shapes=[buf, buf],
        compiler_params=_cparams(("parallel",)),
        name="ssm_scan",
    )(u, bmat, cmat, a_re.reshape(SSM_TILES, half, LANES), a_im.reshape(SSM_TILES, half, LANES),
      d_skip.reshape(1, d), h0_re, h0_im)


def _ssm_step_kernel(u_ref, bm_ref, cm_ref, are_ref, aim_ref, d_ref, h0r_ref, h0i_ref, z_ref, sr_ref, si_ref):
    u = u_ref[...]
    bu = jnp.dot(u.astype(BF16), bm_ref[...], preferred_element_type=F32)
    ar, ai = are_ref[...], aim_ref[...]
    hr, hi = h0r_ref[...], h0i_ref[...]
    xr = ar * hr - ai * hi + bu[:, :SSM_TILE_ST]
    xi = ar * hi + ai * hr + bu[:, SSM_TILE_ST:]
    sr_ref[...] = xr
    si_ref[...] = xi
    xs = jnp.concatenate([xr, xi], axis=-1).astype(BF16)
    y = jnp.dot(xs, cm_ref[...], preferred_element_type=F32) + d_ref[...] * u
    z_ref[...] = _gelu_tanh(y).astype(z_ref.dtype)


def ssm_step(u, bmat, cmat, a_re, a_im, d_skip, h0_re, h0_im):
    b, d = u.shape
    st_spec = pl.BlockSpec((None, b, SSM_TILE_ST), lambda j: (j, 0, 0))
    a_spec = pl.BlockSpec((None, 1, SSM_TILE_ST), lambda j: (j, 0, 0))
    st_shape = jax.ShapeDtypeStruct((SSM_TILES, b, SSM_TILE_ST), F32)
    return pl.pallas_call(
        _ssm_step_kernel,
        out_shape=(jax.ShapeDtypeStruct((b, d), BF16), st_shape, st_shape),
        grid=(SSM_TILES,),
        in_specs=[
            pl.BlockSpec((b, SSM_TILE_CH), lambda j: (0, j)),
            pl.BlockSpec((None, SSM_TILE_CH, 2 * SSM_TILE_ST), lambda j: (j, 0, 0)),
            pl.BlockSpec((None, 2 * SSM_TILE_ST, SSM_TILE_CH), lambda j: (j, 0, 0)),
            a_spec, a_spec,
            pl.BlockSpec((1, SSM_TILE_CH), lambda j: (0, j)),
            st_spec, st_spec,
        ],
        out_specs=(pl.BlockSpec((b, SSM_TILE_CH), lambda j: (0, j)), st_spec, st_spec),
        compiler_params=_cparams(("parallel",)),
        name="ssm_step",
    )(u, bmat, cmat, a_re.reshape(SSM_TILES, 1, SSM_TILE_ST), a_im.reshape(SSM_TILES, 1, SSM_TILE_ST),
      d_skip.reshape(1, d), h0_re, h0_im)


def _block_diag_mats(bb_re, bb_im, c_re, c_im):
    ch_group = jnp.arange(SSM_TILE_CH) // SSM_CH
    st_group = jnp.arange(SSM_TILE_ST) // SSM_STATE
    same = ch_group[:, None] == st_group[None, :]

    def bmat(bb):
        rows = bb.reshape(SSM_TILES, SSM_TILE_CH, SSM_STATE)
        return jnp.where(same, jnp.tile(rows, (1, 1, SSM_TILE_G)), 0.0)

    def cmat(cc):
        cols = jnp.swapaxes(cc.reshape(SSM_TILES, SSM_TILE_CH, SSM_STATE), 1, 2)
        return jnp.where(same.T, jnp.tile(cols, (1, SSM_TILE_G, 1)), 0.0)

    b_all = jnp.concatenate([bmat(bb_re), bmat(bb_im)], axis=2).astype(BF16)
    c_all = jnp.concatenate([cmat(c_re), cmat(-c_im)], axis=1).astype(BF16)
    return b_all, c_all


def _cast_pad_kernel(x_ref, o_ref):
    r, c = x_ref.shape
    rr, cc = o_ref.shape
    o_ref[0:r, 0:c] = x_ref[...].astype(o_ref.dtype)
    if cc > c:
        o_ref[:, c:] = jnp.zeros((rr, cc - c), o_ref.dtype)
    if rr > r:
        o_ref[r:, 0:c] = jnp.zeros((rr - r, c), o_ref.dtype)


def _cast_call(x, out_shape, grid, in_spec, out_spec, name):
    return pl.pallas_call(
        _cast_pad_kernel,
        out_shape=jax.ShapeDtypeStruct(out_shape, BF16),
        grid=grid,
        in_specs=[in_spec],
        out_specs=out_spec,
        compiler_params=_cparams(("parallel",) * len(grid)),
        name=name,
    )(x)


def prep_w_qkv(w):
    n_layers, d, n = w.shape
    tr = 512

    def src_block(ob):
        kv = ob - N_DIL
        return jnp.where(ob < N_DIL, ob, N_DIL + (kv % 2) * N_DIL + kv // 2)

    return _cast_call(
        w, (n_layers, d, n), (n_layers, d // tr, n // ATTN_W),
        pl.BlockSpec((None, tr, ATTN_W), lambda l, r, ob: (l, r, src_block(ob))),
        pl.BlockSpec((None, tr, ATTN_W), lambda l, r, ob: (l, r, ob)), "prep_w_qkv")


def prep_w_up(w):
    n_layers, d, _ = w.shape
    tr = 256
    return _cast_call(
        w, (n_layers, 2, d, D_FF_PAD), (n_layers, 2, d // tr),
        pl.BlockSpec((None, tr, D_FF), lambda l, s, r: (l, r, s)),
        pl.BlockSpec((None, None, tr, D_FF_PAD), lambda l, s, r: (l, s, r, 0)), "prep_w_up")


def prep_w_down(w):
    n_layers, _, d = w.shape
    tc = 256
    return _cast_call(
        w, (n_layers, D_FF_PAD, d), (n_layers, d // tc),
        pl.BlockSpec((None, D_FF, tc), lambda l, c: (l, 0, c)),
        pl.BlockSpec((None, D_FF_PAD, tc), lambda l, c: (l, 0, c)), "prep_w_down")


def _ff_split(a, axis):
    a = a.reshape(a.shape[:-1] + (2, D_FF))
    a = jnp.pad(a, [(0, 0)] * (a.ndim - 1) + [(0, D_FF_PAD - D_FF)])
    return jnp.moveaxis(a, -2, axis)


def _ff_join(gate, val):
    return jnp.concatenate([gate[..., :D_FF], val[..., :D_FF]], axis=-1)


def kernel(x_prompt, x_sample, cache_kv_g0, cache_kv_g1, cache_kv_g2, state_ssm, state_conv, norm_g, w_qkv,
           w_attn_o, w_ssm_in, lambda_re, lambda_im, log_dt, b_re, b_im, c_re, c_im, d_skip, w_glu, w_up,
           conv_w, conv_b, w_down):
    bp, t, d = x_prompt.shape
    bs = x_sample.shape[0]
    mp = bp * t
    xp = x_prompt.reshape(mp, d)
    xs = x_sample.reshape(bs, d)
    caches = (cache_kv_g0, cache_kv_g1, cache_kv_g2)
    tm_big, tm = 1024, 512
    tm_ffn, tn_ffn = 512, 512

    wq_all = prep_w_qkv(w_qkv)
    wo_all = w_attn_o.astype(BF16)
    win_all = w_ssm_in.astype(BF16)
    wglu_all = w_glu.astype(BF16)
    wu_all = prep_w_up(w_up)
    wd_all = prep_w_down(w_down)
    cw_all = _ff_split(conv_w, 1)
    cb_all = _ff_split(conv_b.reshape(DEPTH, 1, 2 * D_FF), 1)
    prev_all = jnp.transpose(_ff_split(state_conv, 3), (0, 2, 3, 1, 4))

    rope_p = rope_tables(t, 0)
    rope_s = rope_tables(SUBLANES, PAST_LEN)
    rope_s = tuple(jnp.broadcast_to(r[0:1], (bs, LANES)) for r in rope_s)

    kv_wide = None
    kv_p = [[] for _ in range(N_DIL)]
    kv_s = [[] for _ in range(N_DIL)]
    ssm_p, ssm_s, conv_p, conv_s = [], [], [], []

    for i in range(DEPTH):
        li = i // N_MIXERS
        if i % N_MIXERS == 0:
            assert DIL_GROUPS[-1][0] >= t
            qkv_p, kv_wide = qkv_project(xp, norm_g[i, 0], wq_all, li, rope_p, tm=tm_big,
                                         stack=(w_qkv.shape[0], kv_wide))
            qkv_s = qkv_project(xs, norm_g[i, 0], wq_all, li, rope_s, tm=bs)
            op = attn_prompt(qkv_p.reshape(bp, t, QKV_COLS))
            os_ = attn_sample(qkv_s, caches, li)
            xp = matmul_residual(op.reshape(mp, ATTN_W), wo_all, li, xp, norm_g[i, 1], tm=tm, tn=d, name="attn_out")
            xs = matmul_residual(os_, wo_all, li, xs, norm_g[i, 1], tm=bs, tn=d, name="attn_out")
            for g, (win, _) in enumerate(DIL_GROUPS):
                keep = min(win, t)
                c0 = _kv_col(g)
                if g < N_DIL - 1:
                    kv_p[g].append(qkv_p.reshape(bp, t, QKV_COLS)[:, t - keep:, c0:c0 + KV_COLS]
                                   .reshape(bp, keep, 2, HEADS, HEAD_DIM))
                kv_s[g].append(qkv_s[:, c0:c0 + KV_COLS].reshape(bs, 1, 2, HEADS, HEAD_DIM))
        else:
            a_re, a_im, bb_re, bb_im = ssm_params(lambda_re[li], lambda_im[li], log_dt[li], b_re[li], b_im[li])
            bmat, cmat = _block_diag_mats(bb_re, bb_im, c_re[li], c_im[li])
            half = SSM_SLABS // 2
            up_ = norm_matmul(xp, norm_g[i, 0], win_all, li, tm=tm_big, tn=1024, name="ssm_in")
            us_ = norm_matmul(xs, norm_g[i, 0], win_all, li, tm=bs, tn=1024, name="ssm_in")
            zeros = jnp.zeros((bp, SSM_TILES, half, LANES), F32)
            zp, sp_re, sp_im = ssm_scan(up_.reshape(bp, t, d), bmat, cmat, a_re, a_im, d_skip[li], zeros, zeros)
            h0 = state_ssm[li].reshape(bs, SSM_TILES, SSM_TILE_ST, 2)
            h0_re = jnp.swapaxes(h0[..., 0], 0, 1)
            h0_im = jnp.swapaxes(h0[..., 1], 0, 1)
            zs, ss_re, ss_im = ssm_step(us_, bmat, cmat, a_re, a_im, d_skip[li], h0_re, h0_im)
            xp = matmul_residual(zp.reshape(mp, d), wglu_all, li, xp, norm_g[i, 1], tm=tm, tn=512, gated=True,
                                 name="ssm_out")
            xs = matmul_residual(zs, wglu_all, li, xs, norm_g[i, 1], tm=bs, tn=512, gated=True, name="ssm_out")
            ssm_p.append(jnp.stack([sp_re.reshape(bp, SSM_GROUPS, SSM_STATE),
                                    sp_im.reshape(bp, SSM_GROUPS, SSM_STATE)], axis=-1))
            ssm_s.append(jnp.stack([jnp.swapaxes(ss_re, 0, 1).reshape(bs, SSM_GROUPS, SSM_STATE),
                                    jnp.swapaxes(ss_im, 0, 1).reshape(bs, SSM_GROUPS, SSM_STATE)], axis=-1))

        xp, sg_p, sv_p = conv_ffn(xp, norm_g[i, 2], norm_g[i, 3], wu_all, cw_all, cb_all, wd_all, i,
                                  tm=tm_ffn, tn=tn_ffn, seq_len=t)
        xs, sg_s, sv_s = conv_ffn(xs, norm_g[i, 2], norm_g[i, 3], wu_all, cw_all, cb_all, wd_all, i,
                                  tm=bs, tn=512, prev_all=prev_all)
        tiles = t // tm_ffn
        tail = _ff_join(sg_p, sv_p).reshape(bp, tiles, SUBLANES, 2 * D_FF)
        conv_p.append(tail[:, tiles - 1, SUBLANES - (CONV_WIDTH - 1):])
        conv_s.append(jnp.concatenate([state_conv[i][:, 1:], _ff_join(sg_s, sv_s)[:, None]], axis=1))

    return (xp.reshape(bp, t, d), xs.reshape(bs, 1, d),
            jnp.stack(kv_p[0]), jnp.stack(kv_s[0]),
            jnp.stack(kv_p[1]), jnp.stack(kv_s[1]),
            kv_wide.reshape(w_qkv.shape[0], bp, t, 2, HEADS, HEAD_DIM), jnp.stack(kv_s[2]),
            jnp.stack(ssm_p), jnp.stack(ssm_s),
            jnp.stack(conv_p), jnp.stack(conv_s))
```
